```python
import jax, jax.numpy as jnp
from jax import lax
import numpy as np

D_MODEL = 1024
BATCH = 2
SEQ = 8192
DEPTH = 4
DEC_BATCH = 8
DEC_SEQ = 4096
PAST_LEN = 128

HEAD_DIM = 64
ROPE_THETA = 10000.0
NORM_EPS = 1e-6
D_FF = 2816
N_MEM = 256
MEM_HEADS = 4
MEM_WIDTH = MEM_HEADS * HEAD_DIM
A_Q_HEADS = 12
A_KV_HEADS = 4
A_GROUP = A_Q_HEADS // A_KV_HEADS
WINDOW = 128
BLOCK = 128
A_Q_W = A_Q_HEADS * HEAD_DIM
A_KV_W = A_KV_HEADS * HEAD_DIM
A_IN = A_Q_W + 2 * A_KV_W + MEM_WIDTH
B_HEADS = 12
Q_LORA = 384
KV_LORA = 256
QK_NOPE = 64
QK_ROPE = 32
V_HEAD = 64
B_QK = QK_NOPE + QK_ROPE
B_IN = Q_LORA + KV_LORA + QK_ROPE + MEM_WIDTH
Q_BLOCK = 128
MIX_WIDTH = A_Q_W + MEM_WIDTH
N_A_LAYERS = (DEPTH + 1) // 2
N_B_LAYERS = DEPTH // 2
NEG = -1e30

kernel_name = "hybrid_swa_mla_memory_macaron_encoder"


def rmsnorm(x, g):
    xf = x.astype(jnp.float32)
    y = xf * lax.rsqrt(jnp.mean(xf * xf, axis=-1, keepdims=True) + NORM_EPS)
    return (y * g.astype(jnp.float32)).astype(x.dtype)


def rope_tables(seq, dim):
    inv = 1.0 / (ROPE_THETA ** (jnp.arange(0, dim, 2, dtype=jnp.float32) / dim))
    ang = jnp.arange(seq, dtype=jnp.float32)[:, None] * inv[None, :]
    return jnp.cos(ang), jnp.sin(ang)


def apply_rope(x, cos, sin):
    xf = x.astype(jnp.float32)
    x1, x2 = jnp.split(xf, 2, axis=-1)
    c = cos[None, :, None, :]
    s = sin[None, :, None, :]
    return jnp.concatenate([x1 * c - x2 * s, x2 * c + x1 * s], axis=-1).astype(x.dtype)


def swiglu(x, g, w_gu, w_down):
    gate, up = jnp.split(rmsnorm(x, g) @ w_gu, 2, axis=-1)
    return (jax.nn.silu(gate) * up) @ w_down


def windowed_gqa(q, k, v, sink):
    b, s, _, _ = q.shape
    nb = s // BLOCK
    qb = q.reshape(b, nb, BLOCK, A_KV_HEADS, A_GROUP, HEAD_DIM)

    def band(t):
        tp = jnp.pad(t, ((0, 0), (WINDOW, WINDOW), (0, 0), (0, 0)))
        tb = tp.reshape(b, nb + 2, BLOCK, A_KV_HEADS, HEAD_DIM)
        return jnp.concatenate([tb[:, :-2], tb[:, 1:-1], tb[:, 2:]], axis=2)

    kb, vb = band(k), band(v)
    sc = jnp.einsum('bnqgrd,bnkgd->bngrqk', qb, kb,
                    preferred_element_type=jnp.float32) * (HEAD_DIM ** -0.5)
    qpos = jnp.arange(nb)[:, None] * BLOCK + jnp.arange(BLOCK)[None, :]
    kpos = jnp.arange(nb)[:, None] * BLOCK - WINDOW + jnp.arange(3 * BLOCK)[None, :]
    valid = ((jnp.abs(qpos[:, :, None] - kpos[:, None, :]) <= WINDOW)
             & (kpos >= 0)[:, None, :] & (kpos < s)[:, None, :])
    sc = jnp.where(valid[None, :, None, None], sc, NEG)
    sk = sink.astype(jnp.float32).reshape(A_KV_HEADS, A_GROUP)[None, None, :, :, None, None]
    m = jnp.maximum(jnp.max(sc, axis=-1, keepdims=True), sk)
    p = jnp.exp(sc - m)
    p = p / (jnp.sum(p, axis=-1, keepdims=True) + jnp.exp(sk - m))
    o = jnp.einsum('bngrqk,bnkgd->bnqgrd', p.astype(v.dtype), vb)
    return o.reshape(b, s, A_Q_W)


def dense_attention_blocks(q, k, v):
    b, s, h, dk = q.shape
    dv = v.shape[-1]
    nb = s // Q_BLOCK
    qb = q.reshape(b, nb, Q_BLOCK, h, dk).transpose(1, 0, 2, 3, 4)
    scale = dk ** -0.5

    def one_block(qblk):
        sc = jnp.einsum('bqhd,bkhd->bhqk', qblk, k, preferred_element_type=jnp.float32) * scale
        p = jax.nn.softmax(sc, axis=-1)
        return jnp.einsum('bhqk,bkhd->bqhd', p.astype(v.dtype), v)

    o = lax.map(one_block, qb)
    return o.transpose(1, 0, 2, 3, 4).reshape(b, s, h * dv)


def memory_attention(qc, mk, mv):
    b, s, _ = qc.shape
    q = qc.reshape(b, s, MEM_HEADS, HEAD_DIM)
    k = mk.reshape(b, N_MEM, MEM_HEADS, HEAD_DIM)
    vv = mv.reshape(b, N_MEM, MEM_HEADS, HEAD_DIM)
    sc = jnp.einsum('bqhd,bkhd->bhqk', q, k, preferred_element_type=jnp.float32) * (HEAD_DIM ** -0.5)
    p = jax.nn.softmax(sc, axis=-1)
    return jnp.einsum('bhqk,bkhd->bqhd', p.astype(vv.dtype), vv).reshape(b, s, MEM_WIDTH)


def mixer_a(h, w_in, sink, cos, sin):
    b, s, _ = h.shape
    proj = h @ w_in
    q, k, v, qc = jnp.split(proj, [A_Q_W, A_Q_W + A_KV_W, A_Q_W + 2 * A_KV_W], axis=-1)
    q = apply_rope(q.reshape(b, s, A_Q_HEADS, HEAD_DIM), cos, sin)
    k = apply_rope(k.reshape(b, s, A_KV_HEADS, HEAD_DIM), cos, sin)
    v = v.reshape(b, s, A_KV_HEADS, HEAD_DIM)
    return windowed_gqa(q, k, v, sink), qc


def mixer_b(h, w_in, q_norm, w_q_up, kv_norm, w_kv_up, cos, sin):
    b, s, _ = h.shape
    proj = h @ w_in
    c_q, c_kv, k_r, qc = jnp.split(proj, [Q_LORA, Q_LORA + KV_LORA, Q_LORA + KV_LORA + QK_ROPE], axis=-1)
    q = (rmsnorm(c_q, q_norm) @ w_q_up).reshape(b, s, B_HEADS, B_QK)
    q_nope, q_rope = jnp.split(q, [QK_NOPE], axis=-1)
    q = jnp.concatenate([q_nope, apply_rope(q_rope, cos, sin)], axis=-1)
    kv = (rmsnorm(c_kv, kv_norm) @ w_kv_up).reshape(b, s, B_HEADS, QK_NOPE + V_HEAD)
    k_nope, v = jnp.split(kv, [QK_NOPE], axis=-1)
    k_rope = apply_rope(k_r[:, :, None, :], cos, sin)
    k = jnp.concatenate([k_nope, jnp.broadcast_to(k_rope, (b, s, B_HEADS, QK_ROPE))], axis=-1)
    return dense_attention_blocks(q, k, v), qc


def run_trunk(x, mem, P):
    s = x.shape[1]
    cos_a, sin_a = rope_tables(s, HEAD_DIM)
    cos_b, sin_b = rope_tables(s, QK_ROPE)
    for i in range(DEPTH):
        x = x + 0.5 * swiglu(x, P['ffn1_norm'][i], P['ffn1_w_gu'][i], P['ffn1_w_down'][i])
        h = rmsnorm(x, P['mix_norm'][i])
        mk, mv = jnp.split(rmsnorm(mem, P['mem_norm'][i]) @ P['w_mem_kv'][i], 2, axis=-1)
        j = i // 2
        if i % 2 == 0:
            local, qc = mixer_a(h, P['a_w_in'][j], P['a_sink'][j], cos_a, sin_a)
        else:
            local, qc = mixer_b(h, P['b_w_in'][j], P['b_q_norm'][j], P['b_w_q_up'][j],
                                P['b_kv_norm'][j], P['b_w_kv_up'][j], cos_b, sin_b)
        cross = memory_attention(qc, mk, mv)
        x = x + jnp.concatenate([local, cross], axis=-1) @ P['w_o'][i]
        x = x + 0.5 * swiglu(x, P['ffn2_norm'][i], P['ffn2_w_gu'][i], P['ffn2_w_down'][i])
    return rmsnorm(x, P['final_norm'])


def setup_inputs(seed: int = 0) -> dict:
    key = jax.random.key(seed)
    ks = iter(jax.random.split(key, 32))

    def w(shape, fan_in):
        return jax.random.normal(next(ks), shape, jnp.float32) * (fan_in ** -0.5)

    def gain(shape):
        return 1.0 + 0.1 * jax.random.normal(next(ks), shape, jnp.float32)

    return {
        'x_prompt': jax.random.normal(next(ks), (BATCH, SEQ, D_MODEL), jnp.float32),
        'x_sample': jax.random.normal(next(ks), (DEC_BATCH, DEC_SEQ, D_MODEL), jnp.float32),
        'mem_prompt': jax.random.normal(next(ks), (BATCH, N_MEM, D_MODEL), jnp.float32),
        'mem_sample': jax.random.normal(next(ks), (DEC_BATCH, N_MEM, D_MODEL), jnp.float32),
        'ffn1_norm': gain((DEPTH, D_MODEL)),
        'ffn1_w_gu': w((DEPTH, D_MODEL, 2 * D_FF), D_MODEL),
        'ffn1_w_down': w((DEPTH, D_FF, D_MODEL), D_FF),
        'mix_norm': gain((DEPTH, D_MODEL)),
        'mem_norm': gain((DEPTH, D_MODEL)),
        'w_mem_kv': w((DEPTH, D_MODEL, 2 * MEM_WIDTH), D_MODEL),
        'a_w_in': w((N_A_LAYERS, D_MODEL, A_IN), D_MODEL),
        'a_sink': 0.5 * jax.random.normal(next(ks), (N_A_LAYERS, A_Q_HEADS), jnp.float32),
        'b_w_in': w((N_B_LAYERS, D_MODEL, B_IN), D_MODEL),
        'b_q_norm': gain((N_B_LAYERS, Q_LORA)),
        'b_w_q_up': w((N_B_LAYERS, Q_LORA, B_HEADS * B_QK), Q_LORA),
        'b_kv_norm': gain((N_B_LAYERS, KV_LORA)),
        'b_w_kv_up': w((N_B_LAYERS, KV_LORA, B_HEADS * (QK_NOPE + V_HEAD)), KV_LORA),
        'w_o': w((DEPTH, MIX_WIDTH, D_MODEL), MIX_WIDTH),
        'ffn2_norm': gain((DEPTH, D_MODEL)),
        'ffn2_w_gu': w((DEPTH, D_MODEL, 2 * D_FF), D_MODEL),
        'ffn2_w_down': w((DEPTH, D_FF, D_MODEL), D_FF),
        'final_norm': gain((D_MODEL,)),
    }


def reference(x_prompt, x_sample, mem_prompt, mem_sample,
              ffn1_norm, ffn1_w_gu, ffn1_w_down, mix_norm, mem_norm, w_mem_kv,
              a_w_in, a_sink, b_w_in, b_q_norm, b_w_q_up, b_kv_norm, b_w_kv_up,
              w_o, ffn2_norm, ffn2_w_gu, ffn2_w_down, final_norm):
    P = {
        'ffn1_norm': ffn1_norm, 'ffn1_w_gu': ffn1_w_gu, 'ffn1_w_down': ffn1_w_down,
        'mix_norm': mix_norm, 'mem_norm': mem_norm, 'w_mem_kv': w_mem_kv,
        'a_w_in': a_w_in, 'a_sink': a_sink,
        'b_w_in': b_w_in, 'b_q_norm': b_q_norm, 'b_w_q_up': b_w_q_up,
        'b_kv_norm': b_kv_norm, 'b_w_kv_up': b_w_kv_up,
        'w_o': w_o, 'ffn2_norm': ffn2_norm, 'ffn2_w_gu': ffn2_w_gu,
        'ffn2_w_down': ffn2_w_down, 'final_norm': final_norm,
    }
    y_prompt = run_trunk(x_prompt, mem_prompt, P)
    y_sample = run_trunk(x_sample, mem_sample, P)
    return (y_prompt, y_sample)
```

```python
import functools
import math

import jax
import jax.numpy as jnp
from jax import lax
from jax.experimental import pallas as pl
from jax.experimental.pallas import tpu as pltpu

D_MODEL = 1024
HEAD_DIM = 64
ROPE_THETA = 10000.0
NORM_EPS = 1e-6
D_FF = 2816
N_MEM = 256
MEM_HEADS = 4
MEM_WIDTH = MEM_HEADS * HEAD_DIM
A_Q_HEADS = 12
A_KV_HEADS = 4
A_GROUP = A_Q_HEADS // A_KV_HEADS
WINDOW = 128
A_Q_W = A_Q_HEADS * HEAD_DIM
A_KV_W = A_KV_HEADS * HEAD_DIM
B_HEADS = 12
Q_LORA = 384
KV_LORA = 256
QK_NOPE = 64
QK_ROPE = 32
V_HEAD = 64
B_QK = QK_NOPE + QK_ROPE
LOCAL_W = A_Q_W
NEG = -1e30
LOG2E = math.log2(math.e)

LANES = 128
B_HEAD_PAD = LANES
FF_CHUNK = 256
VMEM_LIMIT = 56 * 1024 * 1024

F32 = jnp.float32
BF16 = jnp.bfloat16


def _params(n_axes):
    return pltpu.CompilerParams(dimension_semantics=("arbitrary",) * n_axes,
                                vmem_limit_bytes=VMEM_LIMIT)


def _rms(x, g):
    return x * lax.rsqrt(jnp.mean(x * x, axis=-1, keepdims=True) + NORM_EPS) * g


def _dot(a, b):
    return jnp.dot(a, b, preferred_element_type=F32)


def _dot_nt(a, b):
    return lax.dot_general(a, b, (((1,), (1,)), ((), ())), preferred_element_type=F32)


def _resident(shape):
    zeros = (0,) * len(shape)
    return pl.BlockSpec(shape, lambda *_: zeros, pipeline_mode=pl.Buffered(1))


def _ffn_body(*refs, nch, final):
    if final:
        x_ref, g_ref, wg_ref, wu_ref, wd_ref, gf_ref, o_ref, h_ref, acc_ref = refs
    else:
        x_ref, g_ref, wg_ref, wu_ref, wd_ref, o_ref, h_ref, acc_ref = refs
    h_ref[...] = _rms(x_ref[...], g_ref[...]).astype(BF16)
    acc_ref[...] = jnp.zeros_like(acc_ref)

    def chunk(c, carry):
        h = h_ref[...]
        gate = _dot(h, wg_ref[c])
        up = _dot(h, wu_ref[c])
        a = (gate / (1.0 + jnp.exp(-gate)) * up).astype(BF16)
        acc_ref[...] += _dot(a, wd_ref[c])
        return carry

    lax.fori_loop(0, nch, chunk, 0)
    y = x_ref[...] + 0.5 * acc_ref[...]
    if final:
        y = _rms(y, gf_ref[...])
    o_ref[...] = y


def _ffn(x, g, wg, wu, wd, g_final=None):
    t, d = x.shape
    nch, _, fc = wg.shape
    tm = min(512, t)
    final = g_final is not None
    in_specs = [
        pl.BlockSpec((tm, d), lambda i: (i, 0)),
        _resident((1, d)),
        _resident(wg.shape),
        _resident(wu.shape),
        _resident(wd.shape),
    ]
    args = [x, g.reshape(1, d), wg, wu, wd]
    if final:
        in_specs.append(_resident((1, d)))
        args.append(g_final.reshape(1, d))
    return pl.pallas_call(
        functools.partial(_ffn_body, nch=nch, final=final),
        grid=(t // tm,),
        in_specs=in_specs,
        out_specs=pl.BlockSpec((tm, d), lambda i: (i, 0)),
        out_shape=jax.ShapeDtypeStruct((t, d), F32),
        scratch_shapes=[pltpu.VMEM((tm, d), BF16), pltpu.VMEM((tm, d), F32)],
        compiler_params=_params(1),
        name="ffn_final" if final else "ffn",
    )(*args)


def _memkv_body(mem_ref, g_ref, w_ref, mk_ref, mv_ref):
    h = _rms(mem_ref[...], g_ref[...]).astype(BF16)
    kv = _dot(h, w_ref[...])
    mk_ref[...] = kv[:, :MEM_WIDTH].astype(BF16)
    mv_ref[...] = kv[:, MEM_WIDTH:].astype(BF16)


def _mem_kv(mem, g, w):
    b, n, d = mem.shape
    out = jax.ShapeDtypeStruct((b, n, MEM_WIDTH), BF16)
    blk = pl.BlockSpec((None, n, MEM_WIDTH), lambda i: (i, 0, 0))
    return pl.pallas_call(
        _memkv_body,
        grid=(b,),
        in_specs=[pl.BlockSpec((None, n, d), lambda i: (i, 0, 0)), _resident((1, d)), _resident(w.shape)],
        out_specs=[blk, blk],
        out_shape=[out, out],
        compiler_params=_params(1),
        name="mem_kv",
    )(mem, g.reshape(1, d), w)


def _rope_block(xb, tab_ref, shift):
    c = tab_ref[:, 0:LANES]
    s_plus = tab_ref[:, LANES:2 * LANES]
    s_minus = tab_ref[:, 2 * LANES:3 * LANES]
    return (xb * c + pltpu.roll(xb, shift, 1) * s_plus
            + pltpu.roll(xb, LANES - shift, 1) * s_minus)


def _rope_table(seq, dim, lane_start, period, scale, pass_through):
    half = dim // 2
    inv = 1.0 / (ROPE_THETA ** (jnp.arange(0, dim, 2, dtype=F32) / dim))
    ang = jnp.arange(seq, dtype=F32)[:, None] * inv[None, :]
    cos, sin = jnp.cos(ang), jnp.sin(ang)
    zeros_h = jnp.zeros((seq, half), F32)
    lead = jnp.full((seq, lane_start), 1.0 if pass_through else 0.0, F32)
    lead0 = jnp.zeros((seq, lane_start), F32)
    tail0 = jnp.zeros((seq, period - lane_start - dim), F32)
    reps = LANES // period
    c = jnp.tile(jnp.concatenate([lead, cos, cos, tail0], -1), (1, reps))
    s_plus = jnp.tile(jnp.concatenate([lead0, zeros_h, sin, tail0], -1), (1, reps))
    s_minus = jnp.tile(jnp.concatenate([lead0, -sin, zeros_h, tail0], -1), (1, reps))
    return jnp.concatenate([c, s_plus, s_minus], -1) * scale


def _proja_body(x_ref, g_ref, w_ref, tab_ref, q_ref, k_ref, v_ref, qc_ref, *, qscale):
    h = _rms(x_ref[...], g_ref[...]).astype(BF16)
    proj = _dot(h, w_ref[...])
    nq = A_Q_W // LANES
    nk = A_KV_W // LANES
    for j in range(nq + nk):
        rb = _rope_block(proj[:, LANES * j:LANES * (j + 1)], tab_ref, HEAD_DIM // 2)
        if j < nq:
            q_ref[:, LANES * j:LANES * (j + 1)] = (rb * qscale).astype(BF16)
        else:
            k_ref[:, LANES * (j - nq):LANES * (j - nq + 1)] = rb.astype(BF16)
    v0 = A_Q_W + A_KV_W
    v_ref[...] = proj[:, v0:v0 + A_KV_W].astype(BF16)
    qc_ref[...] = (proj[:, v0 + A_KV_W:] * qscale).astype(BF16)


def _proj_a(x, g, w, tab):
    b, s, d = x.shape
    tm = min(512, s)

    def out(width):
        return (jax.ShapeDtypeStruct((b, s, width), BF16),
                pl.BlockSpec((None, tm, width), lambda i, j: (i, j, 0)))

    shapes, specs = zip(out(A_Q_W), out(A_KV_W), out(A_KV_W), out(MEM_WIDTH))
    return pl.pallas_call(
        functools.partial(_proja_body, qscale=HEAD_DIM ** -0.5 * LOG2E),
        grid=(b, s // tm),
        in_specs=[
            pl.BlockSpec((None, tm, d), lambda i, j: (i, j, 0)),
            _resident((1, d)),
            _resident(w.shape),
            pl.BlockSpec((tm, 3 * LANES), lambda i, j: (j, 0)),
        ],
        out_specs=list(specs),
        out_shape=list(shapes),
        compiler_params=_params(2),
        name="proj_a",
    )(x, g.reshape(1, d), w, tab)


def _wattn_body(sink_ref, q_ref, kp_ref, kc_ref, kn_ref, vp_ref, vc_ref, vn_ref, o_ref, *, nb):
    n = pl.program_id(1)
    qi = lax.broadcasted_iota(jnp.int32, (WINDOW, 3 * WINDOW), 0)
    kj = lax.broadcasted_iota(jnp.int32, (WINDOW, 3 * WINDOW), 1)
    dist = kj - qi
    lo = jnp.where(n > 0, 0, WINDOW)
    hi = jnp.where(n < nb - 1, 3 * WINDOW, 2 * WINDOW)
    mask = (dist >= 0) & (dist <= 2 * WINDOW) & (kj >= lo) & (kj < hi)
    for g in range(A_KV_HEADS):
        cs = slice(HEAD_DIM * g, HEAD_DIM * (g + 1))
        kcat = jnp.concatenate([kp_ref[:, cs], kc_ref[:, cs], kn_ref[:, cs]], axis=0)
        vcat = jnp.concatenate([vp_ref[:, cs], vc_ref[:, cs], vn_ref[:, cs]], axis=0)
        for r in range(A_GROUP):
            hd = A_GROUP * g + r
            hs = slice(HEAD_DIM * hd, HEAD_DIM * (hd + 1))
            s = jnp.where(mask, _dot_nt(q_ref[:, hs], kcat), NEG)
            sk = sink_ref[hd] * LOG2E
            m = jnp.maximum(jnp.max(s, axis=-1, keepdims=True), sk)
            p = jnp.exp2(s - m)
            den = jnp.sum(p, axis=-1, keepdims=True) + jnp.exp2(sk - m)
            o = _dot(p.astype(BF16), vcat) / den
            o_ref[:, hs] = o.astype(BF16)


def _win_attn(q, k, v, sink):
    b, s, _ = q.shape
    nb = s // WINDOW

    def kv_spec(shift):
        return pl.BlockSpec((None, WINDOW, A_KV_W),
                            lambda i, j: (i, jnp.clip(j + shift, 0, nb - 1), 0))

    return pl.pallas_call(
        functools.partial(_wattn_body, nb=nb),
        grid=(b, nb),
        in_specs=[
            pl.BlockSpec(memory_space=pltpu.SMEM),
            pl.BlockSpec((None, WINDOW, A_Q_W), lambda i, j: (i, j, 0)),
            kv_spec(-1), kv_spec(0), kv_spec(1),
            kv_spec(-1), kv_spec(0), kv_spec(1),
        ],
        out_specs=pl.BlockSpec((None, WINDOW, A_Q_W), lambda i, j: (i, j, 0)),
        out_shape=jax.ShapeDtypeStruct((b, s, A_Q_W), BF16),
        compiler_params=_params(2),
        name="win_attn",
    )(sink, q, k, k, k, v, v, v)


def _projb_body(x_ref, g_ref, win_ref, gq_ref, wq_ref, gkv_ref, wk_ref, wv_ref, tabq_ref, tabk_ref,
                q_ref, k_ref, v_ref, qc_ref, *, qcscale):
    h = _rms(x_ref[...], g_ref[...]).astype(BF16)
    proj = _dot(h, win_ref[...])
    c_q = _rms(proj[:, :Q_LORA], gq_ref[...]).astype(BF16)
    kv0 = Q_LORA
    c_kv = _rms(proj[:, kv0:kv0 + KV_LORA], gkv_ref[...]).astype(BF16)
    qc0 = kv0 + KV_LORA
    qc_ref[...] = (proj[:, qc0:qc0 + MEM_WIDTH] * qcscale).astype(BF16)
    kr0 = qc0 + MEM_WIDTH
    k_rope = _rope_block(proj[:, kr0:kr0 + LANES], tabk_ref, QK_ROPE // 2)
    q_all = _dot(c_q, wq_ref[...])
    k_all = _dot(c_kv, wk_ref[...])
    for hd in range(B_HEADS):
        hs = slice(B_HEAD_PAD * hd, B_HEAD_PAD * (hd + 1))
        q_ref[:, hs] = _rope_block(q_all[:, hs], tabq_ref, QK_ROPE // 2).astype(BF16)
        k_ref[:, hs] = (k_all[:, hs] + k_rope).astype(BF16)
    v_ref[...] = _dot(c_kv, wv_ref[...]).astype(BF16)


def _proj_b(x, g, w_in, gq, wq, gkv, wk, wv, tabq, tabk):
    b, s, d = x.shape
    tm = min(512, s)

    def out(width):
        return (jax.ShapeDtypeStruct((b, s, width), BF16),
                pl.BlockSpec((None, tm, width), lambda i, j: (i, j, 0)))

    shapes, specs = zip(out(B_HEADS * B_HEAD_PAD), out(B_HEADS * B_HEAD_PAD),
                        out(B_HEADS * V_HEAD), out(MEM_WIDTH))
    tab_spec = pl.BlockSpec((tm, 3 * LANES), lambda i, j: (j, 0))
    return pl.pallas_call(
        functools.partial(_projb_body, qcscale=HEAD_DIM ** -0.5 * LOG2E),
        grid=(b, s // tm),
        in_specs=[
            pl.BlockSpec((None, tm, d), lambda i, j: (i, j, 0)),
            _resident((1, d)),
            _resident(w_in.shape),
            _resident((1, Q_LORA)),
            _resident(wq.shape),
            _resident((1, KV_LORA)),
            _resident(wk.shape),
            _resident(wv.shape),
            tab_spec, tab_spec,
        ],
        out_specs=list(specs),
        out_shape=list(shapes),
        compiler_params=_params(2),
        name="proj_b",
    )(x, g.reshape(1, d), w_in, gq.reshape(1, -1), wq, gkv.reshape(1, -1), wk, wv, tabq, tabk)


def _mla_body(q_ref, k_ref, v_ref, o_ref, *, kb, nkv):
    qb = q_ref.shape[0]
    outs = []
    for hh in range(2):
        hs = slice(B_HEAD_PAD * hh, B_HEAD_PAD * (hh + 1))
        qh = q_ref[:, hs]

        def tile(t, carry, hs=hs, qh=qh):
            m, l, acc = carry
            rows = pl.ds(pl.multiple_of(t * kb, kb), kb)
            s = _dot_nt(qh, k_ref[rows, hs])
            m_new = jnp.maximum(m, jnp.max(s, axis=-1, keepdims=True))
            alpha = jnp.exp2(m - m_new)
            p = jnp.exp2(s - m_new)
            l = alpha * l + jnp.sum(p, axis=-1, keepdims=True)
            acc = alpha * acc + _dot(p.astype(BF16), v_ref[rows, :])
            return m_new, l, acc

        init = (jnp.full((qb, 1), NEG, F32), jnp.zeros((qb, 1), F32), jnp.zeros((qb, LANES), F32))
        _, l, acc = lax.fori_loop(0, nkv, tile, init)
        outs.append(acc / l)
    lane = lax.broadcasted_iota(jnp.int32, (qb, LANES), 1)
    o_ref[...] = jnp.where(lane < V_HEAD, outs[0], outs[1]).astype(BF16)


def _mla_attn(q, k, v):
    b, s, _ = q.shape
    qb = min(256, s)
    kb = min(512, s)
    npair = B_HEADS // 2
    return pl.pallas_call(
        functools.partial(_mla_body, kb=kb, nkv=s // kb),
        grid=(b, npair, s // qb),
        in_specs=[
            pl.BlockSpec((None, qb, 2 * B_HEAD_PAD), lambda i, j, n: (i, n, j)),
            pl.BlockSpec((None, s, 2 * B_HEAD_PAD), lambda i, j, n: (i, 0, j)),
            pl.BlockSpec((None, s, 2 * V_HEAD), lambda i, j, n: (i, 0, j)),
        ],
        out_specs=pl.BlockSpec((None, qb, 2 * V_HEAD), lambda i, j, n: (i, n, j)),
        out_shape=jax.ShapeDtypeStruct((b, s, B_HEADS * V_HEAD), BF16),
        compiler_params=_params(3),
        name="mla_attn",
    )(q, k, v)


def _oproj_body(x_ref, loc_ref, qc_ref, mk_ref, mv_ref, wo_ref, o_ref):
    heads = []
    for hd in range(MEM_HEADS):
        hs = slice(HEAD_DIM * hd, HEAD_DIM * (hd + 1))
        s = _dot_nt(qc_ref[:, hs], mk_ref[:, hs])
        p = jnp.exp2(s - jnp.max(s, axis=-1, keepdims=True))
        o = _dot(p.astype(BF16), mv_ref[:, hs]) / jnp.sum(p, axis=-1, keepdims=True)
        heads.append(o.astype(BF16))
    cross = jnp.concatenate(heads, axis=-1)
    y = _dot(loc_ref[...], wo_ref[:LOCAL_W, :]) + _dot(cross, wo_ref[LOCAL_W:, :])
    o_ref[...] = x_ref[...] + y


def _out_proj(x, local, qc, mk, mv, wo):
    b, s, d = x.shape
    tm = min(512, s)

    def tok(width):
        return pl.BlockSpec((None, tm, width), lambda i, j: (i, j, 0))

    mem = pl.BlockSpec((None, N_MEM, MEM_WIDTH), lambda i, j: (i, 0, 0))
    return pl.pallas_call(
        _oproj_body,
        grid=(b, s // tm),
        in_specs=[tok(d), tok(LOCAL_W), tok(MEM_WIDTH), mem, mem, _resident(wo.shape)],
        out_specs=tok(d),
        out_shape=jax.ShapeDtypeStruct((b, s, d), F32),
        compiler_params=_params(2),
        name="out_proj",
    )(x, local, qc, mk, mv, wo)


def _prep_ffn(w_gu, w_down):
    depth, d, _ = w_gu.shape
    nch = D_FF // FF_CHUNK
    w = w_gu.astype(BF16).reshape(depth, d, 2, nch, FF_CHUNK)
    wg = jnp.transpose(w[:, :, 0], (0, 2, 1, 3))
    wu = jnp.transpose(w[:, :, 1], (0, 2, 1, 3))
    wd = w_down.astype(BF16).reshape(depth, nch, FF_CHUNK, d)
    return wg, wu, wd


def _prep_b(b_w_in, b_w_q_up, b_w_kv_up):
    nl, d, _ = b_w_in.shape
    c_q = b_w_in[:, :, :Q_LORA]
    c_kv = b_w_in[:, :, Q_LORA:Q_LORA + KV_LORA]
    k_r = b_w_in[:, :, Q_LORA + KV_LORA:Q_LORA + KV_LORA + QK_ROPE]
    qc = b_w_in[:, :, Q_LORA + KV_LORA + QK_ROPE:]
    kr_tile = jnp.pad(k_r, ((0, 0), (0, 0), (QK_NOPE, LANES - QK_NOPE - QK_ROPE)))
    w_in = jnp.concatenate([c_q, c_kv, qc, kr_tile], axis=-1).astype(BF16)
    wq = b_w_q_up.reshape(nl, Q_LORA, B_HEADS, B_QK)
    wq = jnp.pad(wq, ((0, 0), (0, 0), (0, 0), (0, B_HEAD_PAD - B_QK)))
    wq = wq.reshape(nl, Q_LORA, B_HEADS * B_HEAD_PAD).astype(BF16)
    wkv = b_w_kv_up.reshape(nl, KV_LORA, B_HEADS, QK_NOPE + V_HEAD)
    wk = jnp.pad(wkv[..., :QK_NOPE], ((0, 0), (0, 0), (0, 0), (0, B_HEAD_PAD - QK_NOPE)))
    wk = wk.reshape(nl, KV_LORA, B_HEADS * B_HEAD_PAD).astype(BF16)
    wv = wkv[..., QK_NOPE:].reshape(nl, KV_LORA, B_HEADS * V_HEAD).astype(BF16)
    return w_in, wq, wk, wv


def _trunk(x, mem, w):
    b, s, d = x.shape
    depth = w["mix_norm"].shape[0]
    tab_a = _rope_table(s, HEAD_DIM, 0, HEAD_DIM, 1.0, False)
    tab_bq = _rope_table(s, QK_ROPE, QK_NOPE, LANES, B_QK ** -0.5 * LOG2E, True)
    tab_bk = _rope_table(s, QK_ROPE, QK_NOPE, LANES, 1.0, False)
    for i in range(depth):
        x = _ffn(x.reshape(b * s, d), w["ffn1_norm"][i], w["ffn1_wg"][i], w["ffn1_wu"][i],
                 w["ffn1_wd"][i]).reshape(b, s, d)
        mk, mv = _mem_kv(mem, w["mem_norm"][i], w["w_mem_kv"][i])
        j = i // 2
        if i % 2 == 0:
            q, k, v, qc = _proj_a(x, w["mix_norm"][i], w["a_w_in"][j], tab_a)
            local = _win_attn(q, k, v, w["a_sink"][j])
        else:
            q, k, v, qc = _proj_b(x, w["mix_norm"][i], w["b_w_in"][j], w["b_q_norm"][j], w["b_wq"][j],
                                  w["b_kv_norm"][j], w["b_wk"][j], w["b_wv"][j], tab_bq, tab_bk)
            local = _mla_attn(q, k, v)
        x = _out_proj(x, local, qc, mk, mv, w["w_o"][i])
        g_final = w["final_norm"] if i == depth - 1 else None
        x = _ffn(x.reshape(b * s, d), w["ffn2_norm"][i], w["ffn2_wg"][i], w["ffn2_wu"][i],
                 w["ffn2_wd"][i], g_final).reshape(b, s, d)
    return x


def kernel(x_prompt, x_sample, mem_prompt, mem_sample, ffn1_norm, ffn1_w_gu, ffn1_w_down, mix_norm,
           mem_norm, w_mem_kv, a_w_in, a_sink, b_w_in, b_q_norm, b_w_q_up, b_kv_norm, b_w_kv_up,
           w_o, ffn2_norm, ffn2_w_gu, ffn2_w_down, final_norm):
    w = {
        "ffn1_norm": ffn1_norm, "ffn2_norm": ffn2_norm, "mix_norm": mix_norm, "mem_norm": mem_norm,
        "w_mem_kv": w_mem_kv.astype(BF16), "a_w_in": a_w_in.astype(BF16), "a_sink": a_sink,
        "b_q_norm": b_q_norm, "b_kv_norm": b_kv_norm, "w_o": w_o.astype(BF16),
        "final_norm": final_norm,
    }
    w["ffn1_wg"], w["ffn1_wu"], w["ffn1_wd"] = _prep_ffn(ffn1_w_gu, ffn1_w_down)
    w["ffn2_wg"], w["ffn2_wu"], w["ffn2_wd"] = _prep_ffn(ffn2_w_gu, ffn2_w_down)
    w["b_w_in"], w["b_wq"], w["b_wk"], w["b_wv"] = _prep_b(b_w_in, b_w_q_up, b_w_kv_up)
    return (_trunk(x_prompt, mem_prompt, w), _trunk(x_sample, mem_sample, w))
```

```python
import functools
import math

import jax
import jax.numpy as jnp
from jax import lax
from jax.experimental import pallas as pl
from jax.experimental.pallas import tpu as pltpu

D_MODEL = 1024
HEAD_DIM = 64
ROPE_THETA = 10000.0
NORM_EPS = 1e-6
D_FF = 2816
N_MEM = 256
MEM_HEADS = 4
MEM_WIDTH = MEM_HEADS * HEAD_DIM
A_Q_HEADS = 12
A_KV_HEADS = 4
A_GROUP = A_Q_HEADS // A_KV_HEADS
WINDOW = 128
A_Q_W = A_Q_HEADS * HEAD_DIM
A_KV_W = A_KV_HEADS * HEAD_DIM
B_HEADS = 12
Q_LORA = 384
KV_LORA = 256
QK_NOPE = 64
QK_ROPE = 32
V_HEAD = 64
B_QK = QK_NOPE + QK_ROPE
LOCAL_W = A_Q_W
NEG = -1e30
LOG2E = math.log2(math.e)

LANES = 128
B_HEAD_PAD = LANES
FF_CHUNK = 256
VMEM_LIMIT = 56 * 1024 * 1024

F32 = jnp.float32
BF16 = jnp.bfloat16


def _params(n_axes):
    return pltpu.CompilerParams(dimension_semantics=("arbitrary",) * n_axes,
                                vmem_limit_bytes=VMEM_LIMIT)


def _rms(x, g):
    return x * lax.rsqrt(jnp.mean(x * x, axis=-1, keepdims=True) + NORM_EPS) * g


def _dot(a, b):
    return jnp.dot(a, b, preferred_element_type=F32)


def _dot_nt(a, b):
    return lax.dot_general(a, b, (((1,), (1,)), ((), ())), preferred_element_type=F32)


def _resident(shape):
    zeros = (0,) * len(shape)
    return pl.BlockSpec(shape, lambda *_: zeros, pipeline_mode=pl.Buffered(1))


def _ffn_body(*refs, nch, final):
    if final:
        x_ref, g_ref, wg_ref, wu_ref, wd_ref, gf_ref, o_ref, h_ref, acc_ref = refs
    else:
        x_ref, g_ref, wg_ref, wu_ref, wd_ref, o_ref, h_ref, acc_ref = refs
    h_ref[...] = _rms(x_ref[...], g_ref[...]).astype(BF16)
    acc_ref[...] = jnp.zeros_like(acc_ref)

    def chunk(c, carry):
        h = h_ref[...]
        gate = _dot(h, wg_ref[c])
        up = _dot(h, wu_ref[c])
        a = (gate / (1.0 + jnp.exp(-gate)) * up).astype(BF16)
        acc_ref[...] += _dot(a, wd_ref[c])
        return carry

    lax.fori_loop(0, nch, chunk, 0)
    y = x_ref[...] + 0.5 * acc_ref[...]
    if final:
        y = _rms(y, gf_ref[...])
    o_ref[...] = y


def _ffn(x, g, wg, wu, wd, g_final=None):
    t, d = x.shape
    nch, _, fc = wg.shape
    tm = min(512, t)
    final = g_final is not None
    in_specs = [
        pl.BlockSpec((tm, d), lambda i: (i, 0)),
        _resident((1, d)),
        _resident(wg.shape),
        _resident(wu.shape),
        _resident(wd.shape),
    ]
    args = [x, g.reshape(1, d), wg, wu, wd]
    if final:
        in_specs.append(_resident((1, d)))
        args.append(g_final.reshape(1, d))
    return pl.pallas_call(
        functools.partial(_ffn_body, nch=nch, final=final),
        grid=(t // tm,),
        in_specs=in_specs,
        out_specs=pl.BlockSpec((tm, d), lambda i: (i, 0)),
        out_shape=jax.ShapeDtypeStruct((t, d), F32),
        scratch_shapes=[pltpu.VMEM((tm, d), BF16), pltpu.VMEM((tm, d), F32)],
        compiler_params=_params(1),
        name="ffn_final" if final else "ffn",
    )(*args)


def _memkv_body(mem_ref, g_ref, w_ref, mk_ref, mv_ref):
    h = _rms(mem_ref[...], g_ref[...]).astype(BF16)
    kv = _dot(h, w_ref[...])
    mk_ref[...] = kv[:, :MEM_WIDTH].astype(BF16)
    mv_ref[...] = kv[:, MEM_WIDTH:].astype(BF16)


def _mem_kv(mem, g, w):
    b, n, d = mem.shape
    out = jax.ShapeDtypeStruct((b, n, MEM_WIDTH), BF16)
    blk = pl.BlockSpec((None, n, MEM_WIDTH), lambda i: (i, 0, 0))
    return pl.pallas_call(
        _memkv_body,
        grid=(b,),
        in_specs=[pl.BlockSpec((None, n, d), lambda i: (i, 0, 0)), _resident((1, d)), _resident(w.shape)],
        out_specs=[blk, blk],
        out_shape=[out, out],
        compiler_params=_params(1),
        name="mem_kv",
    )(mem, g.reshape(1, d), w)


def _rope_block(xb, tab_ref, shift):
    c = tab_ref[:, 0:LANES]
    s_plus = tab_ref[:, LANES:2 * LANES]
    s_minus = tab_ref[:, 2 * LANES:3 * LANES]
    return (xb * c + pltpu.roll(xb, shift, 1) * s_plus
            + pltpu.roll(xb, LANES - shift, 1) * s_minus)


def _rope_table(seq, dim, lane_start, period, scale, pass_through):
    half = dim // 2
    inv = 1.0 / (ROPE_THETA ** (jnp.arange(0, dim, 2, dtype=F32) / dim))
    ang = jnp.arange(seq, dtype=F32)[:, None] * inv[None, :]
    cos, sin = jnp.cos(ang), jnp.sin(ang)
    zeros_h = jnp.zeros((seq, half), F32)
    lead = jnp.full((seq, lane_start), 1.0 if pass_through else 0.0, F32)
    lead0 = jnp.zeros((seq, lane_start), F32)
    tail0 = jnp.zeros((seq, period - lane_start - dim), F32)
    reps = LANES // period
    c = jnp.tile(jnp.concatenate([lead, cos, cos, tail0], -1), (1, reps))
    s_plus = jnp.tile(jnp.concatenate([lead0, zeros_h, sin, tail0], -1), (1, reps))
    s_minus = jnp.tile(jnp.concatenate([lead0, -sin, zeros_h, tail0], -1), (1, reps))
    return jnp.concatenate([c, s_plus, s_minus], -1) * scale


def _proja_body(x_ref, g_ref, w_ref, tab_ref, q_ref, k_ref, v_ref, qc_ref, *, qscale):
    h = _rms(x_ref[...], g_ref[...]).astype(BF16)
    proj = _dot(h, w_ref[...])
    nq = A_Q_W // LANES
    nk = A_KV_W // LANES
    for j in range(nq + nk):
        rb = _rope_block(proj[:, LANES * j:LANES * (j + 1)], tab_ref, HEAD_DIM // 2)
        if j < nq:
            q_ref[:, LANES * j:LANES * (j + 1)] = (rb * qscale).astype(BF16)
        else:
            k_ref[:, LANES * (j - nq):LANES * (j - nq + 1)] = rb.astype(BF16)
    v0 = A_Q_W + A_KV_W
    v_ref[...] = proj[:, v0:v0 + A_KV_W].astype(BF16)
    qc_ref[...] = (proj[:, v0 + A_KV_W:] * qscale).astype(BF16)


def _proj_a(x, g, w, tab):
    b, s, d = x.shape
    tm = min(512, s)

    def out(width):
        return (jax.ShapeDtypeStruct((b, s, width), BF16),
                pl.BlockSpec((None, tm, width), lambda i, j: (i, j, 0)))

    shapes, specs = zip(out(A_Q_W), out(A_KV_W), out(A_KV_W), out(MEM_WIDTH))
    return pl.pallas_call(
        functools.partial(_proja_body, qscale=HEAD_DIM ** -0.5 * LOG2E),
        grid=(b, s // tm),
        in_specs=[
            pl.BlockSpec((None, tm, d), lambda i, j: (i, j, 0)),
            _resident((1, d)),
            _resident(w.shape),
            pl.BlockSpec((tm, 3 * LANES), lambda i, j: (j, 0)),
        ],
        out_specs=list(specs),
        out_shape=list(shapes),
        compiler_params=_params(2),
        name="proj_a",
    )(x, g.reshape(1, d), w, tab)


def _wattn_body(sink_ref, q_ref, kp_ref, kc_ref, kn_ref, vp_ref, vc_ref, vn_ref, o_ref, *, nb):
    n = pl.program_id(1)
    qi = lax.broadcasted_iota(jnp.int32, (WINDOW, 3 * WINDOW), 0)
    kj = lax.broadcasted_iota(jnp.int32, (WINDOW, 3 * WINDOW), 1)
    dist = kj - qi
    lo = jnp.where(n > 0, 0, WINDOW)
    hi = jnp.where(n < nb - 1, 3 * WINDOW, 2 * WINDOW)
    mask = (dist >= 0) & (dist <= 2 * WINDOW) & (kj >= lo) & (kj < hi)
    for g in range(A_KV_HEADS):
        cs = slice(HEAD_DIM * g, HEAD_DIM * (g + 1))
        kcat = jnp.concatenate([kp_ref[:, cs], kc_ref[:, cs], kn_ref[:, cs]], axis=0)
        vcat = jnp.concatenate([vp_ref[:, cs], vc_ref[:, cs], vn_ref[:, cs]], axis=0)
        for r in range(A_GROUP):
            hd = A_GROUP * g + r
            hs = slice(HEAD_DIM * hd, HEAD_DIM * (hd + 1))
            s = jnp.where(mask, _dot_nt(q_ref[:, hs], kcat), NEG)
            sk = sink_ref[hd] * LOG2E
            m = jnp.maximum(jnp.max(s, axis=-1, keepdims=True), sk)
            p = jnp.exp2(s - m)
            den = jnp.sum(p, axis=-1, keepdims=True) + jnp.exp2(sk - m)
            o = _dot(p.astype(BF16), vcat) / den
            o_ref[:, hs] = o.astype(BF16)


def _win_attn(q, k, v, sink):
    b, s, _ = q.shape
    nb = s // WINDOW

    def kv_spec(shift):
        return pl.BlockSpec((None, WINDOW, A_KV_W),
                            lambda i, j: (i, jnp.clip(j + shift, 0, nb - 1), 0))

    return pl.pallas_call(
        functools.partial(_wattn_body, nb=nb),
        grid=(b, nb),
        in_specs=[
            pl.BlockSpec(memory_space=pltpu.SMEM),
            pl.BlockSpec((None, WINDOW, A_Q_W), lambda i, j: (i, j, 0)),
            kv_spec(-1), kv_spec(0), kv_spec(1),
            kv_spec(-1), kv_spec(0), kv_spec(1),
        ],
        out_specs=pl.BlockSpec((None, WINDOW, A_Q_W), lambda i, j: (i, j, 0)),
        out_shape=jax.ShapeDtypeStruct((b, s, A_Q_W), BF16),
        compiler_params=_params(2),
        name="win_attn",
    )(sink, q, k, k, k, v, v, v)


def _projb_body(x_ref, g_ref, win_ref, gq_ref, wq_ref, gkv_ref, wk_ref, wv_ref, tabq_ref, tabk_ref,
                q_ref, k_ref, v_ref, qc_ref, *, qcscale):
    h = _rms(x_ref[...], g_ref[...]).astype(BF16)
    proj = _dot(h, win_ref[...])
    c_q = _rms(proj[:, :Q_LORA], gq_ref[...]).astype(BF16)
    kv0 = Q_LORA
    c_kv = _rms(proj[:, kv0:kv0 + KV_LORA], gkv_ref[...]).astype(BF16)
    qc0 = kv0 + KV_LORA
    qc_ref[...] = (proj[:, qc0:qc0 + MEM_WIDTH] * qcscale).astype(BF16)
    kr0 = qc0 + MEM_WIDTH
    k_rope = _rope_block(proj[:, kr0:kr0 + LANES], tabk_ref, QK_ROPE // 2)
    q_all = _dot(c_q, wq_ref[...])
    k_all = _dot(c_kv, wk_ref[...])
    for hd in range(B_HEADS):
        hs = slice(B_HEAD_PAD * hd, B_HEAD_PAD * (hd + 1))
        q_ref[:, hs] = _rope_block(q_all[:, hs], tabq_ref, QK_ROPE // 2).astype(BF16)
        k_ref[:, hs] = (k_all[:, hs] + k_rope).astype(BF16)
    v_ref[...] = _dot(c_kv, wv_ref[...]).astype(BF16)


def _proj_b(x, g, w_in, gq, wq, gkv, wk, wv, tabq, tabk):
    b, s, d = x.shape
    tm = min(512, s)

    def out(width):
        return (jax.ShapeDtypeStruct((b, s, width), BF16),
                pl.BlockSpec((None, tm, width), lambda i, j: (i, j, 0)))

    shapes, specs = zip(out(B_HEADS * B_HEAD_PAD), out(B_HEADS * B_HEAD_PAD),
                        out(B_HEADS * V_HEAD), out(MEM_WIDTH))
    tab_spec = pl.BlockSpec((tm, 3 * LANES), lambda i, j: (j, 0))
    return pl.pallas_call(
        functools.partial(_projb_body, qcscale=HEAD_DIM ** -0.5 * LOG2E),
        grid=(b, s // tm),
        in_specs=[
            pl.BlockSpec((None, tm, d), lambda i, j: (i, j, 0)),
            _resident((1, d)),
            _resident(w_in.shape),
            _resident((1, Q_LORA)),
            _resident(wq.shape),
            _resident((1, KV_LORA)),
            _resident(wk.shape),
            _resident(wv.shape),
            tab_spec, tab_spec,
        ],
        out_specs=list(specs),
        out_shape=list(shapes),
        compiler_params=_params(2),
        name="proj_b",
    )(x, g.reshape(1, d), w_in, gq.reshape(1, -1), wq, gkv.reshape(1, -1), wk, wv, tabq, tabk)


def _mla_body(q_ref, k_ref, v_ref, o_ref, s_ref, p_ref, alpha_ref, m_ref, l_ref, acc_ref, *, kb, nkv):
    assert nkv == 1 or nkv % 2 == 0
    qb = q_ref.shape[0]
    hslices = [slice(B_HEAD_PAD * hh, B_HEAD_PAD * (hh + 1)) for hh in range(2)]

    def rows(t):
        return pl.ds(pl.multiple_of(t * kb, kb), kb)

    def scores(t, slot):
        for hh, hs in enumerate(hslices):
            s_ref[slot, hh] = _dot_nt(q_ref[:, hs], k_ref[rows(t), hs])

    def softmax(slot):
        for hh in range(2):
            s = s_ref[slot, hh]
            m = m_ref[hh]
            m_new = jnp.maximum(m, jnp.max(s, axis=-1, keepdims=True))
            alpha = jnp.exp2(m - m_new)
            p = jnp.exp2(s - m_new)
            p_ref[slot, hh] = p.astype(BF16)
            alpha_ref[slot, hh] = alpha
            m_ref[hh] = m_new
            l_ref[hh] = alpha * l_ref[hh] + jnp.sum(p, axis=-1, keepdims=True)

    def accumulate(t, slot):
        vt = v_ref[rows(t), :]
        for hh in range(2):
            acc_ref[hh] = alpha_ref[slot, hh] * acc_ref[hh] + _dot(p_ref[slot, hh], vt)

    def stage(t, slot, last=False):
        if not last:
            scores(t + 1, 1 - slot)
        accumulate(t - 1, 1 - slot)
        softmax(slot)

    m_ref[...] = jnp.full(m_ref.shape, NEG, F32)
    l_ref[...] = jnp.zeros(l_ref.shape, F32)
    acc_ref[...] = jnp.zeros(acc_ref.shape, F32)
    scores(0, 0)
    softmax(0)
    if nkv > 1:
        scores(1, 1)

        def two_stages(i, carry):
            stage(2 * i + 1, 1)
            stage(2 * i + 2, 0)
            return carry

        lax.fori_loop(0, (nkv - 2) // 2, two_stages, 0)
        stage(nkv - 1, 1, last=True)
    accumulate(nkv - 1, (nkv - 1) % 2)
    lane = lax.broadcasted_iota(jnp.int32, (qb, LANES), 1)
    o_ref[...] = jnp.where(lane < V_HEAD, acc_ref[0] / l_ref[0], acc_ref[1] / l_ref[1]).astype(BF16)


def _mla_attn(q, k, v):
    b, s, _ = q.shape
    qb = min(256, s)
    kb = min(512, s)
    npair = B_HEADS // 2
    return pl.pallas_call(
        functools.partial(_mla_body, kb=kb, nkv=s // kb),
        grid=(b, npair, s // qb),
        in_specs=[
            pl.BlockSpec((None, qb, 2 * B_HEAD_PAD), lambda i, j, n: (i, n, j)),
            pl.BlockSpec((None, s, 2 * B_HEAD_PAD), lambda i, j, n: (i, 0, j)),
            pl.BlockSpec((None, s, 2 * V_HEAD), lambda i, j, n: (i, 0, j)),
        ],
        out_specs=pl.BlockSpec((None, qb, 2 * V_HEAD), lambda i, j, n: (i, n, j)),
        out_shape=jax.ShapeDtypeStruct((b, s, B_HEADS * V_HEAD), BF16),
        scratch_shapes=[
            pltpu.VMEM((2, 2, qb, kb), F32),
            pltpu.VMEM((2, 2, qb, kb), BF16),
            pltpu.VMEM((2, 2, qb, 1), F32),
            pltpu.VMEM((2, qb, 1), F32),
            pltpu.VMEM((2, qb, 1), F32),
            pltpu.VMEM((2, qb, LANES), F32),
        ],
        compiler_params=_params(3),
        name="mla_attn",
    )(q, k, v)


def _oproj_body(x_ref, loc_ref, qc_ref, mk_ref, mv_ref, wo_ref, o_ref):
    heads = []
    for hd in range(MEM_HEADS):
        hs = slice(HEAD_DIM * hd, HEAD_DIM * (hd + 1))
        s = _dot_nt(qc_ref[:, hs], mk_ref[:, hs])
        p = jnp.exp2(s - jnp.max(s, axis=-1, keepdims=True))
        o = _dot(p.astype(BF16), mv_ref[:, hs]) / jnp.sum(p, axis=-1, keepdims=True)
        heads.append(o.astype(BF16))
    cross = jnp.concatenate(heads, axis=-1)
    y = _dot(loc_ref[...], wo_ref[:LOCAL_W, :]) + _dot(cross, wo_ref[LOCAL_W:, :])
    o_ref[...] = x_ref[...] + y


def _out_proj(x, local, qc, mk, mv, wo):
    b, s, d = x.shape
    tm = min(512, s)

    def tok(width):
        return pl.BlockSpec((None, tm, width), lambda i, j: (i, j, 0))

    mem = pl.BlockSpec((None, N_MEM, MEM_WIDTH), lambda i, j: (i, 0, 0))
    return pl.pallas_call(
        _oproj_body,
        grid=(b, s // tm),
        in_specs=[tok(d), tok(LOCAL_W), tok(MEM_WIDTH), mem, mem, _resident(wo.shape)],
        out_specs=tok(d),
        out_shape=jax.ShapeDtypeStruct((b, s, d), F32),
        compiler_params=_params(2),
        name="out_proj",
    )(x, local, qc, mk, mv, wo)


def _prep_ffn(w_gu, w_down):
    depth, d, _ = w_gu.shape
    nch = D_FF // FF_CHUNK
    w = w_gu.astype(BF16).reshape(depth, d, 2, nch, FF_CHUNK)
    wg = jnp.transpose(w[:, :, 0], (0, 2, 1, 3))
    wu = jnp.transpose(w[:, :, 1], (0, 2, 1, 3))
    wd = w_down.astype(BF16).reshape(depth, nch, FF_CHUNK, d)
    return wg, wu, wd


def _prep_b(b_w_in, b_w_q_up, b_w_kv_up):
    nl, d, _ = b_w_in.shape
    c_q = b_w_in[:, :, :Q_LORA]
    c_kv = b_w_in[:, :, Q_LORA:Q_LORA + KV_LORA]
    k_r = b_w_in[:, :, Q_LORA + KV_LORA:Q_LORA + KV_LORA + QK_ROPE]
    qc = b_w_in[:, :, Q_LORA + KV_LORA + QK_ROPE:]
    kr_tile = jnp.pad(k_r, ((0, 0), (0, 0), (QK_NOPE, LANES - QK_NOPE - QK_ROPE)))
    w_in = jnp.concatenate([c_q, c_kv, qc, kr_tile], axis=-1).astype(BF16)
    wq = b_w_q_up.reshape(nl, Q_LORA, B_HEADS, B_QK)
    wq = jnp.pad(wq, ((0, 0), (0, 0), (0, 0), (0, B_HEAD_PAD - B_QK)))
    wq = wq.reshape(nl, Q_LORA, B_HEADS * B_HEAD_PAD).astype(BF16)
    wkv = b_w_kv_up.reshape(nl, KV_LORA, B_HEADS, QK_NOPE + V_HEAD)
    wk = jnp.pad(wkv[..., :QK_NOPE], ((0, 0), (0, 0), (0, 0), (0, B_HEAD_PAD - QK_NOPE)))
    wk = wk.reshape(nl, KV_LORA, B_HEADS * B_HEAD_PAD).astype(BF16)
    wv = wkv[..., QK_NOPE:].reshape(nl, KV_LORA, B_HEADS * V_HEAD).astype(BF16)
    return w_in, wq, wk, wv


def _trunk(x, mem, w):
    b, s, d = x.shape
    depth = w["mix_norm"].shape[0]
    tab_a = _rope_table(s, HEAD_DIM, 0, HEAD_DIM, 1.0, False)
    tab_bq = _rope_table(s, QK_ROPE, QK_NOPE, LANES, B_QK ** -0.5 * LOG2E, True)
    tab_bk = _rope_table(s, QK_ROPE, QK_NOPE, LANES, 1.0, False)
    for i in range(depth):
        x = _ffn(x.reshape(b * s, d), w["ffn1_norm"][i], w["ffn1_wg"][i], w["ffn1_wu"][i],
                 w["ffn1_wd"][i]).reshape(b, s, d)
        mk, mv = _mem_kv(mem, w["mem_norm"][i], w["w_mem_kv"][i])
        j = i // 2
        if i % 2 == 0:
            q, k, v, qc = _proj_a(x, w["mix_norm"][i], w["a_w_in"][j], tab_a)
            local = _win_attn(q, k, v, w["a_sink"][j])
        else:
            q, k, v, qc = _proj_b(x, w["mix_norm"][i], w["b_w_in"][j], w["b_q_norm"][j], w["b_wq"][j],
                                  w["b_kv_norm"][j], w["b_wk"][j], w["b_wv"][j], tab_bq, tab_bk)
            local = _mla_attn(q, k, v)
        x = _out_proj(x, local, qc, mk, mv, w["w_o"][i])
        g_final = w["final_norm"] if i == depth - 1 else None
        x = _ffn(x.reshape(b * s, d), w["ffn2_norm"][i], w["ffn2_wg"][i], w["ffn2_wu"][i],
                 w["ffn2_wd"][i], g_final).reshape(b, s, d)
    return x


def kernel(x_prompt, x_sample, mem_prompt, mem_sample, ffn1_norm, ffn1_w_gu, ffn1_w_down, mix_norm,
           mem_norm, w_mem_kv, a_w_in, a_sink, b_w_in, b_q_norm, b_w_q_up, b_kv_norm, b_w_kv_up,
           w_o, ffn2_norm, ffn2_w_gu, ffn2_w_down, final_norm):
    w = {
        "ffn1_norm": ffn1_norm, "ffn2_norm": ffn2_norm, "mix_norm": mix_norm, "mem_norm": mem_norm,
        "w_mem_kv": w_mem_kv.astype(BF16), "a_w_in": a_w_in.astype(BF16), "a_sink": a_sink,
        "b_q_norm": b_q_norm, "b_kv_norm": b_kv_norm, "w_o": w_o.astype(BF16),
        "final_norm": final_norm,
    }
    w["ffn1_wg"], w["ffn1_wu"], w["ffn1_wd"] = _prep_ffn(ffn1_w_gu, ffn1_w_down)
    w["ffn2_wg"], w["ffn2_wu"], w["ffn2_wd"] = _prep_ffn(ffn2_w_gu, ffn2_w_down)
    w["b_w_in"], w["b_wq"], w["b_wk"], w["b_wv"] = _prep_b(b_w_in, b_w_q_up, b_w_kv_up)
    return (_trunk(x_prompt, mem_prompt, w), _trunk(x_sample, mem_sample, w))
```

```python
import functools
import math

import jax
import jax.numpy as jnp
from jax import lax
from jax.experimental import pallas as pl
from jax.experimental.pallas import tpu as pltpu

D_MODEL = 1024
HEAD_DIM = 64
ROPE_THETA = 10000.0
NORM_EPS = 1e-6
D_FF = 2816
N_MEM = 256
MEM_HEADS = 4
MEM_WIDTH = MEM_HEADS * HEAD_DIM
A_Q_HEADS = 12
A_KV_HEADS = 4
A_GROUP = A_Q_HEADS // A_KV_HEADS
WINDOW = 128
A_Q_W = A_Q_HEADS * HEAD_DIM
A_KV_W = A_KV_HEADS * HEAD_DIM
B_HEADS = 12
Q_LORA = 384
KV_LORA = 256
QK_NOPE = 64
QK_ROPE = 32
V_HEAD = 64
B_QK = QK_NOPE + QK_ROPE
LOCAL_W = A_Q_W
NEG = -1e30
LOG2E = math.log2(math.e)

LANES = 128
B_HEAD_PAD = LANES
FF_CHUNK = 256
MLA_QB = 256
MLA_KB = 512
MLA_CHUNK = 32
MLA_DEN_ROWS = 16
MLA_AHEAD = 2
MLA_SLOTS = 2 * MLA_AHEAD
VMEM_LIMIT = 56 * 1024 * 1024

F32 = jnp.float32
BF16 = jnp.bfloat16


def _params(n_axes):
    return pltpu.CompilerParams(dimension_semantics=("arbitrary",) * n_axes,
                                vmem_limit_bytes=VMEM_LIMIT)


def _rms(x, g):
    return x * lax.rsqrt(jnp.mean(x * x, axis=-1, keepdims=True) + NORM_EPS) * g


def _dot(a, b):
    return jnp.dot(a, b, preferred_element_type=F32)


def _dot_nt(a, b):
    return lax.dot_general(a, b, (((1,), (1,)), ((), ())), preferred_element_type=F32)


def _resident(shape):
    zeros = (0,) * len(shape)
    return pl.BlockSpec(shape, lambda *_: zeros, pipeline_mode=pl.Buffered(1))


def _ffn_body(*refs, nch, final):
    if final:
        x_ref, g_ref, wg_ref, wu_ref, wd_ref, gf_ref, o_ref, h_ref, acc_ref = refs
    else:
        x_ref, g_ref, wg_ref, wu_ref, wd_ref, o_ref, h_ref, acc_ref = refs
    h_ref[...] = _rms(x_ref[...], g_ref[...]).astype(BF16)
    acc_ref[...] = jnp.zeros_like(acc_ref)

    def chunk(c, carry):
        h = h_ref[...]
        gate = _dot(h, wg_ref[c])
        up = _dot(h, wu_ref[c])
        a = (gate / (1.0 + jnp.exp(-gate)) * up).astype(BF16)
        acc_ref[...] += _dot(a, wd_ref[c])
        return carry

    lax.fori_loop(0, nch, chunk, 0)
    y = x_ref[...] + 0.5 * acc_ref[...]
    if final:
        y = _rms(y, gf_ref[...])
    o_ref[...] = y


def _ffn(x, g, wg, wu, wd, g_final=None):
    t, d = x.shape
    nch, _, fc = wg.shape
    tm = min(512, t)
    final = g_final is not None
    in_specs = [
        pl.BlockSpec((tm, d), lambda i: (i, 0)),
        _resident((1, d)),
        _resident(wg.shape),
        _resident(wu.shape),
        _resident(wd.shape),
    ]
    args = [x, g.reshape(1, d), wg, wu, wd]
    if final:
        in_specs.append(_resident((1, d)))
        args.append(g_final.reshape(1, d))
    return pl.pallas_call(
        functools.partial(_ffn_body, nch=nch, final=final),
        grid=(t // tm,),
        in_specs=in_specs,
        out_specs=pl.BlockSpec((tm, d), lambda i: (i, 0)),
        out_shape=jax.ShapeDtypeStruct((t, d), F32),
        scratch_shapes=[pltpu.VMEM((tm, d), BF16), pltpu.VMEM((tm, d), F32)],
        compiler_params=_params(1),
        name="ffn_final" if final else "ffn",
    )(*args)


def _memkv_body(mem_ref, g_ref, w_ref, mk_ref, mv_ref):
    h = _rms(mem_ref[...], g_ref[...]).astype(BF16)
    kv = _dot(h, w_ref[...])
    mk_ref[...] = kv[:, :MEM_WIDTH].astype(BF16)
    mv_ref[...] = kv[:, MEM_WIDTH:].astype(BF16)


def _mem_kv(mem, g, w):
    b, n, d = mem.shape
    out = jax.ShapeDtypeStruct((b, n, MEM_WIDTH), BF16)
    blk = pl.BlockSpec((None, n, MEM_WIDTH), lambda i: (i, 0, 0))
    return pl.pallas_call(
        _memkv_body,
        grid=(b,),
        in_specs=[pl.BlockSpec((None, n, d), lambda i: (i, 0, 0)), _resident((1, d)), _resident(w.shape)],
        out_specs=[blk, blk],
        out_shape=[out, out],
        compiler_params=_params(1),
        name="mem_kv",
    )(mem, g.reshape(1, d), w)


def _rope_block(xb, tab_ref, shift):
    c = tab_ref[:, 0:LANES]
    s_plus = tab_ref[:, LANES:2 * LANES]
    s_minus = tab_ref[:, 2 * LANES:3 * LANES]
    return (xb * c + pltpu.roll(xb, shift, 1) * s_plus
            + pltpu.roll(xb, LANES - shift, 1) * s_minus)


def _rope_table(seq, dim, lane_start, period, scale, pass_through):
    half = dim // 2
    inv = 1.0 / (ROPE_THETA ** (jnp.arange(0, dim, 2, dtype=F32) / dim))
    ang = jnp.arange(seq, dtype=F32)[:, None] * inv[None, :]
    cos, sin = jnp.cos(ang), jnp.sin(ang)
    zeros_h = jnp.zeros((seq, half), F32)
    lead = jnp.full((seq, lane_start), 1.0 if pass_through else 0.0, F32)
    lead0 = jnp.zeros((seq, lane_start), F32)
    tail0 = jnp.zeros((seq, period - lane_start - dim), F32)
    reps = LANES // period
    c = jnp.tile(jnp.concatenate([lead, cos, cos, tail0], -1), (1, reps))
    s_plus = jnp.tile(jnp.concatenate([lead0, zeros_h, sin, tail0], -1), (1, reps))
    s_minus = jnp.tile(jnp.concatenate([lead0, -sin, zeros_h, tail0], -1), (1, reps))
    return jnp.concatenate([c, s_plus, s_minus], -1) * scale


def _proja_body(x_ref, g_ref, w_ref, tab_ref, q_ref, k_ref, v_ref, qc_ref, *, qscale):
    h = _rms(x_ref[...], g_ref[...]).astype(BF16)
    proj = _dot(h, w_ref[...])
    nq = A_Q_W // LANES
    nk = A_KV_W // LANES
    for j in range(nq + nk):
        rb = _rope_block(proj[:, LANES * j:LANES * (j + 1)], tab_ref, HEAD_DIM // 2)
        if j < nq:
            q_ref[:, LANES * j:LANES * (j + 1)] = (rb * qscale).astype(BF16)
        else:
            k_ref[:, LANES * (j - nq):LANES * (j - nq + 1)] = rb.astype(BF16)
    v0 = A_Q_W + A_KV_W
    v_ref[...] = proj[:, v0:v0 + A_KV_W].astype(BF16)
    qc_ref[...] = (proj[:, v0 + A_KV_W:] * qscale).astype(BF16)


def _proj_a(x, g, w, tab):
    b, s, d = x.shape
    tm = min(512, s)

    def out(width):
        return (jax.ShapeDtypeStruct((b, s, width), BF16),
                pl.BlockSpec((None, tm, width), lambda i, j: (i, j, 0)))

    shapes, specs = zip(out(A_Q_W), out(A_KV_W), out(A_KV_W), out(MEM_WIDTH))
    return pl.pallas_call(
        functools.partial(_proja_body, qscale=HEAD_DIM ** -0.5 * LOG2E),
        grid=(b, s // tm),
        in_specs=[
            pl.BlockSpec((None, tm, d), lambda i, j: (i, j, 0)),
            _resident((1, d)),
            _resident(w.shape),
            pl.BlockSpec((tm, 3 * LANES), lambda i, j: (j, 0)),
        ],
        out_specs=list(specs),
        out_shape=list(shapes),
        compiler_params=_params(2),
        name="proj_a",
    )(x, g.reshape(1, d), w, tab)


def _wattn_body(sink_ref, q_ref, kp_ref, kc_ref, kn_ref, vp_ref, vc_ref, vn_ref, o_ref, *, nb):
    n = pl.program_id(1)
    qi = lax.broadcasted_iota(jnp.int32, (WINDOW, 3 * WINDOW), 0)
    kj = lax.broadcasted_iota(jnp.int32, (WINDOW, 3 * WINDOW), 1)
    dist = kj - qi
    lo = jnp.where(n > 0, 0, WINDOW)
    hi = jnp.where(n < nb - 1, 3 * WINDOW, 2 * WINDOW)
    mask = (dist >= 0) & (dist <= 2 * WINDOW) & (kj >= lo) & (kj < hi)
    for g in range(A_KV_HEADS):
        cs = slice(HEAD_DIM * g, HEAD_DIM * (g + 1))
        kcat = jnp.concatenate([kp_ref[:, cs], kc_ref[:, cs], kn_ref[:, cs]], axis=0)
        vcat = jnp.concatenate([vp_ref[:, cs], vc_ref[:, cs], vn_ref[:, cs]], axis=0)
        for r in range(A_GROUP):
            hd = A_GROUP * g + r
            hs = slice(HEAD_DIM * hd, HEAD_DIM * (hd + 1))
            s = jnp.where(mask, _dot_nt(q_ref[:, hs], kcat), NEG)
            sk = sink_ref[hd] * LOG2E
            m = jnp.maximum(jnp.max(s, axis=-1, keepdims=True), sk)
            p = jnp.exp2(s - m)
            den = jnp.sum(p, axis=-1, keepdims=True) + jnp.exp2(sk - m)
            o = _dot(p.astype(BF16), vcat) / den
            o_ref[:, hs] = o.astype(BF16)


def _win_attn(q, k, v, sink):
    b, s, _ = q.shape
    nb = s // WINDOW

    def kv_spec(shift):
        return pl.BlockSpec((None, WINDOW, A_KV_W),
                            lambda i, j: (i, jnp.clip(j + shift, 0, nb - 1), 0))

    return pl.pallas_call(
        functools.partial(_wattn_body, nb=nb),
        grid=(b, nb),
        in_specs=[
            pl.BlockSpec(memory_space=pltpu.SMEM),
            pl.BlockSpec((None, WINDOW, A_Q_W), lambda i, j: (i, j, 0)),
            kv_spec(-1), kv_spec(0), kv_spec(1),
            kv_spec(-1), kv_spec(0), kv_spec(1),
        ],
        out_specs=pl.BlockSpec((None, WINDOW, A_Q_W), lambda i, j: (i, j, 0)),
        out_shape=jax.ShapeDtypeStruct((b, s, A_Q_W), BF16),
        compiler_params=_params(2),
        name="win_attn",
    )(sink, q, k, k, k, v, v, v)


def _projb_body(x_ref, g_ref, win_ref, gq_ref, wq_ref, gkv_ref, wk_ref, wvt_ref, tabq_ref, tabk_ref,
                q_ref, k_ref, v_ref, qc_ref, *, qcscale):
    h = _rms(x_ref[...], g_ref[...]).astype(BF16)
    proj = _dot(h, win_ref[...])
    c_q = _rms(proj[:, :Q_LORA], gq_ref[...]).astype(BF16)
    kv0 = Q_LORA
    c_kv = _rms(proj[:, kv0:kv0 + KV_LORA], gkv_ref[...]).astype(BF16)
    qc0 = kv0 + KV_LORA
    qc_ref[...] = (proj[:, qc0:qc0 + MEM_WIDTH] * qcscale).astype(BF16)
    kr0 = qc0 + MEM_WIDTH
    k_rope = _rope_block(proj[:, kr0:kr0 + LANES], tabk_ref, QK_ROPE // 2)
    q_all = _dot(c_q, wq_ref[...])
    k_all = _dot(c_kv, wk_ref[...])
    for hd in range(B_HEADS):
        hs = slice(B_HEAD_PAD * hd, B_HEAD_PAD * (hd + 1))
        q_ref[:, hs] = _rope_block(q_all[:, hs], tabq_ref, QK_ROPE // 2).astype(BF16)
        k_ref[:, hs] = (k_all[:, hs] + k_rope).astype(BF16)
    v_ref[...] = _dot_nt(wvt_ref[...], c_kv).astype(BF16)


def _proj_b(x, g, w_in, gq, wq, gkv, wk, wvt, tabq, tabk):
    b, s, d = x.shape
    tm = min(MLA_KB, s)

    def out(width):
        return (jax.ShapeDtypeStruct((b, s, width), BF16),
                pl.BlockSpec((None, tm, width), lambda i, j: (i, j, 0)))

    vt = (jax.ShapeDtypeStruct((b, s // tm, B_HEADS * V_HEAD, tm), BF16),
          pl.BlockSpec((None, None, B_HEADS * V_HEAD, tm), lambda i, j: (i, j, 0, 0)))
    shapes, specs = zip(out(B_HEADS * B_HEAD_PAD), out(B_HEADS * B_HEAD_PAD), vt, out(MEM_WIDTH))
    tab_spec = pl.BlockSpec((tm, 3 * LANES), lambda i, j: (j, 0))
    return pl.pallas_call(
        functools.partial(_projb_body, qcscale=HEAD_DIM ** -0.5 * LOG2E),
        grid=(b, s // tm),
        in_specs=[
            pl.BlockSpec((None, tm, d), lambda i, j: (i, j, 0)),
            _resident((1, d)),
            _resident(w_in.shape),
            _resident((1, Q_LORA)),
            _resident(wq.shape),
            _resident((1, KV_LORA)),
            _resident(wk.shape),
            _resident(wvt.shape),
            tab_spec, tab_spec,
        ],
        out_specs=list(specs),
        out_shape=list(shapes),
        compiler_params=_params(2),
        name="proj_b",
    )(x, g.reshape(1, d), w_in, gq.reshape(1, -1), wq, gkv.reshape(1, -1), wk, wvt, tabq, tabk)


def _mla_body(q_ref, k_ref, vt_ref, o_ref, *scratch, kb, nkv):
    per_slot = 3 * 2
    slots = [scratch[per_slot * i:per_slot * (i + 1)] for i in range(MLA_SLOTS)]
    s_refs = [sl[0:2] for sl in slots]
    p_refs = [sl[2:4] for sl in slots]
    alpha_refs = [sl[4:6] for sl in slots]
    m_refs, acc_refs = (scratch[per_slot * MLA_SLOTS + 2 * i:per_slot * MLA_SLOTS + 2 * (i + 1)]
                        for i in range(2))
    hslices = [slice(B_HEAD_PAD * hh, B_HEAD_PAD * (hh + 1)) for hh in range(2)]
    vslices = [slice(V_HEAD * hh, V_HEAD * (hh + 1)) for hh in range(2)]
    chunks = [slice(c, c + MLA_CHUNK) for c in range(0, kb, MLA_CHUNK)]

    def scores(t, slot):
        rows = pl.ds(pl.multiple_of(t * kb, kb), kb)
        for hh, hs in enumerate(hslices):
            s_refs[slot][hh][...] = _dot_nt(k_ref[rows, hs], q_ref[:, hs])

    def softmax(slot):
        for hh in range(2):
            s_ref, p_ref = s_refs[slot][hh], p_refs[slot][hh]
            cmax = s_ref[chunks[0], :]
            for ch in chunks[1:]:
                cmax = jnp.maximum(cmax, s_ref[ch, :])
            m = m_refs[hh][...]
            m_new = jnp.maximum(m, jnp.max(cmax, axis=0, keepdims=True))
            alpha = jnp.exp2(m - m_new)
            for ch in chunks:
                p_ref[ch, :] = jnp.exp2(s_ref[ch, :] - m_new).astype(BF16)
            alpha_refs[slot][hh][...] = alpha
            m_refs[hh][...] = m_new

    def accumulate(t, slot):
        ones = jnp.ones((MLA_DEN_ROWS, kb), BF16)
        for hh, vs in enumerate(vslices):
            v_aug = jnp.concatenate([vt_ref[t, vs, :], ones], axis=0)
            acc_refs[hh][...] = (alpha_refs[slot][hh][...] * acc_refs[hh][...]
                                 + _dot(v_aug, p_refs[slot][hh][...]))

    def stage(t, phase, ahead=True, behind=True):
        if ahead:
            scores(t + MLA_AHEAD, (phase + MLA_AHEAD) % MLA_SLOTS)
        if behind:
            accumulate(t - MLA_AHEAD, (phase - MLA_AHEAD) % MLA_SLOTS)
        softmax(phase)

    for hh in range(2):
        m_refs[hh][...] = jnp.full(m_refs[hh].shape, NEG, F32)
        acc_refs[hh][...] = jnp.zeros(acc_refs[hh].shape, F32)
    for t in range(min(MLA_AHEAD, nkv)):
        scores(t, t % MLA_SLOTS)
    lo, hi = MLA_AHEAD, nkv - MLA_AHEAD
    trips = max(hi - lo, 0) // MLA_SLOTS
    for t in range(min(lo, nkv)):
        stage(t, t % MLA_SLOTS, ahead=t + MLA_AHEAD < nkv, behind=False)

    def full_stages(i, carry):
        for j in range(MLA_SLOTS):
            stage(lo + MLA_SLOTS * i + j, (lo + j) % MLA_SLOTS)
        return carry

    lax.fori_loop(0, trips, full_stages, 0)
    for t in range(lo + trips * MLA_SLOTS, nkv):
        stage(t, t % MLA_SLOTS, ahead=t + MLA_AHEAD < nkv, behind=t >= MLA_AHEAD)
    for t in range(max(nkv - MLA_AHEAD, 0), nkv):
        accumulate(t, t % MLA_SLOTS)
    out_t = jnp.concatenate([acc_refs[hh][:V_HEAD, :] / acc_refs[hh][V_HEAD:V_HEAD + 1, :] for hh in range(2)],
                            axis=0)
    o_ref[...] = out_t.T.astype(BF16)


def _mla_attn(q, k, vt):
    b, s, _ = q.shape
    _, nkv, _, kb = vt.shape
    qb = min(MLA_QB, s)
    npair = B_HEADS // 2
    per_slot = ([pltpu.VMEM((kb, qb), F32)] * 2
                + [pltpu.VMEM((kb, qb), BF16)] * 2
                + [pltpu.VMEM((1, qb), F32)] * 2)
    state = ([pltpu.VMEM((1, qb), F32)] * 2
             + [pltpu.VMEM((V_HEAD + MLA_DEN_ROWS, qb), F32)] * 2)
    return pl.pallas_call(
        functools.partial(_mla_body, kb=kb, nkv=nkv),
        grid=(b, npair, s // qb),
        in_specs=[
            pl.BlockSpec((None, qb, 2 * B_HEAD_PAD), lambda i, j, n: (i, n, j)),
            pl.BlockSpec((None, s, 2 * B_HEAD_PAD), lambda i, j, n: (i, 0, j)),
            pl.BlockSpec((None, nkv, 2 * V_HEAD, kb), lambda i, j, n: (i, 0, j, 0)),
        ],
        out_specs=pl.BlockSpec((None, qb, 2 * V_HEAD), lambda i, j, n: (i, n, j)),
        out_shape=jax.ShapeDtypeStruct((b, s, B_HEADS * V_HEAD), BF16),
        scratch_shapes=per_slot * MLA_SLOTS + state,
        compiler_params=_params(3),
        name="mla_attn",
    )(q, k, vt)


def _oproj_body(x_ref, loc_ref, qc_ref, mk_ref, mv_ref, wo_ref, o_ref):
    heads = []
    for hd in range(MEM_HEADS):
        hs = slice(HEAD_DIM * hd, HEAD_DIM * (hd + 1))
        s = _dot_nt(qc_ref[:, hs], mk_ref[:, hs])
        p = jnp.exp2(s - jnp.max(s, axis=-1, keepdims=True))
        o = _dot(p.astype(BF16), mv_ref[:, hs]) / jnp.sum(p, axis=-1, keepdims=True)
        heads.append(o.astype(BF16))
    cross = jnp.concatenate(heads, axis=-1)
    y = _dot(loc_ref[...], wo_ref[:LOCAL_W, :]) + _dot(cross, wo_ref[LOCAL_W:, :])
    o_ref[...] = x_ref[...] + y


def _out_proj(x, local, qc, mk, mv, wo):
    b, s, d = x.shape
    tm = min(512, s)

    def tok(width):
        return pl.BlockSpec((None, tm, width), lambda i, j: (i, j, 0))

    mem = pl.BlockSpec((None, N_MEM, MEM_WIDTH), lambda i, j: (i, 0, 0))
    return pl.pallas_call(
        _oproj_body,
        grid=(b, s // tm),
        in_specs=[tok(d), tok(LOCAL_W), tok(MEM_WIDTH), mem, mem, _resident(wo.shape)],
        out_specs=tok(d),
        out_shape=jax.ShapeDtypeStruct((b, s, d), F32),
        compiler_params=_params(2),
        name="out_proj",
    )(x, local, qc, mk, mv, wo)


def _prep_ffn(w_gu, w_down):
    depth, d, _ = w_gu.shape
    nch = D_FF // FF_CHUNK
    w = w_gu.astype(BF16).reshape(depth, d, 2, nch, FF_CHUNK)
    wg = jnp.transpose(w[:, :, 0], (0, 2, 1, 3))
    wu = jnp.transpose(w[:, :, 1], (0, 2, 1, 3))
    wd = w_down.astype(BF16).reshape(depth, nch, FF_CHUNK, d)
    return wg, wu, wd


def _prep_b(b_w_in, b_w_q_up, b_w_kv_up):
    nl, d, _ = b_w_in.shape
    c_q = b_w_in[:, :, :Q_LORA]
    c_kv = b_w_in[:, :, Q_LORA:Q_LORA + KV_LORA]
    k_r = b_w_in[:, :, Q_LORA + KV_LORA:Q_LORA + KV_LORA + QK_ROPE]
    qc = b_w_in[:, :, Q_LORA + KV_LORA + QK_ROPE:]
    kr_tile = jnp.pad(k_r, ((0, 0), (0, 0), (QK_NOPE, LANES - QK_NOPE - QK_ROPE)))
    w_in = jnp.concatenate([c_q, c_kv, qc, kr_tile], axis=-1).astype(BF16)
    wq = b_w_q_up.reshape(nl, Q_LORA, B_HEADS, B_QK)
    wq = jnp.pad(wq, ((0, 0), (0, 0), (0, 0), (0, B_HEAD_PAD - B_QK)))
    wq = wq.reshape(nl, Q_LORA, B_HEADS * B_HEAD_PAD).astype(BF16)
    wkv = b_w_kv_up.reshape(nl, KV_LORA, B_HEADS, QK_NOPE + V_HEAD)
    wk = jnp.pad(wkv[..., :QK_NOPE], ((0, 0), (0, 0), (0, 0), (0, B_HEAD_PAD - QK_NOPE)))
    wk = wk.reshape(nl, KV_LORA, B_HEADS * B_HEAD_PAD).astype(BF16)
    wvt = jnp.transpose(wkv[..., QK_NOPE:].reshape(nl, KV_LORA, B_HEADS * V_HEAD), (0, 2, 1)).astype(BF16)
    return w_in, wq, wk, wvt


def _trunk(x, mem, w):
    b, s, d = x.shape
    depth = w["mix_norm"].shape[0]
    tab_a = _rope_table(s, HEAD_DIM, 0, HEAD_DIM, 1.0, False)
    tab_bq = _rope_table(s, QK_ROPE, QK_NOPE, LANES, B_QK ** -0.5 * LOG2E, True)
    tab_bk = _rope_table(s, QK_ROPE, QK_NOPE, LANES, 1.0, False)
    for i in range(depth):
        x = _ffn(x.reshape(b * s, d), w["ffn1_norm"][i], w["ffn1_wg"][i], w["ffn1_wu"][i],
                 w["ffn1_wd"][i]).reshape(b, s, d)
        mk, mv = _mem_kv(mem, w["mem_norm"][i], w["w_mem_kv"][i])
        j = i // 2
        if i % 2 == 0:
            q, k, v, qc = _proj_a(x, w["mix_norm"][i], w["a_w_in"][j], tab_a)
            local = _win_attn(q, k, v, w["a_sink"][j])
        else:
            q, k, v, qc = _proj_b(x, w["mix_norm"][i], w["b_w_in"][j], w["b_q_norm"][j], w["b_wq"][j],
                                  w["b_kv_norm"][j], w["b_wk"][j], w["b_wvt"][j], tab_bq, tab_bk)
            local = _mla_attn(q, k, v)
        x = _out_proj(x, local, qc, mk, mv, w["w_o"][i])
        g_final = w["final_norm"] if i == depth - 1 else None
        x = _ffn(x.reshape(b * s, d), w["ffn2_norm"][i], w["ffn2_wg"][i], w["ffn2_wu"][i],
                 w["ffn2_wd"][i], g_final).reshape(b, s, d)
    return x


def kernel(x_prompt, x_sample, mem_prompt, mem_sample, ffn1_norm, ffn1_w_gu, ffn1_w_down, mix_norm,
           mem_norm, w_mem_kv, a_w_in, a_sink, b_w_in, b_q_norm, b_w_q_up, b_kv_norm, b_w_kv_up,
           w_o, ffn2_norm, ffn2_w_gu, ffn2_w_down, final_norm):
    w = {
        "ffn1_norm": ffn1_norm, "ffn2_norm": ffn2_norm, "mix_norm": mix_norm, "mem_norm": mem_norm,
        "w_mem_kv": w_mem_kv.astype(BF16), "a_w_in": a_w_in.astype(BF16), "a_sink": a_sink,
        "b_q_norm": b_q_norm, "b_kv_norm": b_kv_norm, "w_o": w_o.astype(BF16),
        "final_norm": final_norm,
    }
    w["ffn1_wg"], w["ffn1_wu"], w["ffn1_wd"] = _prep_ffn(ffn1_w_gu, ffn1_w_down)
    w["ffn2_wg"], w["ffn2_wu"], w["ffn2_wd"] = _prep_ffn(ffn2_w_gu, ffn2_w_down)
    w["b_w_in"], w["b_wq"], w["b_wk"], w["b_wvt"] = _prep_b(b_w_in, b_w_q_up, b_w_kv_up)
    return (_trunk(x_prompt, mem_prompt, w), _trunk(x_sample, mem_sample, w))
```

```python
import functools
import math

import jax
import jax.numpy as jnp
from jax import lax
from jax.experimental import pallas as pl
from jax.experimental.pallas import tpu as pltpu

D_MODEL = 1024
HEAD_DIM = 64
ROPE_THETA = 10000.0
NORM_EPS = 1e-6
D_FF = 2816
N_MEM = 256
MEM_HEADS = 4
MEM_WIDTH = MEM_HEADS * HEAD_DIM
A_Q_HEADS = 12
A_KV_HEADS = 4
A_GROUP = A_Q_HEADS // A_KV_HEADS
WINDOW = 128
A_Q_W = A_Q_HEADS * HEAD_DIM
A_KV_W = A_KV_HEADS * HEAD_DIM
B_HEADS = 12
Q_LORA = 384
KV_LORA = 256
QK_NOPE = 64
QK_ROPE = 32
V_HEAD = 64
B_QK = QK_NOPE + QK_ROPE
LOCAL_W = A_Q_W
NEG = -1e30
LOG2E = math.log2(math.e)

LANES = 128
B_HEAD_PAD = LANES
FF_CHUNK = 256
MLA_QB = 512
MLA_KB = 512
MLA_CHUNK = 32
MLA_DEN_ROWS = 16
MLA_AHEAD = 2
MLA_SLOTS = 2 * MLA_AHEAD
VMEM_LIMIT = 56 * 1024 * 1024

F32 = jnp.float32
BF16 = jnp.bfloat16


def _params(n_axes):
    return pltpu.CompilerParams(dimension_semantics=("arbitrary",) * n_axes,
                                vmem_limit_bytes=VMEM_LIMIT)


def _rms(x, g):
    return x * lax.rsqrt(jnp.mean(x * x, axis=-1, keepdims=True) + NORM_EPS) * g


def _dot(a, b):
    return jnp.dot(a, b, preferred_element_type=F32)


def _dot_nt(a, b):
    return lax.dot_general(a, b, (((1,), (1,)), ((), ())), preferred_element_type=F32)


def _resident(shape):
    zeros = (0,) * len(shape)
    return pl.BlockSpec(shape, lambda *_: zeros, pipeline_mode=pl.Buffered(1))


def _ffn_body(*refs, nch, final):
    if final:
        x_ref, g_ref, wg_ref, wu_ref, wd_ref, gf_ref, o_ref, h_ref = refs
    else:
        x_ref, g_ref, wg_ref, wu_ref, wd_ref, o_ref, h_ref = refs
    h_ref[...] = _rms(x_ref[...], g_ref[...]).astype(BF16)
    acc = None
    for c in range(nch):
        h = h_ref[...]
        gate = _dot(h, wg_ref[c])
        up = _dot(h, wu_ref[c])
        a = (gate / (1.0 + jnp.exp(-gate)) * up).astype(BF16)
        down = _dot(a, wd_ref[c])
        acc = down if acc is None else acc + down
    y = x_ref[...] + 0.5 * acc
    if final:
        y = _rms(y, gf_ref[...])
    o_ref[...] = y


def _ffn(x, g, wg, wu, wd, g_final=None):
    t, d = x.shape
    nch, _, fc = wg.shape
    tm = min(512, t)
    final = g_final is not None
    in_specs = [
        pl.BlockSpec((tm, d), lambda i: (i, 0)),
        _resident((1, d)),
        _resident(wg.shape),
        _resident(wu.shape),
        _resident(wd.shape),
    ]
    args = [x, g.reshape(1, d), wg, wu, wd]
    if final:
        in_specs.append(_resident((1, d)))
        args.append(g_final.reshape(1, d))
    return pl.pallas_call(
        functools.partial(_ffn_body, nch=nch, final=final),
        grid=(t // tm,),
        in_specs=in_specs,
        out_specs=pl.BlockSpec((tm, d), lambda i: (i, 0)),
        out_shape=jax.ShapeDtypeStruct((t, d), F32),
        scratch_shapes=[pltpu.VMEM((tm, d), BF16)],
        compiler_params=_params(1),
        name="ffn_final" if final else "ffn",
    )(*args)


def _memkv_body(mem_ref, g_ref, w_ref, mk_ref, mv_ref):
    h = _rms(mem_ref[...], g_ref[...]).astype(BF16)
    kv = _dot(h, w_ref[...])
    mk_ref[...] = kv[:, :MEM_WIDTH].astype(BF16)
    mv_ref[...] = kv[:, MEM_WIDTH:].astype(BF16)


def _mem_kv(mem, g, w):
    b, n, d = mem.shape
    out = jax.ShapeDtypeStruct((b, n, MEM_WIDTH), BF16)
    blk = pl.BlockSpec((None, n, MEM_WIDTH), lambda i: (i, 0, 0))
    return pl.pallas_call(
        _memkv_body,
        grid=(b,),
        in_specs=[pl.BlockSpec((None, n, d), lambda i: (i, 0, 0)), _resident((1, d)), _resident(w.shape)],
        out_specs=[blk, blk],
        out_shape=[out, out],
        compiler_params=_params(1),
        name="mem_kv",
    )(mem, g.reshape(1, d), w)


def _rope_block(xb, tab_ref, shift):
    c = tab_ref[:, 0:LANES]
    s_plus = tab_ref[:, LANES:2 * LANES]
    s_minus = tab_ref[:, 2 * LANES:3 * LANES]
    return (xb * c + pltpu.roll(xb, shift, 1) * s_plus
            + pltpu.roll(xb, LANES - shift, 1) * s_minus)


def _rope_table(seq, dim, lane_start, period, scale, pass_through):
    half = dim // 2
    inv = 1.0 / (ROPE_THETA ** (jnp.arange(0, dim, 2, dtype=F32) / dim))
    ang = jnp.arange(seq, dtype=F32)[:, None] * inv[None, :]
    cos, sin = jnp.cos(ang), jnp.sin(ang)
    zeros_h = jnp.zeros((seq, half), F32)
    lead = jnp.full((seq, lane_start), 1.0 if pass_through else 0.0, F32)
    lead0 = jnp.zeros((seq, lane_start), F32)
    tail0 = jnp.zeros((seq, period - lane_start - dim), F32)
    reps = LANES // period
    c = jnp.tile(jnp.concatenate([lead, cos, cos, tail0], -1), (1, reps))
    s_plus = jnp.tile(jnp.concatenate([lead0, zeros_h, sin, tail0], -1), (1, reps))
    s_minus = jnp.tile(jnp.concatenate([lead0, -sin, zeros_h, tail0], -1), (1, reps))
    return jnp.concatenate([c, s_plus, s_minus], -1) * scale


def _proja_body(x_ref, g_ref, w_ref, tab_ref, q_ref, k_ref, v_ref, qc_ref, *, qscale):
    h = _rms(x_ref[...], g_ref[...]).astype(BF16)
    proj = _dot(h, w_ref[...])
    nq = A_Q_W // LANES
    nk = A_KV_W // LANES
    for j in range(nq + nk):
        rb = _rope_block(proj[:, LANES * j:LANES * (j + 1)], tab_ref, HEAD_DIM // 2)
        if j < nq:
            q_ref[:, LANES * j:LANES * (j + 1)] = (rb * qscale).astype(BF16)
        else:
            k_ref[:, LANES * (j - nq):LANES * (j - nq + 1)] = rb.astype(BF16)
    v0 = A_Q_W + A_KV_W
    v_ref[...] = proj[:, v0:v0 + A_KV_W].astype(BF16)
    qc_ref[...] = (proj[:, v0 + A_KV_W:] * qscale).astype(BF16)


def _proj_a(x, g, w, tab):
    b, s, d = x.shape
    tm = min(512, s)

    def out(width):
        return (jax.ShapeDtypeStruct((b, s, width), BF16),
                pl.BlockSpec((None, tm, width), lambda i, j: (i, j, 0)))

    shapes, specs = zip(out(A_Q_W), out(A_KV_W), out(A_KV_W), out(MEM_WIDTH))
    return pl.pallas_call(
        functools.partial(_proja_body, qscale=HEAD_DIM ** -0.5 * LOG2E),
        grid=(b, s // tm),
        in_specs=[
            pl.BlockSpec((None, tm, d), lambda i, j: (i, j, 0)),
            _resident((1, d)),
            _resident(w.shape),
            pl.BlockSpec((tm, 3 * LANES), lambda i, j: (j, 0)),
        ],
        out_specs=list(specs),
        out_shape=list(shapes),
        compiler_params=_params(2),
        name="proj_a",
    )(x, g.reshape(1, d), w, tab)


def _wattn_body(sink_ref, q_ref, kp_ref, kc_ref, kn_ref, vp_ref, vc_ref, vn_ref, o_ref, *, nb):
    n = pl.program_id(1)
    qi = lax.broadcasted_iota(jnp.int32, (WINDOW, 3 * WINDOW), 0)
    kj = lax.broadcasted_iota(jnp.int32, (WINDOW, 3 * WINDOW), 1)
    dist = kj - qi
    lo = jnp.where(n > 0, 0, WINDOW)
    hi = jnp.where(n < nb - 1, 3 * WINDOW, 2 * WINDOW)
    mask = (dist >= 0) & (dist <= 2 * WINDOW) & (kj >= lo) & (kj < hi)
    for g in range(A_KV_HEADS):
        cs = slice(HEAD_DIM * g, HEAD_DIM * (g + 1))
        kcat = jnp.concatenate([kp_ref[:, cs], kc_ref[:, cs], kn_ref[:, cs]], axis=0)
        vcat = jnp.concatenate([vp_ref[:, cs], vc_ref[:, cs], vn_ref[:, cs]], axis=0)
        for r in range(A_GROUP):
            hd = A_GROUP * g + r
            hs = slice(HEAD_DIM * hd, HEAD_DIM * (hd + 1))
            s = jnp.where(mask, _dot_nt(q_ref[:, hs], kcat), NEG)
            sk = sink_ref[hd] * LOG2E
            m = jnp.maximum(jnp.max(s, axis=-1, keepdims=True), sk)
            p = jnp.exp2(s - m)
            den = jnp.sum(p, axis=-1, keepdims=True) + jnp.exp2(sk - m)
            o = _dot(p.astype(BF16), vcat) / den
            o_ref[:, hs] = o.astype(BF16)


def _win_attn(q, k, v, sink):
    b, s, _ = q.shape
    nb = s // WINDOW

    def kv_spec(shift):
        return pl.BlockSpec((None, WINDOW, A_KV_W),
                            lambda i, j: (i, jnp.clip(j + shift, 0, nb - 1), 0))

    return pl.pallas_call(
        functools.partial(_wattn_body, nb=nb),
        grid=(b, nb),
        in_specs=[
            pl.BlockSpec(memory_space=pltpu.SMEM),
            pl.BlockSpec((None, WINDOW, A_Q_W), lambda i, j: (i, j, 0)),
            kv_spec(-1), kv_spec(0), kv_spec(1),
            kv_spec(-1), kv_spec(0), kv_spec(1),
        ],
        out_specs=pl.BlockSpec((None, WINDOW, A_Q_W), lambda i, j: (i, j, 0)),
        out_shape=jax.ShapeDtypeStruct((b, s, A_Q_W), BF16),
        compiler_params=_params(2),
        name="win_attn",
    )(sink, q, k, k, k, v, v, v)


def _projb_body(x_ref, g_ref, win_ref, gq_ref, wq_ref, gkv_ref, wk_ref, wvt_ref, tabq_ref, tabk_ref,
                q_ref, k_ref, v_ref, qc_ref, *, qcscale):
    h = _rms(x_ref[...], g_ref[...]).astype(BF16)
    proj = _dot(h, win_ref[...])
    c_q = _rms(proj[:, :Q_LORA], gq_ref[...]).astype(BF16)
    kv0 = Q_LORA
    c_kv = _rms(proj[:, kv0:kv0 + KV_LORA], gkv_ref[...]).astype(BF16)
    qc0 = kv0 + KV_LORA
    qc_ref[...] = (proj[:, qc0:qc0 + MEM_WIDTH] * qcscale).astype(BF16)
    kr0 = qc0 + MEM_WIDTH
    k_rope = _rope_block(proj[:, kr0:kr0 + LANES], tabk_ref, QK_ROPE // 2)
    q_all = _dot(c_q, wq_ref[...])
    k_all = _dot(c_kv, wk_ref[...])
    for hd in range(B_HEADS):
        hs = slice(B_HEAD_PAD * hd, B_HEAD_PAD * (hd + 1))
        q_ref[:, hs] = _rope_block(q_all[:, hs], tabq_ref, QK_ROPE // 2).astype(BF16)
        k_ref[:, hs] = (k_all[:, hs] + k_rope).astype(BF16)
    v_ref[...] = _dot_nt(wvt_ref[...], c_kv).astype(BF16)


def _proj_b(x, g, w_in, gq, wq, gkv, wk, wvt, tabq, tabk):
    b, s, d = x.shape
    tm = min(MLA_KB, s)

    def out(width):
        return (jax.ShapeDtypeStruct((b, s, width), BF16),
                pl.BlockSpec((None, tm, width), lambda i, j: (i, j, 0)))

    vt = (jax.ShapeDtypeStruct((b, s // tm, B_HEADS * V_HEAD, tm), BF16),
          pl.BlockSpec((None, None, B_HEADS * V_HEAD, tm), lambda i, j: (i, j, 0, 0)))
    shapes, specs = zip(out(B_HEADS * B_HEAD_PAD), out(B_HEADS * B_HEAD_PAD), vt, out(MEM_WIDTH))
    tab_spec = pl.BlockSpec((tm, 3 * LANES), lambda i, j: (j, 0))
    return pl.pallas_call(
        functools.partial(_projb_body, qcscale=HEAD_DIM ** -0.5 * LOG2E),
        grid=(b, s // tm),
        in_specs=[
            pl.BlockSpec((None, tm, d), lambda i, j: (i, j, 0)),
            _resident((1, d)),
            _resident(w_in.shape),
            _resident((1, Q_LORA)),
            _resident(wq.shape),
            _resident((1, KV_LORA)),
            _resident(wk.shape),
            _resident(wvt.shape),
            tab_spec, tab_spec,
        ],
        out_specs=list(specs),
        out_shape=list(shapes),
        compiler_params=_params(2),
        name="proj_b",
    )(x, g.reshape(1, d), w_in, gq.reshape(1, -1), wq, gkv.reshape(1, -1), wk, wvt, tabq, tabk)


def _mla_body(q_ref, k_ref, vt_ref, o_ref, *scratch, kb, nkv):
    per_slot = 3 * 2
    slots = [scratch[per_slot * i:per_slot * (i + 1)] for i in range(MLA_SLOTS)]
    s_refs = [sl[0:2] for sl in slots]
    p_refs = [sl[2:4] for sl in slots]
    alpha_refs = [sl[4:6] for sl in slots]
    m_refs, acc_refs = (scratch[per_slot * MLA_SLOTS + 2 * i:per_slot * MLA_SLOTS + 2 * (i + 1)]
                        for i in range(2))
    hslices = [slice(B_HEAD_PAD * hh, B_HEAD_PAD * (hh + 1)) for hh in range(2)]
    vslices = [slice(V_HEAD * hh, V_HEAD * (hh + 1)) for hh in range(2)]
    chunks = [slice(c, c + MLA_CHUNK) for c in range(0, kb, MLA_CHUNK)]

    def scores(t, slot):
        rows = pl.ds(pl.multiple_of(t * kb, kb), kb)
        for hh, hs in enumerate(hslices):
            s_refs[slot][hh][...] = _dot_nt(k_ref[rows, hs], q_ref[:, hs])

    def softmax(slot):
        for hh in range(2):
            s_ref, p_ref = s_refs[slot][hh], p_refs[slot][hh]
            cmax = s_ref[chunks[0], :]
            for ch in chunks[1:]:
                cmax = jnp.maximum(cmax, s_ref[ch, :])
            m = m_refs[hh][...]
            m_new = jnp.maximum(m, jnp.max(cmax, axis=0, keepdims=True))
            alpha = jnp.exp2(m - m_new)
            for ch in chunks:
                p_ref[ch, :] = jnp.exp2(s_ref[ch, :] - m_new).astype(BF16)
            alpha_refs[slot][hh][...] = alpha
            m_refs[hh][...] = m_new

    def accumulate(t, slot):
        ones = jnp.ones((MLA_DEN_ROWS, kb), BF16)
        for hh, vs in enumerate(vslices):
            v_aug = jnp.concatenate([vt_ref[t, vs, :], ones], axis=0)
            acc_refs[hh][...] = (alpha_refs[slot][hh][...] * acc_refs[hh][...]
                                 + _dot(v_aug, p_refs[slot][hh][...]))

    def stage(t, phase, ahead=True, behind=True):
        if ahead:
            scores(t + MLA_AHEAD, (phase + MLA_AHEAD) % MLA_SLOTS)
        if behind:
            accumulate(t - MLA_AHEAD, (phase - MLA_AHEAD) % MLA_SLOTS)
        softmax(phase)

    for hh in range(2):
        m_refs[hh][...] = jnp.full(m_refs[hh].shape, NEG, F32)
        acc_refs[hh][...] = jnp.zeros(acc_refs[hh].shape, F32)
    for t in range(min(MLA_AHEAD, nkv)):
        scores(t, t % MLA_SLOTS)
    lo, hi = MLA_AHEAD, nkv - MLA_AHEAD
    trips = max(hi - lo, 0) // MLA_SLOTS
    for t in range(min(lo, nkv)):
        stage(t, t % MLA_SLOTS, ahead=t + MLA_AHEAD < nkv, behind=False)

    def full_stages(i, carry):
        for j in range(MLA_SLOTS):
            stage(lo + MLA_SLOTS * i + j, (lo + j) % MLA_SLOTS)
        return carry

    lax.fori_loop(0, trips, full_stages, 0)
    for t in range(lo + trips * MLA_SLOTS, nkv):
        stage(t, t % MLA_SLOTS, ahead=t + MLA_AHEAD < nkv, behind=t >= MLA_AHEAD)
    for t in range(max(nkv - MLA_AHEAD, 0), nkv):
        accumulate(t, t % MLA_SLOTS)
    out_t = jnp.concatenate([acc_refs[hh][:V_HEAD, :] / acc_refs[hh][V_HEAD:V_HEAD + 1, :] for hh in range(2)],
                            axis=0)
    o_ref[...] = out_t.T.astype(BF16)


def _mla_attn(q, k, vt):
    b, s, _ = q.shape
    _, nkv, _, kb = vt.shape
    qb = min(MLA_QB, s)
    npair = B_HEADS // 2
    per_slot = ([pltpu.VMEM((kb, qb), F32)] * 2
                + [pltpu.VMEM((kb, qb), BF16)] * 2
                + [pltpu.VMEM((1, qb), F32)] * 2)
    state = ([pltpu.VMEM((1, qb), F32)] * 2
             + [pltpu.VMEM((V_HEAD + MLA_DEN_ROWS, qb), F32)] * 2)
    return pl.pallas_call(
        functools.partial(_mla_body, kb=kb, nkv=nkv),
        grid=(b, npair, s // qb),
        in_specs=[
            pl.BlockSpec((None, qb, 2 * B_HEAD_PAD), lambda i, j, n: (i, n, j)),
            pl.BlockSpec((None, s, 2 * B_HEAD_PAD), lambda i, j, n: (i, 0, j)),
            pl.BlockSpec((None, nkv, 2 * V_HEAD, kb), lambda i, j, n: (i, 0, j, 0)),
        ],
        out_specs=pl.BlockSpec((None, qb, 2 * V_HEAD), lambda i, j, n: (i, n, j)),
        out_shape=jax.ShapeDtypeStruct((b, s, B_HEADS * V_HEAD), BF16),
        scratch_shapes=per_slot * MLA_SLOTS + state,
        compiler_params=_params(3),
        name="mla_attn",
    )(q, k, vt)


def _oproj_body(x_ref, loc_ref, qc_ref, mk_ref, mv_ref, wo_ref, o_ref):
    heads = []
    for hd in range(MEM_HEADS):
        hs = slice(HEAD_DIM * hd, HEAD_DIM * (hd + 1))
        s = _dot_nt(qc_ref[:, hs], mk_ref[:, hs])
        p = jnp.exp2(s - jnp.max(s, axis=-1, keepdims=True))
        o = _dot(p.astype(BF16), mv_ref[:, hs]) / jnp.sum(p, axis=-1, keepdims=True)
        heads.append(o.astype(BF16))
    cross = jnp.concatenate(heads, axis=-1)
    y = _dot(loc_ref[...], wo_ref[:LOCAL_W, :]) + _dot(cross, wo_ref[LOCAL_W:, :])
    o_ref[...] = x_ref[...] + y


def _out_proj(x, local, qc, mk, mv, wo):
    b, s, d = x.shape
    tm = min(512, s)

    def tok(width):
        return pl.BlockSpec((None, tm, width), lambda i, j: (i, j, 0))

    mem = pl.BlockSpec((None, N_MEM, MEM_WIDTH), lambda i, j: (i, 0, 0))
    return pl.pallas_call(
        _oproj_body,
        grid=(b, s // tm),
        in_specs=[tok(d), tok(LOCAL_W), tok(MEM_WIDTH), mem, mem, _resident(wo.shape)],
        out_specs=tok(d),
        out_shape=jax.ShapeDtypeStruct((b, s, d), F32),
        compiler_params=_params(2),
        name="out_proj",
    )(x, local, qc, mk, mv, wo)


def _prep_ffn(w_gu, w_down):
    depth, d, _ = w_gu.shape
    nch = D_FF // FF_CHUNK
    w = w_gu.astype(BF16).reshape(depth, d, 2, nch, FF_CHUNK)
    wg = jnp.transpose(w[:, :, 0], (0, 2, 1, 3))
    wu = jnp.transpose(w[:, :, 1], (0, 2, 1, 3))
    wd = w_down.astype(BF16).reshape(depth, nch, FF_CHUNK, d)
    return wg, wu, wd


def _prep_b(b_w_in, b_w_q_up, b_w_kv_up):
    nl, d, _ = b_w_in.shape
    c_q = b_w_in[:, :, :Q_LORA]
    c_kv = b_w_in[:, :, Q_LORA:Q_LORA + KV_LORA]
    k_r = b_w_in[:, :, Q_LORA + KV_LORA:Q_LORA + KV_LORA + QK_ROPE]
    qc = b_w_in[:, :, Q_LORA + KV_LORA + QK_ROPE:]
    kr_tile = jnp.pad(k_r, ((0, 0), (0, 0), (QK_NOPE, LANES - QK_NOPE - QK_ROPE)))
    w_in = jnp.concatenate([c_q, c_kv, qc, kr_tile], axis=-1).astype(BF16)
    wq = b_w_q_up.reshape(nl, Q_LORA, B_HEADS, B_QK)
    wq = jnp.pad(wq, ((0, 0), (0, 0), (0, 0), (0, B_HEAD_PAD - B_QK)))
    wq = wq.reshape(nl, Q_LORA, B_HEADS * B_HEAD_PAD).astype(BF16)
    wkv = b_w_kv_up.reshape(nl, KV_LORA, B_HEADS, QK_NOPE + V_HEAD)
    wk = jnp.pad(wkv[..., :QK_NOPE], ((0, 0), (0, 0), (0, 0), (0, B_HEAD_PAD - QK_NOPE)))
    wk = wk.reshape(nl, KV_LORA, B_HEADS * B_HEAD_PAD).astype(BF16)
    wvt = jnp.transpose(wkv[..., QK_NOPE:].reshape(nl, KV_LORA, B_HEADS * V_HEAD), (0, 2, 1)).astype(BF16)
    return w_in, wq, wk, wvt


def _trunk(x, mem, w):
    b, s, d = x.shape
    depth = w["mix_norm"].shape[0]
    tab_a = _rope_table(s, HEAD_DIM, 0, HEAD_DIM, 1.0, False)
    tab_bq = _rope_table(s, QK_ROPE, QK_NOPE, LANES, B_QK ** -0.5 * LOG2E, True)
    tab_bk = _rope_table(s, QK_ROPE, QK_NOPE, LANES, 1.0, False)
    for i in range(depth):
        x = _ffn(x.reshape(b * s, d), w["ffn1_norm"][i], w["ffn1_wg"][i], w["ffn1_wu"][i],
                 w["ffn1_wd"][i]).reshape(b, s, d)
        mk, mv = _mem_kv(mem, w["mem_norm"][i], w["w_mem_kv"][i])
        j = i // 2
        if i % 2 == 0:
            q, k, v, qc = _proj_a(x, w["mix_norm"][i], w["a_w_in"][j], tab_a)
            local = _win_attn(q, k, v, w["a_sink"][j])
        else:
            q, k, v, qc = _proj_b(x, w["mix_norm"][i], w["b_w_in"][j], w["b_q_norm"][j], w["b_wq"][j],
                                  w["b_kv_norm"][j], w["b_wk"][j], w["b_wvt"][j], tab_bq, tab_bk)
            local = _mla_attn(q, k, v)
        x = _out_proj(x, local, qc, mk, mv, w["w_o"][i])
        g_final = w["final_norm"] if i == depth - 1 else None
        x = _ffn(x.reshape(b * s, d), w["ffn2_norm"][i], w["ffn2_wg"][i], w["ffn2_wu"][i],
                 w["ffn2_wd"][i], g_final).reshape(b, s, d)
    return x


def kernel(x_prompt, x_sample, mem_prompt, mem_sample, ffn1_norm, ffn1_w_gu, ffn1_w_down, mix_norm,
           mem_norm, w_mem_kv, a_w_in, a_sink, b_w_in, b_q_norm, b_w_q_up, b_kv_norm, b_w_kv_up,
           w_o, ffn2_norm, ffn2_w_gu, ffn2_w_down, final_norm):
    w = {
        "ffn1_norm": ffn1_norm, "ffn2_norm": ffn2_norm, "mix_norm": mix_norm, "mem_norm": mem_norm,
        "w_mem_kv": w_mem_kv.astype(BF16), "a_w_in": a_w_in.astype(BF16), "a_sink": a_sink,
        "b_q_norm": b_q_norm, "b_kv_norm": b_kv_norm, "w_o": w_o.astype(BF16),
        "final_norm": final_norm,
    }
    w["ffn1_wg"], w["ffn1_wu"], w["ffn1_wd"] = _prep_ffn(ffn1_w_gu, ffn1_w_down)
    w["ffn2_wg"], w["ffn2_wu"], w["ffn2_wd"] = _prep_ffn(ffn2_w_gu, ffn2_w_down)
    w["b_w_in"], w["b_wq"], w["b_wk"], w["b_wvt"] = _prep_b(b_w_in, b_w_q_up, b_w_kv_up)
    return (_trunk(x_prompt, mem_prompt, w), _trunk(x_sample, mem_sample, w))
```

```python
import functools
import math

import jax
import jax.numpy as jnp
from jax import lax
from jax.experimental import pallas as pl
from jax.experimental.pallas import tpu as pltpu

D_MODEL = 1024
HEAD_DIM = 64
ROPE_THETA = 10000.0
NORM_EPS = 1e-6
D_FF = 2816
N_MEM = 256
MEM_HEADS = 4
MEM_WIDTH = MEM_HEADS * HEAD_DIM
A_Q_HEADS = 12
A_KV_HEADS = 4
A_GROUP = A_Q_HEADS // A_KV_HEADS
WINDOW = 128
A_Q_W = A_Q_HEADS * HEAD_DIM
A_KV_W = A_KV_HEADS * HEAD_DIM
B_HEADS = 12
Q_LORA = 384
KV_LORA = 256
QK_NOPE = 64
QK_ROPE = 32
V_HEAD = 64
B_QK = QK_NOPE + QK_ROPE
LOCAL_W = A_Q_W
NEG = -1e30
LOG2E = math.log2(math.e)

LANES = 128
B_HEAD_PAD = LANES
FF_CHUNK = 256
MLA_QB = 512
MLA_KB = 512
MLA_CHUNK = 32
MLA_DEN_ROWS = 16
MLA_AHEAD = 2
MLA_SLOTS = 2 * MLA_AHEAD
VMEM_LIMIT = 56 * 1024 * 1024

F32 = jnp.float32
BF16 = jnp.bfloat16


def _params(n_axes):
    return pltpu.CompilerParams(dimension_semantics=("arbitrary",) * n_axes,
                                vmem_limit_bytes=VMEM_LIMIT)


def _rms(x, g):
    return x * lax.rsqrt(jnp.mean(x * x, axis=-1, keepdims=True) + NORM_EPS) * g


def _dot(a, b):
    return jnp.dot(a, b, preferred_element_type=F32)


def _dot_nt(a, b):
    return lax.dot_general(a, b, (((1,), (1,)), ((), ())), preferred_element_type=F32)


def _resident(shape):
    zeros = (0,) * len(shape)
    return pl.BlockSpec(shape, lambda *_: zeros, pipeline_mode=pl.Buffered(1))


def _ffn_body(*refs, nch, final):
    if final:
        x_ref, g_ref, wg_ref, wu_ref, wd_ref, gf_ref, o_ref, h_ref = refs
    else:
        x_ref, g_ref, wg_ref, wu_ref, wd_ref, o_ref, h_ref = refs
    h_ref[...] = _rms(x_ref[...], g_ref[...]).astype(BF16)
    acc = None
    for c in range(nch):
        h = h_ref[...]
        gate = _dot(h, wg_ref[c])
        up = _dot(h, wu_ref[c])
        a = (gate / (1.0 + jnp.exp(-gate)) * up).astype(BF16)
        down = _dot(a, wd_ref[c])
        acc = down if acc is None else acc + down
    y = x_ref[...] + 0.5 * acc
    if final:
        y = _rms(y, gf_ref[...])
    o_ref[...] = y


def _ffn(x, g, wg, wu, wd, g_final=None):
    t, d = x.shape
    nch, _, fc = wg.shape
    tm = min(512, t)
    final = g_final is not None
    in_specs = [
        pl.BlockSpec((tm, d), lambda i: (i, 0)),
        _resident((1, d)),
        _resident(wg.shape),
        _resident(wu.shape),
        _resident(wd.shape),
    ]
    args = [x, g.reshape(1, d), wg, wu, wd]
    if final:
        in_specs.append(_resident((1, d)))
        args.append(g_final.reshape(1, d))
    return pl.pallas_call(
        functools.partial(_ffn_body, nch=nch, final=final),
        grid=(t // tm,),
        in_specs=in_specs,
        out_specs=pl.BlockSpec((tm, d), lambda i: (i, 0)),
        out_shape=jax.ShapeDtypeStruct((t, d), F32),
        scratch_shapes=[pltpu.VMEM((tm, d), BF16)],
        compiler_params=_params(1),
        name="ffn_final" if final else "ffn",
    )(*args)


def _memkv_body(mem_ref, g_ref, w_ref, mk_ref, mv_ref):
    h = _rms(mem_ref[...], g_ref[...]).astype(BF16)
    kv = _dot(h, w_ref[...])
    mk_ref[...] = kv[:, :MEM_WIDTH].astype(BF16)
    mv_ref[...] = kv[:, MEM_WIDTH:].astype(BF16)


def _mem_kv(mem, g, w):
    b, n, d = mem.shape
    out = jax.ShapeDtypeStruct((b, n, MEM_WIDTH), BF16)
    blk = pl.BlockSpec((None, n, MEM_WIDTH), lambda i: (i, 0, 0))
    return pl.pallas_call(
        _memkv_body,
        grid=(b,),
        in_specs=[pl.BlockSpec((None, n, d), lambda i: (i, 0, 0)), _resident((1, d)), _resident(w.shape)],
        out_specs=[blk, blk],
        out_shape=[out, out],
        compiler_params=_params(1),
        name="mem_kv",
    )(mem, g.reshape(1, d), w)


def _rope_block(xb, tab_ref, shift):
    c = tab_ref[:, 0:LANES]
    s_plus = tab_ref[:, LANES:2 * LANES]
    s_minus = tab_ref[:, 2 * LANES:3 * LANES]
    return (xb * c + pltpu.roll(xb, shift, 1) * s_plus
            + pltpu.roll(xb, LANES - shift, 1) * s_minus)


def _rope_table(seq, dim, lane_start, period, scale, pass_through):
    half = dim // 2
    inv = 1.0 / (ROPE_THETA ** (jnp.arange(0, dim, 2, dtype=F32) / dim))
    ang = jnp.arange(seq, dtype=F32)[:, None] * inv[None, :]
    cos, sin = jnp.cos(ang), jnp.sin(ang)
    zeros_h = jnp.zeros((seq, half), F32)
    lead = jnp.full((seq, lane_start), 1.0 if pass_through else 0.0, F32)
    lead0 = jnp.zeros((seq, lane_start), F32)
    tail0 = jnp.zeros((seq, period - lane_start - dim), F32)
    reps = LANES // period
    c = jnp.tile(jnp.concatenate([lead, cos, cos, tail0], -1), (1, reps))
    s_plus = jnp.tile(jnp.concatenate([lead0, zeros_h, sin, tail0], -1), (1, reps))
    s_minus = jnp.tile(jnp.concatenate([lead0, -sin, zeros_h, tail0], -1), (1, reps))
    return jnp.concatenate([c, s_plus, s_minus], -1) * scale


A_HEAD_ORDER = (0, 3, 1, 4, 2, 5, 6, 9, 7, 10, 8, 11)
A_KV_TILES = A_KV_W // LANES
A_TILE_HEADS = A_GROUP
WIN_QBLOCKS = 2


def _proja_body(x_ref, g_ref, w_ref, wvt_ref, tab_ref, q_ref, k_ref, vt_ref, qc_ref, *, qscale):
    h = _rms(x_ref[...], g_ref[...]).astype(BF16)
    proj = _dot(h, w_ref[...])
    nq = A_Q_W // LANES
    nk = A_KV_W // LANES
    for j in range(nq + nk):
        rb = _rope_block(proj[:, LANES * j:LANES * (j + 1)], tab_ref, HEAD_DIM // 2)
        if j < nq:
            q_ref[:, LANES * j:LANES * (j + 1)] = (rb * qscale).astype(BF16)
        else:
            k_ref[:, LANES * (j - nq):LANES * (j - nq + 1)] = rb.astype(BF16)
    qc_ref[...] = (proj[:, A_Q_W + A_KV_W:] * qscale).astype(BF16)
    vt = _dot_nt(wvt_ref[...], h).astype(BF16)
    for i in range(vt_ref.shape[0]):
        vt_ref[i] = vt[:, WINDOW * i:WINDOW * (i + 1)]


def _proj_a(x, g, w, wvt, tab):
    b, s, d = x.shape
    tm = min(512, s)
    nblk = tm // WINDOW

    def out(width):
        return (jax.ShapeDtypeStruct((b, s, width), BF16),
                pl.BlockSpec((None, tm, width), lambda i, j: (i, j, 0)))

    vt = (jax.ShapeDtypeStruct((b, s // WINDOW, A_KV_W, WINDOW), BF16),
          pl.BlockSpec((None, nblk, A_KV_W, WINDOW), lambda i, j: (i, j, 0, 0)))
    shapes, specs = zip(out(A_Q_W), out(A_KV_W), vt, out(MEM_WIDTH))
    return pl.pallas_call(
        functools.partial(_proja_body, qscale=HEAD_DIM ** -0.5 * LOG2E),
        grid=(b, s // tm),
        in_specs=[
            pl.BlockSpec((None, tm, d), lambda i, j: (i, j, 0)),
            _resident((1, d)),
            _resident(w.shape),
            _resident(wvt.shape),
            pl.BlockSpec((tm, 3 * LANES), lambda i, j: (j, 0)),
        ],
        out_specs=list(specs),
        out_shape=list(shapes),
        compiler_params=_params(2),
        name="proj_a",
    )(x, g.reshape(1, d), w, wvt, tab)


def _wattn_body(bias_ref, sink_ref, q_ref, kp_ref, kc_ref, kn_ref, vp_ref, vc_ref, vn_ref, o_ref, *, nsteps):
    step = pl.program_id(1)
    lane = lax.broadcasted_iota(jnp.int32, (WINDOW, LANES), 1)
    ones = jnp.ones((MLA_DEN_ROWS, 3 * WINDOW), BF16)
    zero = jnp.zeros((WINDOW, LANES), BF16)
    for blk in range(WIN_QBLOCKS):
        rows = slice(WINDOW * blk, WINDOW * (blk + 1))
        variant = 1
        if blk == 0:
            variant = jnp.where(step == 0, 0, variant)
        if blk == WIN_QBLOCKS - 1:
            variant = jnp.where(step == nsteps - 1, 2, variant)
        bias = bias_ref[variant]
        for tile in range(A_KV_TILES):
            ts = slice(LANES * tile, LANES * (tile + 1))
            k_blocks = ([kp_ref[:, ts]] + [kc_ref[WINDOW * i:WINDOW * (i + 1), ts] for i in range(WIN_QBLOCKS)]
                        + [kn_ref[:, ts]])
            vt_blocks = [vp_ref[ts, :]] + [vc_ref[i, ts, :] for i in range(WIN_QBLOCKS)] + [vn_ref[ts, :]]
            k_band = jnp.concatenate(k_blocks[blk:blk + 3], axis=0)
            vt_band = jnp.concatenate(vt_blocks[blk:blk + 3], axis=1)
            q_tiles = [q_ref[rows, LANES * (A_TILE_HEADS * tile + r):LANES * (A_TILE_HEADS * tile + r + 1)]
                       for r in range(A_TILE_HEADS)]
            halves = []
            for half in range(2):
                group = 2 * tile + half
                in_half = (lane < HEAD_DIM) if half == 0 else (lane >= HEAD_DIM)
                q_stack = jnp.concatenate([jnp.where(in_half, qt, zero) for qt in q_tiles], axis=0)
                s = _dot_nt(k_band, q_stack) + bias
                sink = sink_ref[group]
                m = jnp.maximum(jnp.max(s, axis=0, keepdims=True), sink)
                p = jnp.exp2(s - m).astype(BF16)
                v_aug = jnp.concatenate([vt_band[HEAD_DIM * half:HEAD_DIM * (half + 1), :], ones], axis=0)
                o_aug = _dot(v_aug, p)
                den = o_aug[HEAD_DIM:HEAD_DIM + 1, :] + jnp.exp2(sink - m)
                halves.append(o_aug[:HEAD_DIM, :] / den)
            for r in range(A_TILE_HEADS):
                cs = slice(WINDOW * r, WINDOW * (r + 1))
                out_t = jnp.concatenate([halves[0][:, cs], halves[1][:, cs]], axis=0)
                j = A_TILE_HEADS * tile + r
                o_ref[rows, LANES * j:LANES * (j + 1)] = out_t.T.astype(BF16)


def _band_bias(dtype=F32):
    kj = jnp.arange(3 * WINDOW)[:, None]
    qi = (jnp.arange(3 * WINDOW) % WINDOW)[None, :]
    band = (kj - qi >= 0) & (kj - qi <= 2 * WINDOW)
    first = band & (kj >= WINDOW)
    last = band & (kj < 2 * WINDOW)
    return jnp.where(jnp.stack([first, band, last]), 0.0, NEG).astype(dtype)


def _win_attn(q, k, vt, sink):
    b, s, _ = q.shape
    nb = s // WINDOW
    nsteps = nb // WIN_QBLOCKS
    assert nb >= 2 and nb % WIN_QBLOCKS == 0
    qrows = WIN_QBLOCKS * WINDOW
    sink_rows = jnp.repeat(sink.reshape(A_KV_HEADS, 1, A_GROUP) * LOG2E, WINDOW, axis=-1)

    def halo(j, shift):
        return jnp.clip(WIN_QBLOCKS * j + shift, 0, nb - 1)

    def k_halo(shift):
        return pl.BlockSpec((None, WINDOW, A_KV_W), lambda i, j: (i, halo(j, shift), 0))

    def vt_halo(shift):
        return pl.BlockSpec((None, None, A_KV_W, WINDOW), lambda i, j: (i, halo(j, shift), 0, 0))

    return pl.pallas_call(
        functools.partial(_wattn_body, nsteps=nsteps),
        grid=(b, nsteps),
        in_specs=[
            _resident((3, 3 * WINDOW, 3 * WINDOW)),
            _resident((A_KV_HEADS, 1, 3 * WINDOW)),
            pl.BlockSpec((None, qrows, A_Q_W), lambda i, j: (i, j, 0)),
            k_halo(-1),
            pl.BlockSpec((None, qrows, A_KV_W), lambda i, j: (i, j, 0)),
            k_halo(WIN_QBLOCKS),
            vt_halo(-1),
            pl.BlockSpec((None, WIN_QBLOCKS, A_KV_W, WINDOW), lambda i, j: (i, j, 0, 0)),
            vt_halo(WIN_QBLOCKS),
        ],
        out_specs=pl.BlockSpec((None, qrows, A_Q_W), lambda i, j: (i, j, 0)),
        out_shape=jax.ShapeDtypeStruct((b, s, A_Q_W), BF16),
        compiler_params=_params(2),
        name="win_attn",
    )(_band_bias(), sink_rows, q, k, k, k, vt, vt, vt)


def _projb_body(x_ref, g_ref, win_ref, gq_ref, wq_ref, gkv_ref, wk_ref, wvt_ref, tabq_ref, tabk_ref,
                q_ref, k_ref, v_ref, qc_ref, *, qcscale):
    h = _rms(x_ref[...], g_ref[...]).astype(BF16)
    proj = _dot(h, win_ref[...])
    c_q = _rms(proj[:, :Q_LORA], gq_ref[...]).astype(BF16)
    kv0 = Q_LORA
    c_kv = _rms(proj[:, kv0:kv0 + KV_LORA], gkv_ref[...]).astype(BF16)
    qc0 = kv0 + KV_LORA
    qc_ref[...] = (proj[:, qc0:qc0 + MEM_WIDTH] * qcscale).astype(BF16)
    kr0 = qc0 + MEM_WIDTH
    k_rope = _rope_block(proj[:, kr0:kr0 + LANES], tabk_ref, QK_ROPE // 2)
    q_all = _dot(c_q, wq_ref[...])
    k_all = _dot(c_kv, wk_ref[...])
    for hd in range(B_HEADS):
        hs = slice(B_HEAD_PAD * hd, B_HEAD_PAD * (hd + 1))
        q_ref[:, hs] = _rope_block(q_all[:, hs], tabq_ref, QK_ROPE // 2).astype(BF16)
        k_ref[:, hs] = (k_all[:, hs] + k_rope).astype(BF16)
    v_ref[...] = _dot_nt(wvt_ref[...], c_kv).astype(BF16)


def _proj_b(x, g, w_in, gq, wq, gkv, wk, wvt, tabq, tabk):
    b, s, d = x.shape
    tm = min(MLA_KB, s)

    def out(width):
        return (jax.ShapeDtypeStruct((b, s, width), BF16),
                pl.BlockSpec((None, tm, width), lambda i, j: (i, j, 0)))

    vt = (jax.ShapeDtypeStruct((b, s // tm, B_HEADS * V_HEAD, tm), BF16),
          pl.BlockSpec((None, None, B_HEADS * V_HEAD, tm), lambda i, j: (i, j, 0, 0)))
    shapes, specs = zip(out(B_HEADS * B_HEAD_PAD), out(B_HEADS * B_HEAD_PAD), vt, out(MEM_WIDTH))
    tab_spec = pl.BlockSpec((tm, 3 * LANES), lambda i, j: (j, 0))
    return pl.pallas_call(
        functools.partial(_projb_body, qcscale=HEAD_DIM ** -0.5 * LOG2E),
        grid=(b, s // tm),
        in_specs=[
            pl.BlockSpec((None, tm, d), lambda i, j: (i, j, 0)),
            _resident((1, d)),
            _resident(w_in.shape),
            _resident((1, Q_LORA)),
            _resident(wq.shape),
            _resident((1, KV_LORA)),
            _resident(wk.shape),
            _resident(wvt.shape),
            tab_spec, tab_spec,
        ],
        out_specs=list(specs),
        out_shape=list(shapes),
        compiler_params=_params(2),
        name="proj_b",
    )(x, g.reshape(1, d), w_in, gq.reshape(1, -1), wq, gkv.reshape(1, -1), wk, wvt, tabq, tabk)


def _mla_body(q_ref, k_ref, vt_ref, o_ref, *scratch, kb, nkv):
    per_slot = 3 * 2
    slots = [scratch[per_slot * i:per_slot * (i + 1)] for i in range(MLA_SLOTS)]
    s_refs = [sl[0:2] for sl in slots]
    p_refs = [sl[2:4] for sl in slots]
    alpha_refs = [sl[4:6] for sl in slots]
    m_refs, acc_refs = (scratch[per_slot * MLA_SLOTS + 2 * i:per_slot * MLA_SLOTS + 2 * (i + 1)]
                        for i in range(2))
    hslices = [slice(B_HEAD_PAD * hh, B_HEAD_PAD * (hh + 1)) for hh in range(2)]
    vslices = [slice(V_HEAD * hh, V_HEAD * (hh + 1)) for hh in range(2)]
    chunks = [slice(c, c + MLA_CHUNK) for c in range(0, kb, MLA_CHUNK)]

    def scores(t, slot):
        rows = pl.ds(pl.multiple_of(t * kb, kb), kb)
        for hh, hs in enumerate(hslices):
            s_refs[slot][hh][...] = _dot_nt(k_ref[rows, hs], q_ref[:, hs])

    def softmax(slot):
        for hh in range(2):
            s_ref, p_ref = s_refs[slot][hh], p_refs[slot][hh]
            cmax = s_ref[chunks[0], :]
            for ch in chunks[1:]:
                cmax = jnp.maximum(cmax, s_ref[ch, :])
            m = m_refs[hh][...]
            m_new = jnp.maximum(m, jnp.max(cmax, axis=0, keepdims=True))
            alpha = jnp.exp2(m - m_new)
            for ch in chunks:
                p_ref[ch, :] = jnp.exp2(s_ref[ch, :] - m_new).astype(BF16)
            alpha_refs[slot][hh][...] = alpha
            m_refs[hh][...] = m_new

    def accumulate(t, slot):
        ones = jnp.ones((MLA_DEN_ROWS, kb), BF16)
        for hh, vs in enumerate(vslices):
            v_aug = jnp.concatenate([vt_ref[t, vs, :], ones], axis=0)
            acc_refs[hh][...] = (alpha_refs[slot][hh][...] * acc_refs[hh][...]
                                 + _dot(v_aug, p_refs[slot][hh][...]))

    def stage(t, phase, ahead=True, behind=True):
        if ahead:
            scores(t + MLA_AHEAD, (phase + MLA_AHEAD) % MLA_SLOTS)
        if behind:
            accumulate(t - MLA_AHEAD, (phase - MLA_AHEAD) % MLA_SLOTS)
        softmax(phase)

    for hh in range(2):
        m_refs[hh][...] = jnp.full(m_refs[hh].shape, NEG, F32)
        acc_refs[hh][...] = jnp.zeros(acc_refs[hh].shape, F32)
    for t in range(min(MLA_AHEAD, nkv)):
        scores(t, t % MLA_SLOTS)
    lo, hi = MLA_AHEAD, nkv - MLA_AHEAD
    trips = max(hi - lo, 0) // MLA_SLOTS
    for t in range(min(lo, nkv)):
        stage(t, t % MLA_SLOTS, ahead=t + MLA_AHEAD < nkv, behind=False)

    def full_stages(i, carry):
        for j in range(MLA_SLOTS):
            stage(lo + MLA_SLOTS * i + j, (lo + j) % MLA_SLOTS)
        return carry

    lax.fori_loop(0, trips, full_stages, 0)
    for t in range(lo + trips * MLA_SLOTS, nkv):
        stage(t, t % MLA_SLOTS, ahead=t + MLA_AHEAD < nkv, behind=t >= MLA_AHEAD)
    for t in range(max(nkv - MLA_AHEAD, 0), nkv):
        accumulate(t, t % MLA_SLOTS)
    out_t = jnp.concatenate([acc_refs[hh][:V_HEAD, :] / acc_refs[hh][V_HEAD:V_HEAD + 1, :] for hh in range(2)],
                            axis=0)
    o_ref[...] = out_t.T.astype(BF16)


def _mla_attn(q, k, vt):
    b, s, _ = q.shape
    _, nkv, _, kb = vt.shape
    qb = min(MLA_QB, s)
    npair = B_HEADS // 2
    per_slot = ([pltpu.VMEM((kb, qb), F32)] * 2
                + [pltpu.VMEM((kb, qb), BF16)] * 2
                + [pltpu.VMEM((1, qb), F32)] * 2)
    state = ([pltpu.VMEM((1, qb), F32)] * 2
             + [pltpu.VMEM((V_HEAD + MLA_DEN_ROWS, qb), F32)] * 2)
    return pl.pallas_call(
        functools.partial(_mla_body, kb=kb, nkv=nkv),
        grid=(b, npair, s // qb),
        in_specs=[
            pl.BlockSpec((None, qb, 2 * B_HEAD_PAD), lambda i, j, n: (i, n, j)),
            pl.BlockSpec((None, s, 2 * B_HEAD_PAD), lambda i, j, n: (i, 0, j)),
            pl.BlockSpec((None, nkv, 2 * V_HEAD, kb), lambda i, j, n: (i, 0, j, 0)),
        ],
        out_specs=pl.BlockSpec((None, qb, 2 * V_HEAD), lambda i, j, n: (i, n, j)),
        out_shape=jax.ShapeDtypeStruct((b, s, B_HEADS * V_HEAD), BF16),
        scratch_shapes=per_slot * MLA_SLOTS + state,
        compiler_params=_params(3),
        name="mla_attn",
    )(q, k, vt)


def _oproj_body(x_ref, loc_ref, qc_ref, mk_ref, mv_ref, wo_ref, o_ref):
    heads = []
    for hd in range(MEM_HEADS):
        hs = slice(HEAD_DIM * hd, HEAD_DIM * (hd + 1))
        s = _dot_nt(qc_ref[:, hs], mk_ref[:, hs])
        p = jnp.exp2(s - jnp.max(s, axis=-1, keepdims=True))
        o = _dot(p.astype(BF16), mv_ref[:, hs]) / jnp.sum(p, axis=-1, keepdims=True)
        heads.append(o.astype(BF16))
    cross = jnp.concatenate(heads, axis=-1)
    y = _dot(loc_ref[...], wo_ref[:LOCAL_W, :]) + _dot(cross, wo_ref[LOCAL_W:, :])
    o_ref[...] = x_ref[...] + y


def _out_proj(x, local, qc, mk, mv, wo):
    b, s, d = x.shape
    tm = min(512, s)

    def tok(width):
        return pl.BlockSpec((None, tm, width), lambda i, j: (i, j, 0))

    mem = pl.BlockSpec((None, N_MEM, MEM_WIDTH), lambda i, j: (i, 0, 0))
    return pl.pallas_call(
        _oproj_body,
        grid=(b, s // tm),
        in_specs=[tok(d), tok(LOCAL_W), tok(MEM_WIDTH), mem, mem, _resident(wo.shape)],
        out_specs=tok(d),
        out_shape=jax.ShapeDtypeStruct((b, s, d), F32),
        compiler_params=_params(2),
        name="out_proj",
    )(x, local, qc, mk, mv, wo)


def _prep_ffn(w_gu, w_down):
    depth, d, _ = w_gu.shape
    nch = D_FF // FF_CHUNK
    w = w_gu.astype(BF16).reshape(depth, d, 2, nch, FF_CHUNK)
    wg = jnp.transpose(w[:, :, 0], (0, 2, 1, 3))
    wu = jnp.transpose(w[:, :, 1], (0, 2, 1, 3))
    wd = w_down.astype(BF16).reshape(depth, nch, FF_CHUNK, d)
    return wg, wu, wd


def _prep_a(a_w_in):
    nl, d, _ = a_w_in.shape
    order = jnp.array(A_HEAD_ORDER)
    q = a_w_in[:, :, :A_Q_W].reshape(nl, d, A_Q_HEADS, HEAD_DIM)[:, :, order].reshape(nl, d, A_Q_W)
    k = a_w_in[:, :, A_Q_W:A_Q_W + A_KV_W]
    v = a_w_in[:, :, A_Q_W + A_KV_W:A_Q_W + 2 * A_KV_W]
    qc = a_w_in[:, :, A_Q_W + 2 * A_KV_W:]
    w = jnp.concatenate([q, k, qc], axis=-1).astype(BF16)
    return w, jnp.transpose(v, (0, 2, 1)).astype(BF16)


def _prep_w_o(w_o):
    depth, _, d = w_o.shape
    order = jnp.array(A_HEAD_ORDER)
    local = w_o[:, :LOCAL_W].reshape(depth, A_Q_HEADS, HEAD_DIM, d)
    local = jnp.where((jnp.arange(depth) % 2 == 0)[:, None, None, None], local[:, order], local)
    return jnp.concatenate([local.reshape(depth, LOCAL_W, d), w_o[:, LOCAL_W:]], axis=1).astype(BF16)


def _prep_b(b_w_in, b_w_q_up, b_w_kv_up):
    nl, d, _ = b_w_in.shape
    c_q = b_w_in[:, :, :Q_LORA]
    c_kv = b_w_in[:, :, Q_LORA:Q_LORA + KV_LORA]
    k_r = b_w_in[:, :, Q_LORA + KV_LORA:Q_LORA + KV_LORA + QK_ROPE]
    qc = b_w_in[:, :, Q_LORA + KV_LORA + QK_ROPE:]
    kr_tile = jnp.pad(k_r, ((0, 0), (0, 0), (QK_NOPE, LANES - QK_NOPE - QK_ROPE)))
    w_in = jnp.concatenate([c_q, c_kv, qc, kr_tile], axis=-1).astype(BF16)
    wq = b_w_q_up.reshape(nl, Q_LORA, B_HEADS, B_QK)
    wq = jnp.pad(wq, ((0, 0), (0, 0), (0, 0), (0, B_HEAD_PAD - B_QK)))
    wq = wq.reshape(nl, Q_LORA, B_HEADS * B_HEAD_PAD).astype(BF16)
    wkv = b_w_kv_up.reshape(nl, KV_LORA, B_HEADS, QK_NOPE + V_HEAD)
    wk = jnp.pad(wkv[..., :QK_NOPE], ((0, 0), (0, 0), (0, 0), (0, B_HEAD_PAD - QK_NOPE)))
    wk = wk.reshape(nl, KV_LORA, B_HEADS * B_HEAD_PAD).astype(BF16)
    wvt = jnp.transpose(wkv[..., QK_NOPE:].reshape(nl, KV_LORA, B_HEADS * V_HEAD), (0, 2, 1)).astype(BF16)
    return w_in, wq, wk, wvt


def _trunk(x, mem, w):
    b, s, d = x.shape
    depth = w["mix_norm"].shape[0]
    tab_a = _rope_table(s, HEAD_DIM, 0, HEAD_DIM, 1.0, False)
    tab_bq = _rope_table(s, QK_ROPE, QK_NOPE, LANES, B_QK ** -0.5 * LOG2E, True)
    tab_bk = _rope_table(s, QK_ROPE, QK_NOPE, LANES, 1.0, False)
    for i in range(depth):
        x = _ffn(x.reshape(b * s, d), w["ffn1_norm"][i], w["ffn1_wg"][i], w["ffn1_wu"][i],
                 w["ffn1_wd"][i]).reshape(b, s, d)
        mk, mv = _mem_kv(mem, w["mem_norm"][i], w["w_mem_kv"][i])
        j = i // 2
        if i % 2 == 0:
            q, k, vt, qc = _proj_a(x, w["mix_norm"][i], w["a_w_in"][j], w["a_wvt"][j], tab_a)
            local = _win_attn(q, k, vt, w["a_sink"][j])
        else:
            q, k, v, qc = _proj_b(x, w["mix_norm"][i], w["b_w_in"][j], w["b_q_norm"][j], w["b_wq"][j],
                                  w["b_kv_norm"][j], w["b_wk"][j], w["b_wvt"][j], tab_bq, tab_bk)
            local = _mla_attn(q, k, v)
        x = _out_proj(x, local, qc, mk, mv, w["w_o"][i])
        g_final = w["final_norm"] if i == depth - 1 else None
        x = _ffn(x.reshape(b * s, d), w["ffn2_norm"][i], w["ffn2_wg"][i], w["ffn2_wu"][i],
                 w["ffn2_wd"][i], g_final).reshape(b, s, d)
    return x


def kernel(x_prompt, x_sample, mem_prompt, mem_sample, ffn1_norm, ffn1_w_gu, ffn1_w_down, mix_norm,
           mem_norm, w_mem_kv, a_w_in, a_sink, b_w_in, b_q_norm, b_w_q_up, b_kv_norm, b_w_kv_up,
           w_o, ffn2_norm, ffn2_w_gu, ffn2_w_down, final_norm):
    w = {
        "ffn1_norm": ffn1_norm, "ffn2_norm": ffn2_norm, "mix_norm": mix_norm, "mem_norm": mem_norm,
        "w_mem_kv": w_mem_kv.astype(BF16), "a_sink": a_sink,
        "b_q_norm": b_q_norm, "b_kv_norm": b_kv_norm, "w_o": _prep_w_o(w_o),
        "final_norm": final_norm,
    }
    w["ffn1_wg"], w["ffn1_wu"], w["ffn1_wd"] = _prep_ffn(ffn1_w_gu, ffn1_w_down)
    w["ffn2_wg"], w["ffn2_wu"], w["ffn2_wd"] = _prep_ffn(ffn2_w_gu, ffn2_w_down)
    w["a_w_in"], w["a_wvt"] = _prep_a(a_w_in)
    w["b_w_in"], w["b_wq"], w["b_wk"], w["b_wvt"] = _prep_b(b_w_in, b_w_q_up, b_w_kv_up)
    return (_trunk(x_prompt, mem_prompt, w), _trunk(x_sample, mem_sample, w))
```

```python
import functools
import math

import jax
import jax.numpy as jnp
from jax import lax
from jax.experimental import pallas as pl
from jax.experimental.pallas import tpu as pltpu

D_MODEL = 1024
HEAD_DIM = 64
ROPE_THETA = 10000.0
NORM_EPS = 1e-6
D_FF = 2816
N_MEM = 256
MEM_HEADS = 4
MEM_WIDTH = MEM_HEADS * HEAD_DIM
A_Q_HEADS = 12
A_KV_HEADS = 4
A_GROUP = A_Q_HEADS // A_KV_HEADS
WINDOW = 128
A_Q_W = A_Q_HEADS * HEAD_DIM
A_KV_W = A_KV_HEADS * HEAD_DIM
B_HEADS = 12
Q_LORA = 384
KV_LORA = 256
QK_NOPE = 64
QK_ROPE = 32
V_HEAD = 64
B_QK = QK_NOPE + QK_ROPE
LOCAL_W = A_Q_W
NEG = -1e30
LOG2E = math.log2(math.e)

LANES = 128
B_HEAD_PAD = LANES
FF_CHUNK = 256
MLA_QB = 512
MLA_KB = 512
MLA_CHUNK = 32
MLA_DEN_ROWS = 16
MLA_AHEAD = 2
MLA_SLOTS = 2 * MLA_AHEAD
VMEM_LIMIT = 56 * 1024 * 1024

F32 = jnp.float32
BF16 = jnp.bfloat16


def _params(n_axes):
    return pltpu.CompilerParams(dimension_semantics=("arbitrary",) * n_axes,
                                vmem_limit_bytes=VMEM_LIMIT)


def _rms(x, g):
    return x * lax.rsqrt(jnp.mean(x * x, axis=-1, keepdims=True) + NORM_EPS) * g


def _dot(a, b):
    return jnp.dot(a, b, preferred_element_type=F32)


def _dot_nt(a, b):
    return lax.dot_general(a, b, (((1,), (1,)), ((), ())), preferred_element_type=F32)


def _resident(shape):
    zeros = (0,) * len(shape)
    return pl.BlockSpec(shape, lambda *_: zeros, pipeline_mode=pl.Buffered(1))


def _ffn_body(*refs, nch, final):
    if final:
        x_ref, g_ref, wg_ref, wu_ref, wd_ref, gf_ref, o_ref, h_ref = refs
    else:
        x_ref, g_ref, wg_ref, wu_ref, wd_ref, o_ref, h_ref = refs
    h_ref[...] = _rms(x_ref[...], g_ref[...]).astype(BF16)
    acc = None
    for c in range(nch):
        h = h_ref[...]
        gate = _dot(h, wg_ref[c])
        up = _dot(h, wu_ref[c])
        a = (gate / (1.0 + jnp.exp(-gate)) * up).astype(BF16)
        down = _dot(a, wd_ref[c])
        acc = down if acc is None else acc + down
    y = x_ref[...] + 0.5 * acc
    if final:
        y = _rms(y, gf_ref[...])
    o_ref[...] = y


def _ffn(x, g, wg, wu, wd, g_final=None):
    t, d = x.shape
    nch, _, fc = wg.shape
    tm = min(512, t)
    final = g_final is not None
    in_specs = [
        pl.BlockSpec((tm, d), lambda i: (i, 0)),
        _resident((1, d)),
        _resident(wg.shape),
        _resident(wu.shape),
        _resident(wd.shape),
    ]
    args = [x, g.reshape(1, d), wg, wu, wd]
    if final:
        in_specs.append(_resident((1, d)))
        args.append(g_final.reshape(1, d))
    return pl.pallas_call(
        functools.partial(_ffn_body, nch=nch, final=final),
        grid=(t // tm,),
        in_specs=in_specs,
        out_specs=pl.BlockSpec((tm, d), lambda i: (i, 0)),
        out_shape=jax.ShapeDtypeStruct((t, d), F32),
        scratch_shapes=[pltpu.VMEM((tm, d), BF16)],
        compiler_params=_params(1),
        name="ffn_final" if final else "ffn",
    )(*args)


def _memkv_body(mem_ref, g_ref, wk_ref, wvt_ref, mk_ref, mvt_ref):
    h = _rms(mem_ref[...], g_ref[...]).astype(BF16)
    mk_ref[...] = _dot(h, wk_ref[...]).astype(BF16)
    mvt_ref[...] = _dot_nt(wvt_ref[...], h).astype(BF16)


def _mem_kv(mem, g, wk, wvt):
    b, n, d = mem.shape
    return pl.pallas_call(
        _memkv_body,
        grid=(b,),
        in_specs=[pl.BlockSpec((None, n, d), lambda i: (i, 0, 0)), _resident((1, d)),
                  _resident(wk.shape), _resident(wvt.shape)],
        out_specs=[pl.BlockSpec((None, n, MEM_WIDTH), lambda i: (i, 0, 0)),
                   pl.BlockSpec((None, MEM_WIDTH, n), lambda i: (i, 0, 0))],
        out_shape=[jax.ShapeDtypeStruct((b, n, MEM_WIDTH), BF16),
                   jax.ShapeDtypeStruct((b, MEM_WIDTH, n), BF16)],
        compiler_params=_params(1),
        name="mem_kv",
    )(mem, g.reshape(1, d), wk, wvt)


def _rope_block(xb, tab_ref, shift):
    c = tab_ref[:, 0:LANES]
    s_plus = tab_ref[:, LANES:2 * LANES]
    s_minus = tab_ref[:, 2 * LANES:3 * LANES]
    return (xb * c + pltpu.roll(xb, shift, 1) * s_plus
            + pltpu.roll(xb, LANES - shift, 1) * s_minus)


def _rope_table(seq, dim, lane_start, period, scale, pass_through):
    half = dim // 2
    inv = 1.0 / (ROPE_THETA ** (jnp.arange(0, dim, 2, dtype=F32) / dim))
    ang = jnp.arange(seq, dtype=F32)[:, None] * inv[None, :]
    cos, sin = jnp.cos(ang), jnp.sin(ang)
    zeros_h = jnp.zeros((seq, half), F32)
    lead = jnp.full((seq, lane_start), 1.0 if pass_through else 0.0, F32)
    lead0 = jnp.zeros((seq, lane_start), F32)
    tail0 = jnp.zeros((seq, period - lane_start - dim), F32)
    reps = LANES // period
    c = jnp.tile(jnp.concatenate([lead, cos, cos, tail0], -1), (1, reps))
    s_plus = jnp.tile(jnp.concatenate([lead0, zeros_h, sin, tail0], -1), (1, reps))
    s_minus = jnp.tile(jnp.concatenate([lead0, -sin, zeros_h, tail0], -1), (1, reps))
    return jnp.concatenate([c, s_plus, s_minus], -1) * scale


A_HEAD_ORDER = (0, 3, 1, 4, 2, 5, 6, 9, 7, 10, 8, 11)
A_KV_TILES = A_KV_W // LANES
A_TILE_HEADS = A_GROUP
WIN_QBLOCKS = 2


def _proja_body(x_ref, g_ref, w_ref, wvt_ref, tab_ref, q_ref, k_ref, vt_ref, qc_ref, *, qscale):
    h = _rms(x_ref[...], g_ref[...]).astype(BF16)
    proj = _dot(h, w_ref[...])
    nq = A_Q_W // LANES
    nk = A_KV_W // LANES
    for j in range(nq + nk):
        rb = _rope_block(proj[:, LANES * j:LANES * (j + 1)], tab_ref, HEAD_DIM // 2)
        if j < nq:
            q_ref[:, LANES * j:LANES * (j + 1)] = (rb * qscale).astype(BF16)
        else:
            k_ref[:, LANES * (j - nq):LANES * (j - nq + 1)] = rb.astype(BF16)
    qc_ref[...] = (proj[:, A_Q_W + A_KV_W:] * qscale).astype(BF16)
    vt = _dot_nt(wvt_ref[...], h).astype(BF16)
    for i in range(vt_ref.shape[0]):
        vt_ref[i] = vt[:, WINDOW * i:WINDOW * (i + 1)]


def _proj_a(x, g, w, wvt, tab):
    b, s, d = x.shape
    tm = min(512, s)
    nblk = tm // WINDOW

    def out(width):
        return (jax.ShapeDtypeStruct((b, s, width), BF16),
                pl.BlockSpec((None, tm, width), lambda i, j: (i, j, 0)))

    vt = (jax.ShapeDtypeStruct((b, s // WINDOW, A_KV_W, WINDOW), BF16),
          pl.BlockSpec((None, nblk, A_KV_W, WINDOW), lambda i, j: (i, j, 0, 0)))
    shapes, specs = zip(out(A_Q_W), out(A_KV_W), vt, out(MEM_WIDTH))
    return pl.pallas_call(
        functools.partial(_proja_body, qscale=HEAD_DIM ** -0.5 * LOG2E),
        grid=(b, s // tm),
        in_specs=[
            pl.BlockSpec((None, tm, d), lambda i, j: (i, j, 0)),
            _resident((1, d)),
            _resident(w.shape),
            _resident(wvt.shape),
            pl.BlockSpec((tm, 3 * LANES), lambda i, j: (j, 0)),
        ],
        out_specs=list(specs),
        out_shape=list(shapes),
        compiler_params=_params(2),
        name="proj_a",
    )(x, g.reshape(1, d), w, wvt, tab)


def _wattn_body(bias_ref, sink_ref, q_ref, kp_ref, kc_ref, kn_ref, vp_ref, vc_ref, vn_ref, o_ref, *, nsteps):
    step = pl.program_id(1)
    lane = lax.broadcasted_iota(jnp.int32, (WINDOW, LANES), 1)
    ones = jnp.ones((MLA_DEN_ROWS, 3 * WINDOW), BF16)
    zero = jnp.zeros((WINDOW, LANES), BF16)
    for blk in range(WIN_QBLOCKS):
        rows = slice(WINDOW * blk, WINDOW * (blk + 1))
        variant = 1
        if blk == 0:
            variant = jnp.where(step == 0, 0, variant)
        if blk == WIN_QBLOCKS - 1:
            variant = jnp.where(step == nsteps - 1, 2, variant)
        bias = bias_ref[variant]
        for tile in range(A_KV_TILES):
            ts = slice(LANES * tile, LANES * (tile + 1))
            k_blocks = ([kp_ref[:, ts]] + [kc_ref[WINDOW * i:WINDOW * (i + 1), ts] for i in range(WIN_QBLOCKS)]
                        + [kn_ref[:, ts]])
            vt_blocks = [vp_ref[ts, :]] + [vc_ref[i, ts, :] for i in range(WIN_QBLOCKS)] + [vn_ref[ts, :]]
            k_band = jnp.concatenate(k_blocks[blk:blk + 3], axis=0)
            vt_band = jnp.concatenate(vt_blocks[blk:blk + 3], axis=1)
            q_tiles = [q_ref[rows, LANES * (A_TILE_HEADS * tile + r):LANES * (A_TILE_HEADS * tile + r + 1)]
                       for r in range(A_TILE_HEADS)]
            halves = []
            for half in range(2):
                group = 2 * tile + half
                in_half = (lane < HEAD_DIM) if half == 0 else (lane >= HEAD_DIM)
                q_stack = jnp.concatenate([jnp.where(in_half, qt, zero) for qt in q_tiles], axis=0)
                s = _dot_nt(k_band, q_stack) + bias
                sink = sink_ref[group]
                m = jnp.maximum(jnp.max(s, axis=0, keepdims=True), sink)
                p = jnp.exp2(s - m).astype(BF16)
                v_aug = jnp.concatenate([vt_band[HEAD_DIM * half:HEAD_DIM * (half + 1), :], ones], axis=0)
                o_aug = _dot(v_aug, p)
                den = o_aug[HEAD_DIM:HEAD_DIM + 1, :] + jnp.exp2(sink - m)
                halves.append(o_aug[:HEAD_DIM, :] / den)
            for r in range(A_TILE_HEADS):
                cs = slice(WINDOW * r, WINDOW * (r + 1))
                out_t = jnp.concatenate([halves[0][:, cs], halves[1][:, cs]], axis=0)
                j = A_TILE_HEADS * tile + r
                o_ref[rows, LANES * j:LANES * (j + 1)] = out_t.T.astype(BF16)


def _band_bias(dtype=F32):
    kj = jnp.arange(3 * WINDOW)[:, None]
    qi = (jnp.arange(3 * WINDOW) % WINDOW)[None, :]
    band = (kj - qi >= 0) & (kj - qi <= 2 * WINDOW)
    first = band & (kj >= WINDOW)
    last = band & (kj < 2 * WINDOW)
    return jnp.where(jnp.stack([first, band, last]), 0.0, NEG).astype(dtype)


def _win_attn(q, k, vt, sink):
    b, s, _ = q.shape
    nb = s // WINDOW
    nsteps = nb // WIN_QBLOCKS
    assert nb >= 2 and nb % WIN_QBLOCKS == 0
    qrows = WIN_QBLOCKS * WINDOW
    sink_rows = jnp.repeat(sink.reshape(A_KV_HEADS, 1, A_GROUP) * LOG2E, WINDOW, axis=-1)

    def halo(j, shift):
        return jnp.clip(WIN_QBLOCKS * j + shift, 0, nb - 1)

    def k_halo(shift):
        return pl.BlockSpec((None, WINDOW, A_KV_W), lambda i, j: (i, halo(j, shift), 0))

    def vt_halo(shift):
        return pl.BlockSpec((None, None, A_KV_W, WINDOW), lambda i, j: (i, halo(j, shift), 0, 0))

    return pl.pallas_call(
        functools.partial(_wattn_body, nsteps=nsteps),
        grid=(b, nsteps),
        in_specs=[
            _resident((3, 3 * WINDOW, 3 * WINDOW)),
            _resident((A_KV_HEADS, 1, 3 * WINDOW)),
            pl.BlockSpec((None, qrows, A_Q_W), lambda i, j: (i, j, 0)),
            k_halo(-1),
            pl.BlockSpec((None, qrows, A_KV_W), lambda i, j: (i, j, 0)),
            k_halo(WIN_QBLOCKS),
            vt_halo(-1),
            pl.BlockSpec((None, WIN_QBLOCKS, A_KV_W, WINDOW), lambda i, j: (i, j, 0, 0)),
            vt_halo(WIN_QBLOCKS),
        ],
        out_specs=pl.BlockSpec((None, qrows, A_Q_W), lambda i, j: (i, j, 0)),
        out_shape=jax.ShapeDtypeStruct((b, s, A_Q_W), BF16),
        compiler_params=_params(2),
        name="win_attn",
    )(_band_bias(), sink_rows, q, k, k, k, vt, vt, vt)


def _projb_body(x_ref, g_ref, win_ref, gq_ref, wq_ref, gkv_ref, wk_ref, wvt_ref, tabq_ref, tabk_ref,
                q_ref, k_ref, v_ref, qc_ref, *, qcscale):
    h = _rms(x_ref[...], g_ref[...]).astype(BF16)
    proj = _dot(h, win_ref[...])
    c_q = _rms(proj[:, :Q_LORA], gq_ref[...]).astype(BF16)
    kv0 = Q_LORA
    c_kv = _rms(proj[:, kv0:kv0 + KV_LORA], gkv_ref[...]).astype(BF16)
    qc0 = kv0 + KV_LORA
    qc_ref[...] = (proj[:, qc0:qc0 + MEM_WIDTH] * qcscale).astype(BF16)
    kr0 = qc0 + MEM_WIDTH
    k_rope = _rope_block(proj[:, kr0:kr0 + LANES], tabk_ref, QK_ROPE // 2)
    q_all = _dot(c_q, wq_ref[...])
    k_all = _dot(c_kv, wk_ref[...])
    for hd in range(B_HEADS):
        hs = slice(B_HEAD_PAD * hd, B_HEAD_PAD * (hd + 1))
        q_ref[:, hs] = _rope_block(q_all[:, hs], tabq_ref, QK_ROPE // 2).astype(BF16)
        k_ref[:, hs] = (k_all[:, hs] + k_rope).astype(BF16)
    v_ref[...] = _dot_nt(wvt_ref[...], c_kv).astype(BF16)


def _proj_b(x, g, w_in, gq, wq, gkv, wk, wvt, tabq, tabk):
    b, s, d = x.shape
    tm = min(MLA_KB, s)

    def out(width):
        return (jax.ShapeDtypeStruct((b, s, width), BF16),
                pl.BlockSpec((None, tm, width), lambda i, j: (i, j, 0)))

    vt = (jax.ShapeDtypeStruct((b, s // tm, B_HEADS * V_HEAD, tm), BF16),
          pl.BlockSpec((None, None, B_HEADS * V_HEAD, tm), lambda i, j: (i, j, 0, 0)))
    shapes, specs = zip(out(B_HEADS * B_HEAD_PAD), out(B_HEADS * B_HEAD_PAD), vt, out(MEM_WIDTH))
    tab_spec = pl.BlockSpec((tm, 3 * LANES), lambda i, j: (j, 0))
    return pl.pallas_call(
        functools.partial(_projb_body, qcscale=HEAD_DIM ** -0.5 * LOG2E),
        grid=(b, s // tm),
        in_specs=[
            pl.BlockSpec((None, tm, d), lambda i, j: (i, j, 0)),
            _resident((1, d)),
            _resident(w_in.shape),
            _resident((1, Q_LORA)),
            _resident(wq.shape),
            _resident((1, KV_LORA)),
            _resident(wk.shape),
            _resident(wvt.shape),
            tab_spec, tab_spec,
        ],
        out_specs=list(specs),
        out_shape=list(shapes),
        compiler_params=_params(2),
        name="proj_b",
    )(x, g.reshape(1, d), w_in, gq.reshape(1, -1), wq, gkv.reshape(1, -1), wk, wvt, tabq, tabk)


def _mla_body(q_ref, k_ref, vt_ref, o_ref, *scratch, kb, nkv):
    per_slot = 3 * 2
    slots = [scratch[per_slot * i:per_slot * (i + 1)] for i in range(MLA_SLOTS)]
    s_refs = [sl[0:2] for sl in slots]
    p_refs = [sl[2:4] for sl in slots]
    alpha_refs = [sl[4:6] for sl in slots]
    m_refs, acc_refs = (scratch[per_slot * MLA_SLOTS + 2 * i:per_slot * MLA_SLOTS + 2 * (i + 1)]
                        for i in range(2))
    hslices = [slice(B_HEAD_PAD * hh, B_HEAD_PAD * (hh + 1)) for hh in range(2)]
    vslices = [slice(V_HEAD * hh, V_HEAD * (hh + 1)) for hh in range(2)]
    chunks = [slice(c, c + MLA_CHUNK) for c in range(0, kb, MLA_CHUNK)]

    def scores(t, slot):
        rows = pl.ds(pl.multiple_of(t * kb, kb), kb)
        for hh, hs in enumerate(hslices):
            s_refs[slot][hh][...] = _dot_nt(k_ref[rows, hs], q_ref[:, hs])

    def softmax(slot):
        for hh in range(2):
            s_ref, p_ref = s_refs[slot][hh], p_refs[slot][hh]
            cmax = s_ref[chunks[0], :]
            for ch in chunks[1:]:
                cmax = jnp.maximum(cmax, s_ref[ch, :])
            m = m_refs[hh][...]
            m_new = jnp.maximum(m, jnp.max(cmax, axis=0, keepdims=True))
            alpha = jnp.exp2(m - m_new)
            for ch in chunks:
                p_ref[ch, :] = jnp.exp2(s_ref[ch, :] - m_new).astype(BF16)
            alpha_refs[slot][hh][...] = alpha
            m_refs[hh][...] = m_new

    def accumulate(t, slot):
        ones = jnp.ones((MLA_DEN_ROWS, kb), BF16)
        for hh, vs in enumerate(vslices):
            v_aug = jnp.concatenate([vt_ref[t, vs, :], ones], axis=0)
            acc_refs[hh][...] = (alpha_refs[slot][hh][...] * acc_refs[hh][...]
                                 + _dot(v_aug, p_refs[slot][hh][...]))

    def stage(t, phase, ahead=True, behind=True):
        if ahead:
            scores(t + MLA_AHEAD, (phase + MLA_AHEAD) % MLA_SLOTS)
        if behind:
            accumulate(t - MLA_AHEAD, (phase - MLA_AHEAD) % MLA_SLOTS)
        softmax(phase)

    for hh in range(2):
        m_refs[hh][...] = jnp.full(m_refs[hh].shape, NEG, F32)
        acc_refs[hh][...] = jnp.zeros(acc_refs[hh].shape, F32)
    for t in range(min(MLA_AHEAD, nkv)):
        scores(t, t % MLA_SLOTS)
    lo, hi = MLA_AHEAD, nkv - MLA_AHEAD
    trips = max(hi - lo, 0) // MLA_SLOTS
    for t in range(min(lo, nkv)):
        stage(t, t % MLA_SLOTS, ahead=t + MLA_AHEAD < nkv, behind=False)

    def full_stages(i, carry):
        for j in range(MLA_SLOTS):
            stage(lo + MLA_SLOTS * i + j, (lo + j) % MLA_SLOTS)
        return carry

    lax.fori_loop(0, trips, full_stages, 0)
    for t in range(lo + trips * MLA_SLOTS, nkv):
        stage(t, t % MLA_SLOTS, ahead=t + MLA_AHEAD < nkv, behind=t >= MLA_AHEAD)
    for t in range(max(nkv - MLA_AHEAD, 0), nkv):
        accumulate(t, t % MLA_SLOTS)
    out_t = jnp.concatenate([acc_refs[hh][:V_HEAD, :] / acc_refs[hh][V_HEAD:V_HEAD + 1, :] for hh in range(2)],
                            axis=0)
    o_ref[...] = out_t.T.astype(BF16)


def _mla_attn(q, k, vt):
    b, s, _ = q.shape
    _, nkv, _, kb = vt.shape
    qb = min(MLA_QB, s)
    npair = B_HEADS // 2
    per_slot = ([pltpu.VMEM((kb, qb), F32)] * 2
                + [pltpu.VMEM((kb, qb), BF16)] * 2
                + [pltpu.VMEM((1, qb), F32)] * 2)
    state = ([pltpu.VMEM((1, qb), F32)] * 2
             + [pltpu.VMEM((V_HEAD + MLA_DEN_ROWS, qb), F32)] * 2)
    return pl.pallas_call(
        functools.partial(_mla_body, kb=kb, nkv=nkv),
        grid=(b, npair, s // qb),
        in_specs=[
            pl.BlockSpec((None, qb, 2 * B_HEAD_PAD), lambda i, j, n: (i, n, j)),
            pl.BlockSpec((None, s, 2 * B_HEAD_PAD), lambda i, j, n: (i, 0, j)),
            pl.BlockSpec((None, nkv, 2 * V_HEAD, kb), lambda i, j, n: (i, 0, j, 0)),
        ],
        out_specs=pl.BlockSpec((None, qb, 2 * V_HEAD), lambda i, j, n: (i, n, j)),
        out_shape=jax.ShapeDtypeStruct((b, s, B_HEADS * V_HEAD), BF16),
        scratch_shapes=per_slot * MLA_SLOTS + state,
        compiler_params=_params(3),
        name="mla_attn",
    )(q, k, vt)


def _oproj_body(x_ref, loc_ref, qc_ref, mk_ref, mvt_ref, wo_ref, o_ref):
    tm = qc_ref.shape[0]
    lane = lax.broadcasted_iota(jnp.int32, (tm, LANES), 1)
    ones = jnp.ones((MLA_DEN_ROWS, N_MEM), BF16)
    zero = jnp.zeros((tm, LANES), BF16)
    tiles = []
    for tile in range(MEM_WIDTH // LANES):
        ts = slice(LANES * tile, LANES * (tile + 1))
        halves = []
        for half in range(2):
            hd = 2 * tile + half
            in_half = (lane < HEAD_DIM) if half == 0 else (lane >= HEAD_DIM)
            s = _dot_nt(mk_ref[:, ts], jnp.where(in_half, qc_ref[:, ts], zero))
            p = jnp.exp2(s - jnp.max(s, axis=0, keepdims=True)).astype(BF16)
            v_aug = jnp.concatenate([mvt_ref[HEAD_DIM * hd:HEAD_DIM * (hd + 1), :], ones], axis=0)
            o_aug = _dot(v_aug, p)
            halves.append(o_aug[:HEAD_DIM, :] / o_aug[HEAD_DIM:HEAD_DIM + 1, :])
        tiles.append(jnp.concatenate(halves, axis=0).T.astype(BF16))
    cross = jnp.concatenate(tiles, axis=-1)
    y = _dot(loc_ref[...], wo_ref[:LOCAL_W, :]) + _dot(cross, wo_ref[LOCAL_W:, :])
    o_ref[...] = x_ref[...] + y


def _out_proj(x, local, qc, mk, mvt, wo):
    b, s, d = x.shape
    tm = min(512, s)

    def tok(width):
        return pl.BlockSpec((None, tm, width), lambda i, j: (i, j, 0))

    return pl.pallas_call(
        _oproj_body,
        grid=(b, s // tm),
        in_specs=[tok(d), tok(LOCAL_W), tok(MEM_WIDTH),
                  pl.BlockSpec((None, N_MEM, MEM_WIDTH), lambda i, j: (i, 0, 0)),
                  pl.BlockSpec((None, MEM_WIDTH, N_MEM), lambda i, j: (i, 0, 0)),
                  _resident(wo.shape)],
        out_specs=tok(d),
        out_shape=jax.ShapeDtypeStruct((b, s, d), F32),
        compiler_params=_params(2),
        name="out_proj",
    )(x, local, qc, mk, mvt, wo)


def _prep_ffn(w_gu, w_down):
    depth, d, _ = w_gu.shape
    nch = D_FF // FF_CHUNK
    w = w_gu.astype(BF16).reshape(depth, d, 2, nch, FF_CHUNK)
    wg = jnp.transpose(w[:, :, 0], (0, 2, 1, 3))
    wu = jnp.transpose(w[:, :, 1], (0, 2, 1, 3))
    wd = w_down.astype(BF16).reshape(depth, nch, FF_CHUNK, d)
    return wg, wu, wd


def _prep_a(a_w_in):
    nl, d, _ = a_w_in.shape
    order = jnp.array(A_HEAD_ORDER)
    q = a_w_in[:, :, :A_Q_W].reshape(nl, d, A_Q_HEADS, HEAD_DIM)[:, :, order].reshape(nl, d, A_Q_W)
    k = a_w_in[:, :, A_Q_W:A_Q_W + A_KV_W]
    v = a_w_in[:, :, A_Q_W + A_KV_W:A_Q_W + 2 * A_KV_W]
    qc = a_w_in[:, :, A_Q_W + 2 * A_KV_W:]
    w = jnp.concatenate([q, k, qc], axis=-1).astype(BF16)
    return w, jnp.transpose(v, (0, 2, 1)).astype(BF16)


def _prep_w_o(w_o):
    depth, _, d = w_o.shape
    order = jnp.array(A_HEAD_ORDER)
    local = w_o[:, :LOCAL_W].reshape(depth, A_Q_HEADS, HEAD_DIM, d)
    local = jnp.where((jnp.arange(depth) % 2 == 0)[:, None, None, None], local[:, order], local)
    return jnp.concatenate([local.reshape(depth, LOCAL_W, d), w_o[:, LOCAL_W:]], axis=1).astype(BF16)


def _prep_b(b_w_in, b_w_q_up, b_w_kv_up):
    nl, d, _ = b_w_in.shape
    c_q = b_w_in[:, :, :Q_LORA]
    c_kv = b_w_in[:, :, Q_LORA:Q_LORA + KV_LORA]
    k_r = b_w_in[:, :, Q_LORA + KV_LORA:Q_LORA + KV_LORA + QK_ROPE]
    qc = b_w_in[:, :, Q_LORA + KV_LORA + QK_ROPE:]
    kr_tile = jnp.pad(k_r, ((0, 0), (0, 0), (QK_NOPE, LANES - QK_NOPE - QK_ROPE)))
    w_in = jnp.concatenate([c_q, c_kv, qc, kr_tile], axis=-1).astype(BF16)
    wq = b_w_q_up.reshape(nl, Q_LORA, B_HEADS, B_QK)
    wq = jnp.pad(wq, ((0, 0), (0, 0), (0, 0), (0, B_HEAD_PAD - B_QK)))
    wq = wq.reshape(nl, Q_LORA, B_HEADS * B_HEAD_PAD).astype(BF16)
    wkv = b_w_kv_up.reshape(nl, KV_LORA, B_HEADS, QK_NOPE + V_HEAD)
    wk = jnp.pad(wkv[..., :QK_NOPE], ((0, 0), (0, 0), (0, 0), (0, B_HEAD_PAD - QK_NOPE)))
    wk = wk.reshape(nl, KV_LORA, B_HEADS * B_HEAD_PAD).astype(BF16)
    wvt = jnp.transpose(wkv[..., QK_NOPE:].reshape(nl, KV_LORA, B_HEADS * V_HEAD), (0, 2, 1)).astype(BF16)
    return w_in, wq, wk, wvt


def _trunk(x, mem, w):
    b, s, d = x.shape
    depth = w["mix_norm"].shape[0]
    tab_a = _rope_table(s, HEAD_DIM, 0, HEAD_DIM, 1.0, False)
    tab_bq = _rope_table(s, QK_ROPE, QK_NOPE, LANES, B_QK ** -0.5 * LOG2E, True)
    tab_bk = _rope_table(s, QK_ROPE, QK_NOPE, LANES, 1.0, False)
    for i in range(depth):
        x = _ffn(x.reshape(b * s, d), w["ffn1_norm"][i], w["ffn1_wg"][i], w["ffn1_wu"][i],
                 w["ffn1_wd"][i]).reshape(b, s, d)
        mk, mvt = _mem_kv(mem, w["mem_norm"][i], w["w_mem_k"][i], w["w_mem_vt"][i])
        j = i // 2
        if i % 2 == 0:
            q, k, vt, qc = _proj_a(x, w["mix_norm"][i], w["a_w_in"][j], w["a_wvt"][j], tab_a)
            local = _win_attn(q, k, vt, w["a_sink"][j])
        else:
            q, k, v, qc = _proj_b(x, w["mix_norm"][i], w["b_w_in"][j], w["b_q_norm"][j], w["b_wq"][j],
                                  w["b_kv_norm"][j], w["b_wk"][j], w["b_wvt"][j], tab_bq, tab_bk)
            local = _mla_attn(q, k, v)
        x = _out_proj(x, local, qc, mk, mvt, w["w_o"][i])
        g_final = w["final_norm"] if i == depth - 1 else None
        x = _ffn(x.reshape(b * s, d), w["ffn2_norm"][i], w["ffn2_wg"][i], w["ffn2_wu"][i],
                 w["ffn2_wd"][i], g_final).reshape(b, s, d)
    return x


def kernel(x_prompt, x_sample, mem_prompt, mem_sample, ffn1_norm, ffn1_w_gu, ffn1_w_down, mix_norm,
           mem_norm, w_mem_kv, a_w_in, a_sink, b_w_in, b_q_norm, b_w_q_up, b_kv_norm, b_w_kv_up,
           w_o, ffn2_norm, ffn2_w_gu, ffn2_w_down, final_norm):
    w = {
        "ffn1_norm": ffn1_norm, "ffn2_norm": ffn2_norm, "mix_norm": mix_norm, "mem_norm": mem_norm,
        "w_mem_k": w_mem_kv[:, :, :MEM_WIDTH].astype(BF16),
        "w_mem_vt": jnp.transpose(w_mem_kv[:, :, MEM_WIDTH:], (0, 2, 1)).astype(BF16), "a_sink": a_sink,
        "b_q_norm": b_q_norm, "b_kv_norm": b_kv_norm, "w_o": _prep_w_o(w_o),
        "final_norm": final_norm,
    }
    w["ffn1_wg"], w["ffn1_wu"], w["ffn1_wd"] = _prep_ffn(ffn1_w_gu, ffn1_w_down)
    w["ffn2_wg"], w["ffn2_wu"], w["ffn2_wd"] = _prep_ffn(ffn2_w_gu, ffn2_w_down)
    w["a_w_in"], w["a_wvt"] = _prep_a(a_w_in)
    w["b_w_in"], w["b_wq"], w["b_wk"], w["b_wvt"] = _prep_b(b_w_in, b_w_q_up, b_w_kv_up)
    return (_trunk(x_prompt, mem_prompt, w), _trunk(x_sample, mem_sample, w))
```

```python
import functools
import math

import jax
import jax.numpy as jnp
from jax import lax
from jax.experimental import pallas as pl
from jax.experimental.pallas import tpu as pltpu

D_MODEL = 1024
HEAD_DIM = 64
ROPE_THETA = 10000.0
NORM_EPS = 1e-6
D_FF = 2816
N_MEM = 256
MEM_HEADS = 4
MEM_WIDTH = MEM_HEADS * HEAD_DIM
A_Q_HEADS = 12
A_KV_HEADS = 4
A_GROUP = A_Q_HEADS // A_KV_HEADS
WINDOW = 128
A_Q_W = A_Q_HEADS * HEAD_DIM
A_KV_W = A_KV_HEADS * HEAD_DIM
B_HEADS = 12
Q_LORA = 384
KV_LORA = 256
QK_NOPE = 64
QK_ROPE = 32
V_HEAD = 64
B_QK = QK_NOPE + QK_ROPE
LOCAL_W = A_Q_W
NEG = -1e30
LOG2E = math.log2(math.e)

LANES = 128
B_HEAD_PAD = LANES
FF_CHUNK = 256
MLA_QB = 512
MLA_KB = 512
MLA_CHUNK = 32
MLA_DEN_ROWS = 16
MLA_AHEAD = 2
MLA_SLOTS = 2 * MLA_AHEAD
VMEM_LIMIT = 56 * 1024 * 1024

F32 = jnp.float32
BF16 = jnp.bfloat16


def _params(n_axes):
    return pltpu.CompilerParams(dimension_semantics=("arbitrary",) * n_axes,
                                vmem_limit_bytes=VMEM_LIMIT)


def _rms(x, g):
    return x * lax.rsqrt(jnp.mean(x * x, axis=-1, keepdims=True) + NORM_EPS) * g


def _dot(a, b):
    return jnp.dot(a, b, preferred_element_type=F32)


def _dot_nt(a, b):
    return lax.dot_general(a, b, (((1,), (1,)), ((), ())), preferred_element_type=F32)


def _resident(shape):
    zeros = (0,) * len(shape)
    return pl.BlockSpec(shape, lambda *_: zeros, pipeline_mode=pl.Buffered(1))


def _ffn_body(*refs, final):
    if final:
        x_ref, g_ref, wgu_ref, wd_ref, gf_ref, o_ref, h_ref = refs
    else:
        x_ref, g_ref, wgu_ref, wd_ref, o_ref, h_ref = refs
    h_ref[...] = _rms(x_ref[...], g_ref[...]).astype(BF16)
    acc = None
    for c in range(D_FF // FF_CHUNK):
        cols = slice(FF_CHUNK * c, FF_CHUNK * (c + 1))
        up_cols = slice(D_FF + FF_CHUNK * c, D_FF + FF_CHUNK * (c + 1))
        h = h_ref[...]
        gate = _dot(h, wgu_ref[:, cols])
        up = _dot(h, wgu_ref[:, up_cols])
        a = (gate / (1.0 + jnp.exp(-gate)) * up).astype(BF16)
        down = _dot(a, wd_ref[cols, :])
        acc = down if acc is None else acc + down
    y = x_ref[...] + 0.5 * acc
    if final:
        y = _rms(y, gf_ref[...])
    o_ref[...] = y


def _layer_resident(stacked, layer):
    zeros = (0,) * (stacked.ndim - 1)
    return pl.BlockSpec((None,) + stacked.shape[1:], lambda *_: (layer,) + zeros,
                        pipeline_mode=pl.Buffered(1))


def _ffn(x, g, w_gu, w_down, layer, g_final=None):
    t, d = x.shape
    tm = min(512, t)
    final = g_final is not None
    in_specs = [
        pl.BlockSpec((tm, d), lambda i: (i, 0)),
        _resident((1, d)),
        _layer_resident(w_gu, layer),
        _layer_resident(w_down, layer),
    ]
    args = [x, g.reshape(1, d), w_gu, w_down]
    if final:
        in_specs.append(_resident((1, d)))
        args.append(g_final.reshape(1, d))
    return pl.pallas_call(
        functools.partial(_ffn_body, final=final),
        grid=(t // tm,),
        in_specs=in_specs,
        out_specs=pl.BlockSpec((tm, d), lambda i: (i, 0)),
        out_shape=jax.ShapeDtypeStruct((t, d), F32),
        scratch_shapes=[pltpu.VMEM((tm, d), BF16)],
        compiler_params=_params(1),
        name="ffn_final" if final else "ffn",
    )(*args)


def _memkv_body(mem_ref, g_ref, wk_ref, wvt_ref, mk_ref, mvt_ref):
    h = _rms(mem_ref[...], g_ref[...]).astype(BF16)
    mk_ref[...] = _dot(h, wk_ref[...]).astype(BF16)
    mvt_ref[...] = _dot_nt(wvt_ref[...], h).astype(BF16)


def _mem_kv(mem, g, wk, wvt):
    b, n, d = mem.shape
    return pl.pallas_call(
        _memkv_body,
        grid=(b,),
        in_specs=[pl.BlockSpec((None, n, d), lambda i: (i, 0, 0)), _resident((1, d)),
                  _resident(wk.shape), _resident(wvt.shape)],
        out_specs=[pl.BlockSpec((None, n, MEM_WIDTH), lambda i: (i, 0, 0)),
                   pl.BlockSpec((None, MEM_WIDTH, n), lambda i: (i, 0, 0))],
        out_shape=[jax.ShapeDtypeStruct((b, n, MEM_WIDTH), BF16),
                   jax.ShapeDtypeStruct((b, MEM_WIDTH, n), BF16)],
        compiler_params=_params(1),
        name="mem_kv",
    )(mem, g.reshape(1, d), wk, wvt)


def _rope_block(xb, tab_ref, shift):
    c = tab_ref[:, 0:LANES]
    s_plus = tab_ref[:, LANES:2 * LANES]
    s_minus = tab_ref[:, 2 * LANES:3 * LANES]
    return (xb * c + pltpu.roll(xb, shift, 1) * s_plus
            + pltpu.roll(xb, LANES - shift, 1) * s_minus)


def _rope_table(seq, dim, lane_start, period, scale, pass_through):
    half = dim // 2
    inv = 1.0 / (ROPE_THETA ** (jnp.arange(0, dim, 2, dtype=F32) / dim))
    ang = jnp.arange(seq, dtype=F32)[:, None] * inv[None, :]
    cos, sin = jnp.cos(ang), jnp.sin(ang)
    zeros_h = jnp.zeros((seq, half), F32)
    lead = jnp.full((seq, lane_start), 1.0 if pass_through else 0.0, F32)
    lead0 = jnp.zeros((seq, lane_start), F32)
    tail0 = jnp.zeros((seq, period - lane_start - dim), F32)
    reps = LANES // period
    c = jnp.tile(jnp.concatenate([lead, cos, cos, tail0], -1), (1, reps))
    s_plus = jnp.tile(jnp.concatenate([lead0, zeros_h, sin, tail0], -1), (1, reps))
    s_minus = jnp.tile(jnp.concatenate([lead0, -sin, zeros_h, tail0], -1), (1, reps))
    return jnp.concatenate([c, s_plus, s_minus], -1) * scale


def _rope_rot_block(xb, tab_ref):
    c = tab_ref[:, 0:LANES]
    s = tab_ref[:, LANES:2 * LANES]
    return xb * c + pltpu.roll(xb, LANES - QK_ROPE, 1) * s


def _rope_rot_table(seq, scale, pass_through):
    inv = 1.0 / (ROPE_THETA ** (jnp.arange(0, QK_ROPE, 2, dtype=F32) / QK_ROPE))
    ang = jnp.arange(seq, dtype=F32)[:, None] * inv[None, :]
    cos, sin = jnp.cos(ang), jnp.sin(ang)
    lead = jnp.full((seq, QK_NOPE), 1.0 if pass_through else 0.0, F32)
    lead0 = jnp.zeros((seq, QK_NOPE), F32)
    tail0 = jnp.zeros((seq, LANES - QK_NOPE - QK_ROPE), F32)
    c = jnp.concatenate([lead, cos, cos, tail0], -1)
    s = jnp.concatenate([lead0, sin, sin, tail0], -1)
    return jnp.concatenate([c, s], -1) * scale


def _with_rotate_half(w_rope):
    x1, x2 = w_rope[..., :QK_ROPE // 2], w_rope[..., QK_ROPE // 2:]
    return jnp.concatenate([w_rope, -x2, x1], axis=-1)


A_HEAD_ORDER = (0, 3, 1, 4, 2, 5, 6, 9, 7, 10, 8, 11)
A_KV_TILES = A_KV_W // LANES
A_TILE_HEADS = A_GROUP
WIN_QBLOCKS = 2


def _proja_body(x_ref, g_ref, w_ref, wvt_ref, tab_ref, q_ref, k_ref, vt_ref, qc_ref, *, qscale):
    h = _rms(x_ref[...], g_ref[...]).astype(BF16)
    proj = _dot(h, w_ref[...])
    nq = A_Q_W // LANES
    nk = A_KV_W // LANES
    for j in range(nq + nk):
        rb = _rope_block(proj[:, LANES * j:LANES * (j + 1)], tab_ref, HEAD_DIM // 2)
        if j < nq:
            q_ref[:, LANES * j:LANES * (j + 1)] = (rb * qscale).astype(BF16)
        else:
            k_ref[:, LANES * (j - nq):LANES * (j - nq + 1)] = rb.astype(BF16)
    qc_ref[...] = (proj[:, A_Q_W + A_KV_W:] * qscale).astype(BF16)
    vt = _dot_nt(wvt_ref[...], h).astype(BF16)
    for i in range(vt_ref.shape[0]):
        vt_ref[i] = vt[:, WINDOW * i:WINDOW * (i + 1)]


def _proj_a(x, g, w, wvt, tab):
    b, s, d = x.shape
    tm = min(512, s)
    nblk = tm // WINDOW

    def out(width):
        return (jax.ShapeDtypeStruct((b, s, width), BF16),
                pl.BlockSpec((None, tm, width), lambda i, j: (i, j, 0)))

    vt = (jax.ShapeDtypeStruct((b, s // WINDOW, A_KV_W, WINDOW), BF16),
          pl.BlockSpec((None, nblk, A_KV_W, WINDOW), lambda i, j: (i, j, 0, 0)))
    shapes, specs = zip(out(A_Q_W), out(A_KV_W), vt, out(MEM_WIDTH))
    return pl.pallas_call(
        functools.partial(_proja_body, qscale=HEAD_DIM ** -0.5 * LOG2E),
        grid=(b, s // tm),
        in_specs=[
            pl.BlockSpec((None, tm, d), lambda i, j: (i, j, 0)),
            _resident((1, d)),
            _resident(w.shape),
            _resident(wvt.shape),
            pl.BlockSpec((tm, 3 * LANES), lambda i, j: (j, 0)),
        ],
        out_specs=list(specs),
        out_shape=list(shapes),
        compiler_params=_params(2),
        name="proj_a",
    )(x, g.reshape(1, d), w, wvt, tab)


def _wattn_body(bias_ref, sink_ref, q_ref, kp_ref, kc_ref, kn_ref, vp_ref, vc_ref, vn_ref, o_ref, *, nsteps):
    step = pl.program_id(1)
    lane = lax.broadcasted_iota(jnp.int32, (WINDOW, LANES), 1)
    ones = jnp.ones((MLA_DEN_ROWS, 3 * WINDOW), BF16)
    zero = jnp.zeros((WINDOW, LANES), BF16)
    for blk in range(WIN_QBLOCKS):
        rows = slice(WINDOW * blk, WINDOW * (blk + 1))
        variant = 1
        if blk == 0:
            variant = jnp.where(step == 0, 0, variant)
        if blk == WIN_QBLOCKS - 1:
            variant = jnp.where(step == nsteps - 1, 2, variant)
        bias = bias_ref[variant]
        for tile in range(A_KV_TILES):
            ts = slice(LANES * tile, LANES * (tile + 1))
            k_blocks = ([kp_ref[:, ts]] + [kc_ref[WINDOW * i:WINDOW * (i + 1), ts] for i in range(WIN_QBLOCKS)]
                        + [kn_ref[:, ts]])
            vt_blocks = [vp_ref[ts, :]] + [vc_ref[i, ts, :] for i in range(WIN_QBLOCKS)] + [vn_ref[ts, :]]
            k_band = jnp.concatenate(k_blocks[blk:blk + 3], axis=0)
            vt_band = jnp.concatenate(vt_blocks[blk:blk + 3], axis=1)
            q_tiles = [q_ref[rows, LANES * (A_TILE_HEADS * tile + r):LANES * (A_TILE_HEADS * tile + r + 1)]
                       for r in range(A_TILE_HEADS)]
            halves = []
            for half in range(2):
                group = 2 * tile + half
                in_half = (lane < HEAD_DIM) if half == 0 else (lane >= HEAD_DIM)
                q_stack = jnp.concatenate([jnp.where(in_half, qt, zero) for qt in q_tiles], axis=0)
                s = _dot_nt(k_band, q_stack) + bias
                sink = sink_ref[group]
                m = jnp.maximum(jnp.max(s, axis=0, keepdims=True), sink)
                p = jnp.exp2(s - m).astype(BF16)
                v_aug = jnp.concatenate([vt_band[HEAD_DIM * half:HEAD_DIM * (half + 1), :], ones], axis=0)
                o_aug = _dot(v_aug, p)
                den = o_aug[HEAD_DIM:HEAD_DIM + 1, :] + jnp.exp2(sink - m)
                halves.append(o_aug[:HEAD_DIM, :] / den)
            for r in range(A_TILE_HEADS):
                cs = slice(WINDOW * r, WINDOW * (r + 1))
                out_t = jnp.concatenate([halves[0][:, cs], halves[1][:, cs]], axis=0)
                j = A_TILE_HEADS * tile + r
                o_ref[rows, LANES * j:LANES * (j + 1)] = out_t.T.astype(BF16)


def _band_bias(dtype=F32):
    kj = jnp.arange(3 * WINDOW)[:, None]
    qi = (jnp.arange(3 * WINDOW) % WINDOW)[None, :]
    band = (kj - qi >= 0) & (kj - qi <= 2 * WINDOW)
    first = band & (kj >= WINDOW)
    last = band & (kj < 2 * WINDOW)
    return jnp.where(jnp.stack([first, band, last]), 0.0, NEG).astype(dtype)


def _win_attn(q, k, vt, sink):
    b, s, _ = q.shape
    nb = s // WINDOW
    nsteps = nb // WIN_QBLOCKS
    assert nb >= 2 and nb % WIN_QBLOCKS == 0
    qrows = WIN_QBLOCKS * WINDOW
    sink_rows = jnp.repeat(sink.reshape(A_KV_HEADS, 1, A_GROUP) * LOG2E, WINDOW, axis=-1)

    def halo(j, shift):
        return jnp.clip(WIN_QBLOCKS * j + shift, 0, nb - 1)

    def k_halo(shift):
        return pl.BlockSpec((None, WINDOW, A_KV_W), lambda i, j: (i, halo(j, shift), 0))

    def vt_halo(shift):
        return pl.BlockSpec((None, None, A_KV_W, WINDOW), lambda i, j: (i, halo(j, shift), 0, 0))

    return pl.pallas_call(
        functools.partial(_wattn_body, nsteps=nsteps),
        grid=(b, nsteps),
        in_specs=[
            _resident((3, 3 * WINDOW, 3 * WINDOW)),
            _resident((A_KV_HEADS, 1, 3 * WINDOW)),
            pl.BlockSpec((None, qrows, A_Q_W), lambda i, j: (i, j, 0)),
            k_halo(-1),
            pl.BlockSpec((None, qrows, A_KV_W), lambda i, j: (i, j, 0)),
            k_halo(WIN_QBLOCKS),
            vt_halo(-1),
            pl.BlockSpec((None, WIN_QBLOCKS, A_KV_W, WINDOW), lambda i, j: (i, j, 0, 0)),
            vt_halo(WIN_QBLOCKS),
        ],
        out_specs=pl.BlockSpec((None, qrows, A_Q_W), lambda i, j: (i, j, 0)),
        out_shape=jax.ShapeDtypeStruct((b, s, A_Q_W), BF16),
        compiler_params=_params(2),
        name="win_attn",
    )(_band_bias(), sink_rows, q, k, k, k, vt, vt, vt)


def _projb_body(x_ref, g_ref, win_ref, gq_ref, wq_ref, gkv_ref, wk_ref, wvt_ref, tabq_ref, tabk_ref,
                q_ref, k_ref, v_ref, qc_ref, *, qcscale):
    h = _rms(x_ref[...], g_ref[...]).astype(BF16)
    proj = _dot(h, win_ref[...])
    c_q = _rms(proj[:, :Q_LORA], gq_ref[...]).astype(BF16)
    kv0 = Q_LORA
    c_kv = _rms(proj[:, kv0:kv0 + KV_LORA], gkv_ref[...]).astype(BF16)
    qc0 = kv0 + KV_LORA
    qc_ref[...] = (proj[:, qc0:qc0 + MEM_WIDTH] * qcscale).astype(BF16)
    kr0 = qc0 + MEM_WIDTH
    k_rope = _rope_rot_block(proj[:, kr0:kr0 + LANES], tabk_ref)
    q_all = _dot(c_q, wq_ref[...])
    k_all = _dot(c_kv, wk_ref[...])
    for hd in range(B_HEADS):
        hs = slice(B_HEAD_PAD * hd, B_HEAD_PAD * (hd + 1))
        q_ref[:, hs] = _rope_rot_block(q_all[:, hs], tabq_ref).astype(BF16)
        k_ref[:, hs] = (k_all[:, hs] + k_rope).astype(BF16)
    v_ref[...] = _dot_nt(wvt_ref[...], c_kv).astype(BF16)


def _proj_b(x, g, w_in, gq, wq, gkv, wk, wvt, tabq, tabk):
    b, s, d = x.shape
    tm = min(MLA_KB, s)

    def out(width):
        return (jax.ShapeDtypeStruct((b, s, width), BF16),
                pl.BlockSpec((None, tm, width), lambda i, j: (i, j, 0)))

    vt = (jax.ShapeDtypeStruct((b, s // tm, B_HEADS * V_HEAD, tm), BF16),
          pl.BlockSpec((None, None, B_HEADS * V_HEAD, tm), lambda i, j: (i, j, 0, 0)))
    shapes, specs = zip(out(B_HEADS * B_HEAD_PAD), out(B_HEADS * B_HEAD_PAD), vt, out(MEM_WIDTH))
    tab_spec = pl.BlockSpec((tm, 2 * LANES), lambda i, j: (j, 0))
    return pl.pallas_call(
        functools.partial(_projb_body, qcscale=HEAD_DIM ** -0.5 * LOG2E),
        grid=(b, s // tm),
        in_specs=[
            pl.BlockSpec((None, tm, d), lambda i, j: (i, j, 0)),
            _resident((1, d)),
            _resident(w_in.shape),
            _resident((1, Q_LORA)),
            _resident(wq.shape),
            _resident((1, KV_LORA)),
            _resident(wk.shape),
            _resident(wvt.shape),
            tab_spec, tab_spec,
        ],
        out_specs=list(specs),
        out_shape=list(shapes),
        compiler_params=_params(2),
        name="proj_b",
    )(x, g.reshape(1, d), w_in, gq.reshape(1, -1), wq, gkv.reshape(1, -1), wk, wvt, tabq, tabk)


def _mla_body(q_ref, k_ref, vt_ref, o_ref, *scratch, kb, nkv):
    per_slot = 3 * 2
    slots = [scratch[per_slot * i:per_slot * (i + 1)] for i in range(MLA_SLOTS)]
    s_refs = [sl[0:2] for sl in slots]
    p_refs = [sl[2:4] for sl in slots]
    alpha_refs = [sl[4:6] for sl in slots]
    m_refs, acc_refs = (scratch[per_slot * MLA_SLOTS + 2 * i:per_slot * MLA_SLOTS + 2 * (i + 1)]
                        for i in range(2))
    hslices = [slice(B_HEAD_PAD * hh, B_HEAD_PAD * (hh + 1)) for hh in range(2)]
    vslices = [slice(V_HEAD * hh, V_HEAD * (hh + 1)) for hh in range(2)]
    chunks = [slice(c, c + MLA_CHUNK) for c in range(0, kb, MLA_CHUNK)]

    def scores(t, slot):
        rows = pl.ds(pl.multiple_of(t * kb, kb), kb)
        for hh, hs in enumerate(hslices):
            s_refs[slot][hh][...] = _dot_nt(k_ref[rows, hs], q_ref[:, hs])

    def softmax(slot):
        for hh in range(2):
            s_ref, p_ref = s_refs[slot][hh], p_refs[slot][hh]
            cmax = s_ref[chunks[0], :]
            for ch in chunks[1:]:
                cmax = jnp.maximum(cmax, s_ref[ch, :])
            m = m_refs[hh][...]
            m_new = jnp.maximum(m, jnp.max(cmax, axis=0, keepdims=True))
            alpha = jnp.exp2(m - m_new)
            for ch in chunks:
                p_ref[ch, :] = jnp.exp2(s_ref[ch, :] - m_new).astype(BF16)
            alpha_refs[slot][hh][...] = alpha
            m_refs[hh][...] = m_new

    def accumulate(t, slot):
        ones = jnp.ones((MLA_DEN_ROWS, kb), BF16)
        for hh, vs in enumerate(vslices):
            v_aug = jnp.concatenate([vt_ref[t, vs, :], ones], axis=0)
            acc_refs[hh][...] = (alpha_refs[slot][hh][...] * acc_refs[hh][...]
                                 + _dot(v_aug, p_refs[slot][hh][...]))

    def stage(t, phase, ahead=True, behind=True):
        if ahead:
            scores(t + MLA_AHEAD, (phase + MLA_AHEAD) % MLA_SLOTS)
        if behind:
            accumulate(t - MLA_AHEAD, (phase - MLA_AHEAD) % MLA_SLOTS)
        softmax(phase)

    for hh in range(2):
        m_refs[hh][...] = jnp.full(m_refs[hh].shape, NEG, F32)
        acc_refs[hh][...] = jnp.zeros(acc_refs[hh].shape, F32)
    for t in range(min(MLA_AHEAD, nkv)):
        scores(t, t % MLA_SLOTS)
    lo, hi = MLA_AHEAD, nkv - MLA_AHEAD
    trips = max(hi - lo, 0) // MLA_SLOTS
    for t in range(min(lo, nkv)):
        stage(t, t % MLA_SLOTS, ahead=t + MLA_AHEAD < nkv, behind=False)

    def full_stages(i, carry):
        for j in range(MLA_SLOTS):
            stage(lo + MLA_SLOTS * i + j, (lo + j) % MLA_SLOTS)
        return carry

    lax.fori_loop(0, trips, full_stages, 0)
    for t in range(lo + trips * MLA_SLOTS, nkv):
        stage(t, t % MLA_SLOTS, ahead=t + MLA_AHEAD < nkv, behind=t >= MLA_AHEAD)
    for t in range(max(nkv - MLA_AHEAD, 0), nkv):
        accumulate(t, t % MLA_SLOTS)
    out_t = jnp.concatenate([acc_refs[hh][:V_HEAD, :] / acc_refs[hh][V_HEAD:V_HEAD + 1, :] for hh in range(2)],
                            axis=0)
    o_ref[...] = out_t.T.astype(BF16)


def _mla_attn(q, k, vt):
    b, s, _ = q.shape
    _, nkv, _, kb = vt.shape
    qb = min(MLA_QB, s)
    npair = B_HEADS // 2
    per_slot = ([pltpu.VMEM((kb, qb), F32)] * 2
                + [pltpu.VMEM((kb, qb), BF16)] * 2
                + [pltpu.VMEM((1, qb), F32)] * 2)
    state = ([pltpu.VMEM((1, qb), F32)] * 2
             + [pltpu.VMEM((V_HEAD + MLA_DEN_ROWS, qb), F32)] * 2)
    return pl.pallas_call(
        functools.partial(_mla_body, kb=kb, nkv=nkv),
        grid=(b, npair, s // qb),
        in_specs=[
            pl.BlockSpec((None, qb, 2 * B_HEAD_PAD), lambda i, j, n: (i, n, j)),
            pl.BlockSpec((None, s, 2 * B_HEAD_PAD), lambda i, j, n: (i, 0, j)),
            pl.BlockSpec((None, nkv, 2 * V_HEAD, kb), lambda i, j, n: (i, 0, j, 0)),
        ],
        out_specs=pl.BlockSpec((None, qb, 2 * V_HEAD), lambda i, j, n: (i, n, j)),
        out_shape=jax.ShapeDtypeStruct((b, s, B_HEADS * V_HEAD), BF16),
        scratch_shapes=per_slot * MLA_SLOTS + state,
        compiler_params=_params(3),
        name="mla_attn",
    )(q, k, vt)


def _oproj_body(x_ref, loc_ref, qc_ref, mk_ref, mvt_ref, wo_ref, o_ref):
    tm = qc_ref.shape[0]
    lane = lax.broadcasted_iota(jnp.int32, (tm, LANES), 1)
    ones = jnp.ones((MLA_DEN_ROWS, N_MEM), BF16)
    zero = jnp.zeros((tm, LANES), BF16)
    tiles = []
    for tile in range(MEM_WIDTH // LANES):
        ts = slice(LANES * tile, LANES * (tile + 1))
        halves = []
        for half in range(2):
            hd = 2 * tile + half
            in_half = (lane < HEAD_DIM) if half == 0 else (lane >= HEAD_DIM)
            s = _dot_nt(mk_ref[:, ts], jnp.where(in_half, qc_ref[:, ts], zero))
            p = jnp.exp2(s - jnp.max(s, axis=0, keepdims=True)).astype(BF16)
            v_aug = jnp.concatenate([mvt_ref[HEAD_DIM * hd:HEAD_DIM * (hd + 1), :], ones], axis=0)
            o_aug = _dot(v_aug, p)
            halves.append(o_aug[:HEAD_DIM, :] / o_aug[HEAD_DIM:HEAD_DIM + 1, :])
        tiles.append(jnp.concatenate(halves, axis=0).T.astype(BF16))
    cross = jnp.concatenate(tiles, axis=-1)
    y = _dot(loc_ref[...], wo_ref[:LOCAL_W, :]) + _dot(cross, wo_ref[LOCAL_W:, :])
    o_ref[...] = x_ref[...] + y


def _out_proj(x, local, qc, mk, mvt, wo):
    b, s, d = x.shape
    tm = min(512, s)

    def tok(width):
        return pl.BlockSpec((None, tm, width), lambda i, j: (i, j, 0))

    return pl.pallas_call(
        _oproj_body,
        grid=(b, s // tm),
        in_specs=[tok(d), tok(LOCAL_W), tok(MEM_WIDTH),
                  pl.BlockSpec((None, N_MEM, MEM_WIDTH), lambda i, j: (i, 0, 0)),
                  pl.BlockSpec((None, MEM_WIDTH, N_MEM), lambda i, j: (i, 0, 0)),
                  _resident(wo.shape)],
        out_specs=tok(d),
        out_shape=jax.ShapeDtypeStruct((b, s, d), F32),
        compiler_params=_params(2),
        name="out_proj",
    )(x, local, qc, mk, mvt, wo)


def _prep_a(a_w_in):
    nl, d, _ = a_w_in.shape
    order = jnp.array(A_HEAD_ORDER)
    q = a_w_in[:, :, :A_Q_W].reshape(nl, d, A_Q_HEADS, HEAD_DIM)[:, :, order].reshape(nl, d, A_Q_W)
    k = a_w_in[:, :, A_Q_W:A_Q_W + A_KV_W]
    v = a_w_in[:, :, A_Q_W + A_KV_W:A_Q_W + 2 * A_KV_W]
    qc = a_w_in[:, :, A_Q_W + 2 * A_KV_W:]
    w = jnp.concatenate([q, k, qc], axis=-1).astype(BF16)
    return w, jnp.transpose(v, (0, 2, 1)).astype(BF16)


def _prep_w_o(w_o):
    depth, _, d = w_o.shape
    order = jnp.array(A_HEAD_ORDER)
    local = w_o[:, :LOCAL_W].reshape(depth, A_Q_HEADS, HEAD_DIM, d)
    local = jnp.where((jnp.arange(depth) % 2 == 0)[:, None, None, None], local[:, order], local)
    return jnp.concatenate([local.reshape(depth, LOCAL_W, d), w_o[:, LOCAL_W:]], axis=1).astype(BF16)


def _prep_b(b_w_in, b_w_q_up, b_w_kv_up):
    nl, d, _ = b_w_in.shape
    c_q = b_w_in[:, :, :Q_LORA]
    c_kv = b_w_in[:, :, Q_LORA:Q_LORA + KV_LORA]
    k_r = b_w_in[:, :, Q_LORA + KV_LORA:Q_LORA + KV_LORA + QK_ROPE]
    qc = b_w_in[:, :, Q_LORA + KV_LORA + QK_ROPE:]
    kr_tile = jnp.pad(_with_rotate_half(k_r), ((0, 0), (0, 0), (QK_NOPE, 0)))
    w_in = jnp.concatenate([c_q, c_kv, qc, kr_tile], axis=-1).astype(BF16)
    wq = b_w_q_up.reshape(nl, Q_LORA, B_HEADS, B_QK)
    wq = jnp.concatenate([wq[..., :QK_NOPE], _with_rotate_half(wq[..., QK_NOPE:])], axis=-1)
    wq = wq.reshape(nl, Q_LORA, B_HEADS * B_HEAD_PAD).astype(BF16)
    wkv = b_w_kv_up.reshape(nl, KV_LORA, B_HEADS, QK_NOPE + V_HEAD)
    wk = jnp.pad(wkv[..., :QK_NOPE], ((0, 0), (0, 0), (0, 0), (0, B_HEAD_PAD - QK_NOPE)))
    wk = wk.reshape(nl, KV_LORA, B_HEADS * B_HEAD_PAD).astype(BF16)
    wvt = jnp.transpose(wkv[..., QK_NOPE:].reshape(nl, KV_LORA, B_HEADS * V_HEAD), (0, 2, 1)).astype(BF16)
    return w_in, wq, wk, wvt


def _trunk(x, mem, w):
    b, s, d = x.shape
    depth = w["mix_norm"].shape[0]
    tab_a = _rope_table(s, HEAD_DIM, 0, HEAD_DIM, 1.0, False)
    tab_bq = _rope_rot_table(s, B_QK ** -0.5 * LOG2E, True)
    tab_bk = _rope_rot_table(s, 1.0, False)
    for i in range(depth):
        x = _ffn(x.reshape(b * s, d), w["ffn1_norm"][i], w["ffn1_w_gu"], w["ffn1_w_down"], i).reshape(b, s, d)
        mk, mvt = _mem_kv(mem, w["mem_norm"][i], w["w_mem_k"][i], w["w_mem_vt"][i])
        j = i // 2
        if i % 2 == 0:
            q, k, vt, qc = _proj_a(x, w["mix_norm"][i], w["a_w_in"][j], w["a_wvt"][j], tab_a)
            local = _win_attn(q, k, vt, w["a_sink"][j])
        else:
            q, k, v, qc = _proj_b(x, w["mix_norm"][i], w["b_w_in"][j], w["b_q_norm"][j], w["b_wq"][j],
                                  w["b_kv_norm"][j], w["b_wk"][j], w["b_wvt"][j], tab_bq, tab_bk)
            local = _mla_attn(q, k, v)
        x = _out_proj(x, local, qc, mk, mvt, w["w_o"][i])
        g_final = w["final_norm"] if i == depth - 1 else None
        x = _ffn(x.reshape(b * s, d), w["ffn2_norm"][i], w["ffn2_w_gu"], w["ffn2_w_down"], i,
                 g_final).reshape(b, s, d)
    return x


def kernel(x_prompt, x_sample, mem_prompt, mem_sample, ffn1_norm, ffn1_w_gu, ffn1_w_down, mix_norm,
           mem_norm, w_mem_kv, a_w_in, a_sink, b_w_in, b_q_norm, b_w_q_up, b_kv_norm, b_w_kv_up,
           w_o, ffn2_norm, ffn2_w_gu, ffn2_w_down, final_norm):
    w = {
        "ffn1_norm": ffn1_norm, "ffn2_norm": ffn2_norm, "mix_norm": mix_norm, "mem_norm": mem_norm,
        "w_mem_k": w_mem_kv[:, :, :MEM_WIDTH].astype(BF16),
        "w_mem_vt": jnp.transpose(w_mem_kv[:, :, MEM_WIDTH:], (0, 2, 1)).astype(BF16), "a_sink": a_sink,
        "b_q_norm": b_q_norm, "b_kv_norm": b_kv_norm, "w_o": _prep_w_o(w_o),
        "final_norm": final_norm,
    }
    w["ffn1_w_gu"], w["ffn1_w_down"] = ffn1_w_gu.astype(BF16), ffn1_w_down.astype(BF16)
    w["ffn2_w_gu"], w["ffn2_w_down"] = ffn2_w_gu.astype(BF16), ffn2_w_down.astype(BF16)
    w["a_w_in"], w["a_wvt"] = _prep_a(a_w_in)
    w["b_w_in"], w["b_wq"], w["b_wk"], w["b_wvt"] = _prep_b(b_w_in, b_w_q_up, b_w_kv_up)
    return (_trunk(x_prompt, mem_prompt, w), _trunk(x_sample, mem_sample, w))
```

```python
import functools
import math

import jax
import jax.numpy as jnp
from jax import lax
from jax.experimental import pallas as pl
from jax.experimental.pallas import tpu as pltpu

D_MODEL = 1024
HEAD_DIM = 64
ROPE_THETA = 10000.0
NORM_EPS = 1e-6
D_FF = 2816
N_MEM = 256
MEM_HEADS = 4
MEM_WIDTH = MEM_HEADS * HEAD_DIM
A_Q_HEADS = 12
A_KV_HEADS = 4
A_GROUP = A_Q_HEADS // A_KV_HEADS
WINDOW = 128
A_Q_W = A_Q_HEADS * HEAD_DIM
A_KV_W = A_KV_HEADS * HEAD_DIM
B_HEADS = 12
Q_LORA = 384
KV_LORA = 256
QK_NOPE = 64
QK_ROPE = 32
V_HEAD = 64
B_QK = QK_NOPE + QK_ROPE
LOCAL_W = A_Q_W
NEG = -1e30
LOG2E = math.log2(math.e)

LANES = 128
B_HEAD_PAD = LANES
FF_CHUNK = 256
MLA_QB = 512
MLA_KB = 512
MLA_NKV = 8
MLA_CHUNK = 32
MLA_DEN_ROWS = 16
MLA_AHEAD = 2
MLA_SLOTS = 2 * MLA_AHEAD
VMEM_LIMIT = 56 * 1024 * 1024

F32 = jnp.float32
BF16 = jnp.bfloat16


def _params(n_axes):
    return pltpu.CompilerParams(dimension_semantics=("arbitrary",) * n_axes,
                                vmem_limit_bytes=VMEM_LIMIT)


def _rms(x, g):
    return x * lax.rsqrt(jnp.mean(x * x, axis=-1, keepdims=True) + NORM_EPS) * g


def _dot(a, b):
    return jnp.dot(a, b, preferred_element_type=F32)


def _dot_nt(a, b):
    return lax.dot_general(a, b, (((1,), (1,)), ((), ())), preferred_element_type=F32)


def _resident(shape):
    zeros = (0,) * len(shape)
    return pl.BlockSpec(shape, lambda *_: zeros, pipeline_mode=pl.Buffered(1))


def _ffn_body(*refs, final):
    if final:
        x_ref, g_ref, wgu_ref, wd_ref, gf_ref, o_ref, h_ref = refs
    else:
        x_ref, g_ref, wgu_ref, wd_ref, o_ref, h_ref = refs
    h_ref[...] = _rms(x_ref[...], g_ref[...]).astype(BF16)
    acc = None
    for c in range(D_FF // FF_CHUNK):
        cols = slice(FF_CHUNK * c, FF_CHUNK * (c + 1))
        up_cols = slice(D_FF + FF_CHUNK * c, D_FF + FF_CHUNK * (c + 1))
        h = h_ref[...]
        gate = _dot(h, wgu_ref[:, cols])
        up = _dot(h, wgu_ref[:, up_cols])
        a = (gate / (1.0 + jnp.exp(-gate)) * up).astype(BF16)
        down = _dot(a, wd_ref[cols, :])
        acc = down if acc is None else acc + down
    y = x_ref[...] + 0.5 * acc
    if final:
        y = _rms(y, gf_ref[...])
    o_ref[...] = y


def _layer_resident(stacked, layer):
    zeros = (0,) * (stacked.ndim - 1)
    return pl.BlockSpec((None,) + stacked.shape[1:], lambda *_: (layer,) + zeros,
                        pipeline_mode=pl.Buffered(1))


def _ffn(x, g, w_gu, w_down, layer, g_final=None):
    t, d = x.shape
    tm = min(512, t)
    final = g_final is not None
    in_specs = [
        pl.BlockSpec((tm, d), lambda i: (i, 0)),
        _resident((1, d)),
        _layer_resident(w_gu, layer),
        _layer_resident(w_down, layer),
    ]
    args = [x, g.reshape(1, d), w_gu, w_down]
    if final:
        in_specs.append(_resident((1, d)))
        args.append(g_final.reshape(1, d))
    return pl.pallas_call(
        functools.partial(_ffn_body, final=final),
        grid=(t // tm,),
        in_specs=in_specs,
        out_specs=pl.BlockSpec((tm, d), lambda i: (i, 0)),
        out_shape=jax.ShapeDtypeStruct((t, d), F32),
        scratch_shapes=[pltpu.VMEM((tm, d), BF16)],
        compiler_params=_params(1),
        name="ffn_final" if final else "ffn",
    )(*args)


def _memkv_body(mem_ref, g_ref, wk_ref, wvt_ref, mk_ref, mvt_ref):
    h = _rms(mem_ref[...], g_ref[...]).astype(BF16)
    mk_ref[...] = _dot(h, wk_ref[...]).astype(BF16)
    mvt_ref[...] = _dot_nt(wvt_ref[...], h).astype(BF16)


def _mem_kv(mem, g, wk, wvt):
    b, n, d = mem.shape
    return pl.pallas_call(
        _memkv_body,
        grid=(b,),
        in_specs=[pl.BlockSpec((None, n, d), lambda i: (i, 0, 0)), _resident((1, d)),
                  _resident(wk.shape), _resident(wvt.shape)],
        out_specs=[pl.BlockSpec((None, n, MEM_WIDTH), lambda i: (i, 0, 0)),
                   pl.BlockSpec((None, MEM_WIDTH, n), lambda i: (i, 0, 0))],
        out_shape=[jax.ShapeDtypeStruct((b, n, MEM_WIDTH), BF16),
                   jax.ShapeDtypeStruct((b, MEM_WIDTH, n), BF16)],
        compiler_params=_params(1),
        name="mem_kv",
    )(mem, g.reshape(1, d), wk, wvt)


def _rope_block(xb, tab_ref, shift):
    c = tab_ref[:, 0:LANES]
    s_plus = tab_ref[:, LANES:2 * LANES]
    s_minus = tab_ref[:, 2 * LANES:3 * LANES]
    return (xb * c + pltpu.roll(xb, shift, 1) * s_plus
            + pltpu.roll(xb, LANES - shift, 1) * s_minus)


def _rope_table(seq, dim, lane_start, period, scale, pass_through):
    half = dim // 2
    inv = 1.0 / (ROPE_THETA ** (jnp.arange(0, dim, 2, dtype=F32) / dim))
    ang = jnp.arange(seq, dtype=F32)[:, None] * inv[None, :]
    cos, sin = jnp.cos(ang), jnp.sin(ang)
    zeros_h = jnp.zeros((seq, half), F32)
    lead = jnp.full((seq, lane_start), 1.0 if pass_through else 0.0, F32)
    lead0 = jnp.zeros((seq, lane_start), F32)
    tail0 = jnp.zeros((seq, period - lane_start - dim), F32)
    reps = LANES // period
    c = jnp.tile(jnp.concatenate([lead, cos, cos, tail0], -1), (1, reps))
    s_plus = jnp.tile(jnp.concatenate([lead0, zeros_h, sin, tail0], -1), (1, reps))
    s_minus = jnp.tile(jnp.concatenate([lead0, -sin, zeros_h, tail0], -1), (1, reps))
    return jnp.concatenate([c, s_plus, s_minus], -1) * scale


def _rope_rot_block(xb, tab_ref):
    c = tab_ref[:, 0:LANES]
    s = tab_ref[:, LANES:2 * LANES]
    return xb * c + pltpu.roll(xb, LANES - QK_ROPE, 1) * s


def _rope_rot_table(seq, scale, pass_through):
    inv = 1.0 / (ROPE_THETA ** (jnp.arange(0, QK_ROPE, 2, dtype=F32) / QK_ROPE))
    ang = jnp.arange(seq, dtype=F32)[:, None] * inv[None, :]
    cos, sin = jnp.cos(ang), jnp.sin(ang)
    lead = jnp.full((seq, QK_NOPE), 1.0 if pass_through else 0.0, F32)
    lead0 = jnp.zeros((seq, QK_NOPE), F32)
    tail0 = jnp.zeros((seq, LANES - QK_NOPE - QK_ROPE), F32)
    c = jnp.concatenate([lead, cos, cos, tail0], -1)
    s = jnp.concatenate([lead0, sin, sin, tail0], -1)
    return jnp.concatenate([c, s], -1) * scale


def _with_rotate_half(w_rope):
    x1, x2 = w_rope[..., :QK_ROPE // 2], w_rope[..., QK_ROPE // 2:]
    return jnp.concatenate([w_rope, -x2, x1], axis=-1)


A_HEAD_ORDER = (0, 3, 1, 4, 2, 5, 6, 9, 7, 10, 8, 11)
A_KV_TILES = A_KV_W // LANES
A_TILE_HEADS = A_GROUP
WIN_QBLOCKS = 2


def _proja_body(x_ref, g_ref, w_ref, wvt_ref, tab_ref, q_ref, k_ref, vt_ref, qc_ref, *, qscale):
    h = _rms(x_ref[...], g_ref[...]).astype(BF16)
    proj = _dot(h, w_ref[...])
    nq = A_Q_W // LANES
    nk = A_KV_W // LANES
    for j in range(nq + nk):
        rb = _rope_block(proj[:, LANES * j:LANES * (j + 1)], tab_ref, HEAD_DIM // 2)
        if j < nq:
            q_ref[:, LANES * j:LANES * (j + 1)] = (rb * qscale).astype(BF16)
        else:
            k_ref[:, LANES * (j - nq):LANES * (j - nq + 1)] = rb.astype(BF16)
    qc_ref[...] = (proj[:, A_Q_W + A_KV_W:] * qscale).astype(BF16)
    vt = _dot_nt(wvt_ref[...], h).astype(BF16)
    for i in range(vt_ref.shape[0]):
        vt_ref[i] = vt[:, WINDOW * i:WINDOW * (i + 1)]


def _proj_a(x, g, w, wvt, tab):
    b, s, d = x.shape
    tm = min(512, s)
    nblk = tm // WINDOW

    def out(width):
        return (jax.ShapeDtypeStruct((b, s, width), BF16),
                pl.BlockSpec((None, tm, width), lambda i, j: (i, j, 0)))

    vt = (jax.ShapeDtypeStruct((b, s // WINDOW, A_KV_W, WINDOW), BF16),
          pl.BlockSpec((None, nblk, A_KV_W, WINDOW), lambda i, j: (i, j, 0, 0)))
    shapes, specs = zip(out(A_Q_W), out(A_KV_W), vt, out(MEM_WIDTH))
    return pl.pallas_call(
        functools.partial(_proja_body, qscale=HEAD_DIM ** -0.5 * LOG2E),
        grid=(b, s // tm),
        in_specs=[
            pl.BlockSpec((None, tm, d), lambda i, j: (i, j, 0)),
            _resident((1, d)),
            _resident(w.shape),
            _resident(wvt.shape),
            pl.BlockSpec((tm, 3 * LANES), lambda i, j: (j, 0)),
        ],
        out_specs=list(specs),
        out_shape=list(shapes),
        compiler_params=_params(2),
        name="proj_a",
    )(x, g.reshape(1, d), w, wvt, tab)


def _wattn_body(bias_ref, sink_ref, q_ref, kp_ref, kc_ref, kn_ref, vp_ref, vc_ref, vn_ref, o_ref, *, nsteps):
    step = pl.program_id(1)
    lane = lax.broadcasted_iota(jnp.int32, (WINDOW, LANES), 1)
    ones = jnp.ones((MLA_DEN_ROWS, 3 * WINDOW), BF16)
    zero = jnp.zeros((WINDOW, LANES), BF16)
    for blk in range(WIN_QBLOCKS):
        rows = slice(WINDOW * blk, WINDOW * (blk + 1))
        variant = 1
        if blk == 0:
            variant = jnp.where(step == 0, 0, variant)
        if blk == WIN_QBLOCKS - 1:
            variant = jnp.where(step == nsteps - 1, 2, variant)
        bias = bias_ref[variant]
        for tile in range(A_KV_TILES):
            ts = slice(LANES * tile, LANES * (tile + 1))
            k_blocks = ([kp_ref[:, ts]] + [kc_ref[WINDOW * i:WINDOW * (i + 1), ts] for i in range(WIN_QBLOCKS)]
                        + [kn_ref[:, ts]])
            vt_blocks = [vp_ref[ts, :]] + [vc_ref[i, ts, :] for i in range(WIN_QBLOCKS)] + [vn_ref[ts, :]]
            k_band = jnp.concatenate(k_blocks[blk:blk + 3], axis=0)
            vt_band = jnp.concatenate(vt_blocks[blk:blk + 3], axis=1)
            q_tiles = [q_ref[rows, LANES * (A_TILE_HEADS * tile + r):LANES * (A_TILE_HEADS * tile + r + 1)]
                       for r in range(A_TILE_HEADS)]
            halves = []
            for half in range(2):
                group = 2 * tile + half
                in_half = (lane < HEAD_DIM) if half == 0 else (lane >= HEAD_DIM)
                q_stack = jnp.concatenate([jnp.where(in_half, qt, zero) for qt in q_tiles], axis=0)
                s = _dot_nt(k_band, q_stack) + bias
                sink = sink_ref[group]
                m = jnp.maximum(jnp.max(s, axis=0, keepdims=True), sink)
                p = jnp.exp2(s - m).astype(BF16)
                v_aug = jnp.concatenate([vt_band[HEAD_DIM * half:HEAD_DIM * (half + 1), :], ones], axis=0)
                o_aug = _dot(v_aug, p)
                den = o_aug[HEAD_DIM:HEAD_DIM + 1, :] + jnp.exp2(sink - m)
                halves.append(o_aug[:HEAD_DIM, :] / den)
            for r in range(A_TILE_HEADS):
                cs = slice(WINDOW * r, WINDOW * (r + 1))
                out_t = jnp.concatenate([halves[0][:, cs], halves[1][:, cs]], axis=0)
                j = A_TILE_HEADS * tile + r
                o_ref[rows, LANES * j:LANES * (j + 1)] = out_t.T.astype(BF16)


def _band_bias(dtype=F32):
    kj = jnp.arange(3 * WINDOW)[:, None]
    qi = (jnp.arange(3 * WINDOW) % WINDOW)[None, :]
    band = (kj - qi >= 0) & (kj - qi <= 2 * WINDOW)
    first = band & (kj >= WINDOW)
    last = band & (kj < 2 * WINDOW)
    return jnp.where(jnp.stack([first, band, last]), 0.0, NEG).astype(dtype)


def _win_attn(q, k, vt, sink):
    b, s, _ = q.shape
    nb = s // WINDOW
    nsteps = nb // WIN_QBLOCKS
    assert nb >= 2 and nb % WIN_QBLOCKS == 0
    qrows = WIN_QBLOCKS * WINDOW
    sink_rows = jnp.repeat(sink.reshape(A_KV_HEADS, 1, A_GROUP) * LOG2E, WINDOW, axis=-1)

    def halo(j, shift):
        return jnp.clip(WIN_QBLOCKS * j + shift, 0, nb - 1)

    def k_halo(shift):
        return pl.BlockSpec((None, WINDOW, A_KV_W), lambda i, j: (i, halo(j, shift), 0))

    def vt_halo(shift):
        return pl.BlockSpec((None, None, A_KV_W, WINDOW), lambda i, j: (i, halo(j, shift), 0, 0))

    return pl.pallas_call(
        functools.partial(_wattn_body, nsteps=nsteps),
        grid=(b, nsteps),
        in_specs=[
            _resident((3, 3 * WINDOW, 3 * WINDOW)),
            _resident((A_KV_HEADS, 1, 3 * WINDOW)),
            pl.BlockSpec((None, qrows, A_Q_W), lambda i, j: (i, j, 0)),
            k_halo(-1),
            pl.BlockSpec((None, qrows, A_KV_W), lambda i, j: (i, j, 0)),
            k_halo(WIN_QBLOCKS),
            vt_halo(-1),
            pl.BlockSpec((None, WIN_QBLOCKS, A_KV_W, WINDOW), lambda i, j: (i, j, 0, 0)),
            vt_halo(WIN_QBLOCKS),
        ],
        out_specs=pl.BlockSpec((None, qrows, A_Q_W), lambda i, j: (i, j, 0)),
        out_shape=jax.ShapeDtypeStruct((b, s, A_Q_W), BF16),
        compiler_params=_params(2),
        name="win_attn",
    )(_band_bias(), sink_rows, q, k, k, k, vt, vt, vt)


def _projb_body(x_ref, g_ref, win_ref, gq_ref, wq_ref, gkv_ref, wk_ref, wvt_ref, tabq_ref, tabk_ref,
                q_ref, k_ref, v_ref, qc_ref, *, qcscale):
    h = _rms(x_ref[...], g_ref[...]).astype(BF16)
    proj = _dot(h, win_ref[...])
    c_q = _rms(proj[:, :Q_LORA], gq_ref[...]).astype(BF16)
    kv0 = Q_LORA
    c_kv = _rms(proj[:, kv0:kv0 + KV_LORA], gkv_ref[...]).astype(BF16)
    qc0 = kv0 + KV_LORA
    qc_ref[...] = (proj[:, qc0:qc0 + MEM_WIDTH] * qcscale).astype(BF16)
    kr0 = qc0 + MEM_WIDTH
    k_rope = _rope_rot_block(proj[:, kr0:kr0 + LANES], tabk_ref)
    q_all = _dot(c_q, wq_ref[...])
    k_all = _dot(c_kv, wk_ref[...])
    for hd in range(B_HEADS):
        hs = slice(B_HEAD_PAD * hd, B_HEAD_PAD * (hd + 1))
        q_ref[:, hs] = _rope_rot_block(q_all[:, hs], tabq_ref).astype(BF16)
        k_ref[:, hs] = (k_all[:, hs] + k_rope).astype(BF16)
    v_ref[...] = _dot_nt(wvt_ref[...], c_kv).astype(BF16)


def _proj_b(x, g, w_in, gq, wq, gkv, wk, wvt, tabq, tabk):
    b, s, d = x.shape
    tm = min(max(MLA_KB, s // MLA_NKV), s)

    def out(width):
        return (jax.ShapeDtypeStruct((b, s, width), BF16),
                pl.BlockSpec((None, tm, width), lambda i, j: (i, j, 0)))

    vt = (jax.ShapeDtypeStruct((b, s // tm, B_HEADS * V_HEAD, tm), BF16),
          pl.BlockSpec((None, None, B_HEADS * V_HEAD, tm), lambda i, j: (i, j, 0, 0)))
    shapes, specs = zip(out(B_HEADS * B_HEAD_PAD), out(B_HEADS * B_HEAD_PAD), vt, out(MEM_WIDTH))
    tab_spec = pl.BlockSpec((tm, 2 * LANES), lambda i, j: (j, 0))
    return pl.pallas_call(
        functools.partial(_projb_body, qcscale=HEAD_DIM ** -0.5 * LOG2E),
        grid=(b, s // tm),
        in_specs=[
            pl.BlockSpec((None, tm, d), lambda i, j: (i, j, 0)),
            _resident((1, d)),
            _resident(w_in.shape),
            _resident((1, Q_LORA)),
            _resident(wq.shape),
            _resident((1, KV_LORA)),
            _resident(wk.shape),
            _resident(wvt.shape),
            tab_spec, tab_spec,
        ],
        out_specs=list(specs),
        out_shape=list(shapes),
        compiler_params=_params(2),
        name="proj_b",
    )(x, g.reshape(1, d), w_in, gq.reshape(1, -1), wq, gkv.reshape(1, -1), wk, wvt, tabq, tabk)


def _mla_body(q_ref, k_ref, vt_ref, o_ref, *scratch, kb, nkv):
    per_slot = 3 * 2
    slots = [scratch[per_slot * i:per_slot * (i + 1)] for i in range(MLA_SLOTS)]
    s_refs = [sl[0:2] for sl in slots]
    p_refs = [sl[2:4] for sl in slots]
    alpha_refs = [sl[4:6] for sl in slots]
    m_refs, acc_refs = (scratch[per_slot * MLA_SLOTS + 2 * i:per_slot * MLA_SLOTS + 2 * (i + 1)]
                        for i in range(2))
    hslices = [slice(B_HEAD_PAD * hh, B_HEAD_PAD * (hh + 1)) for hh in range(2)]
    vslices = [slice(V_HEAD * hh, V_HEAD * (hh + 1)) for hh in range(2)]
    chunks = [slice(c, c + MLA_CHUNK) for c in range(0, kb, MLA_CHUNK)]

    def scores(t, slot):
        rows = pl.ds(pl.multiple_of(t * kb, kb), kb)
        for hh, hs in enumerate(hslices):
            s_refs[slot][hh][...] = _dot_nt(k_ref[rows, hs], q_ref[:, hs])

    def softmax(slot):
        for hh in range(2):
            s_ref, p_ref = s_refs[slot][hh], p_refs[slot][hh]
            cmax = s_ref[chunks[0], :]
            for ch in chunks[1:]:
                cmax = jnp.maximum(cmax, s_ref[ch, :])
            m = m_refs[hh][...]
            m_new = jnp.maximum(m, jnp.max(cmax, axis=0, keepdims=True))
            alpha = jnp.exp2(m - m_new)
            for ch in chunks:
                p_ref[ch, :] = jnp.exp2(s_ref[ch, :] - m_new).astype(BF16)
            alpha_refs[slot][hh][...] = alpha
            m_refs[hh][...] = m_new

    def accumulate(t, slot):
        ones = jnp.ones((MLA_DEN_ROWS, kb), BF16)
        for hh, vs in enumerate(vslices):
            v_aug = jnp.concatenate([vt_ref[t, vs, :], ones], axis=0)
            acc_refs[hh][...] = (alpha_refs[slot][hh][...] * acc_refs[hh][...]
                                 + _dot(v_aug, p_refs[slot][hh][...]))

    def stage(t, phase, ahead=True, behind=True):
        if ahead:
            scores(t + MLA_AHEAD, (phase + MLA_AHEAD) % MLA_SLOTS)
        if behind:
            accumulate(t - MLA_AHEAD, (phase - MLA_AHEAD) % MLA_SLOTS)
        softmax(phase)

    lo = MLA_AHEAD
    trips = max(nkv - 2 * MLA_AHEAD, 0) // MLA_SLOTS
    tail = range(lo + trips * MLA_SLOTS, nkv)
    assert all(t + MLA_AHEAD >= nkv for t in tail)

    def fill_and_loop():
        for hh in range(2):
            m_refs[hh][...] = jnp.full(m_refs[hh].shape, NEG, F32)
            acc_refs[hh][...] = jnp.zeros(acc_refs[hh].shape, F32)
        for t in range(min(MLA_AHEAD, nkv)):
            scores(t, t % MLA_SLOTS)
        for t in range(min(lo, nkv)):
            stage(t, t % MLA_SLOTS, ahead=t + MLA_AHEAD < nkv, behind=False)

        def full_stages(i, carry):
            for j in range(MLA_SLOTS):
                stage(lo + MLA_SLOTS * i + j, (lo + j) % MLA_SLOTS)
            return carry

        lax.fori_loop(0, trips, full_stages, 0)

    def drain():
        for t in tail:
            stage(t, t % MLA_SLOTS, ahead=False, behind=t >= MLA_AHEAD)
        for t in range(max(nkv - MLA_AHEAD, 0), nkv):
            accumulate(t, t % MLA_SLOTS)
        out_t = jnp.concatenate(
            [acc_refs[hh][:V_HEAD, :] / acc_refs[hh][V_HEAD:V_HEAD + 1, :] for hh in range(2)], axis=0)
        o_ref[...] = out_t.T.astype(BF16)

    n = pl.program_id(2)
    last = pl.num_programs(2) - 1

    @pl.when(n == 0)
    def _():
        fill_and_loop()

    @pl.when(jnp.logical_and(n > 0, n < last))
    def _():
        drain()
        fill_and_loop()

    @pl.when(n == last)
    def _():
        drain()


def _mla_attn(q, k, vt):
    b, s, _ = q.shape
    _, nkv, _, kb = vt.shape
    qb = min(MLA_QB, s)
    nq = s // qb
    npair = B_HEADS // 2
    per_slot = ([pltpu.VMEM((kb, qb), F32)] * 2
                + [pltpu.VMEM((kb, qb), BF16)] * 2
                + [pltpu.VMEM((1, qb), F32)] * 2)
    state = ([pltpu.VMEM((1, qb), F32)] * 2
             + [pltpu.VMEM((V_HEAD + MLA_DEN_ROWS, qb), F32)] * 2)
    return pl.pallas_call(
        functools.partial(_mla_body, kb=kb, nkv=nkv),
        grid=(b, npair, nq + 1),
        in_specs=[
            pl.BlockSpec((None, qb, 2 * B_HEAD_PAD), lambda i, j, n: (i, jnp.minimum(n, nq - 1), j)),
            pl.BlockSpec((None, s, 2 * B_HEAD_PAD), lambda i, j, n: (i, 0, j)),
            pl.BlockSpec((None, nkv, 2 * V_HEAD, kb), lambda i, j, n: (i, 0, j, 0)),
        ],
        out_specs=pl.BlockSpec((None, qb, 2 * V_HEAD), lambda i, j, n: (i, jnp.maximum(n - 1, 0), j)),
        out_shape=jax.ShapeDtypeStruct((b, s, B_HEADS * V_HEAD), BF16),
        scratch_shapes=per_slot * MLA_SLOTS + state,
        compiler_params=_params(3),
        name="mla_attn",
    )(q, k, vt)


def _oproj_body(x_ref, loc_ref, qc_ref, mk_ref, mvt_ref, wo_ref, o_ref):
    tm = qc_ref.shape[0]
    lane = lax.broadcasted_iota(jnp.int32, (tm, LANES), 1)
    ones = jnp.ones((MLA_DEN_ROWS, N_MEM), BF16)
    zero = jnp.zeros((tm, LANES), BF16)
    tiles = []
    for tile in range(MEM_WIDTH // LANES):
        ts = slice(LANES * tile, LANES * (tile + 1))
        halves = []
        for half in range(2):
            hd = 2 * tile + half
            in_half = (lane < HEAD_DIM) if half == 0 else (lane >= HEAD_DIM)
            s = _dot_nt(mk_ref[:, ts], jnp.where(in_half, qc_ref[:, ts], zero))
            p = jnp.exp2(s - jnp.max(s, axis=0, keepdims=True)).astype(BF16)
            v_aug = jnp.concatenate([mvt_ref[HEAD_DIM * hd:HEAD_DIM * (hd + 1), :], ones], axis=0)
            o_aug = _dot(v_aug, p)
            halves.append(o_aug[:HEAD_DIM, :] / o_aug[HEAD_DIM:HEAD_DIM + 1, :])
        tiles.append(jnp.concatenate(halves, axis=0).T.astype(BF16))
    cross = jnp.concatenate(tiles, axis=-1)
    y = _dot(loc_ref[...], wo_ref[:LOCAL_W, :]) + _dot(cross, wo_ref[LOCAL_W:, :])
    o_ref[...] = x_ref[...] + y


def _out_proj(x, local, qc, mk, mvt, wo):
    b, s, d = x.shape
    tm = min(512, s)

    def tok(width):
        return pl.BlockSpec((None, tm, width), lambda i, j: (i, j, 0))

    return pl.pallas_call(
        _oproj_body,
        grid=(b, s // tm),
        in_specs=[tok(d), tok(LOCAL_W), tok(MEM_WIDTH),
                  pl.BlockSpec((None, N_MEM, MEM_WIDTH), lambda i, j: (i, 0, 0)),
                  pl.BlockSpec((None, MEM_WIDTH, N_MEM), lambda i, j: (i, 0, 0)),
                  _resident(wo.shape)],
        out_specs=tok(d),
        out_shape=jax.ShapeDtypeStruct((b, s, d), F32),
        compiler_params=_params(2),
        name="out_proj",
    )(x, local, qc, mk, mvt, wo)


def _prep_a(a_w_in):
    nl, d, _ = a_w_in.shape
    order = jnp.array(A_HEAD_ORDER)
    q = a_w_in[:, :, :A_Q_W].reshape(nl, d, A_Q_HEADS, HEAD_DIM)[:, :, order].reshape(nl, d, A_Q_W)
    k = a_w_in[:, :, A_Q_W:A_Q_W + A_KV_W]
    v = a_w_in[:, :, A_Q_W + A_KV_W:A_Q_W + 2 * A_KV_W]
    qc = a_w_in[:, :, A_Q_W + 2 * A_KV_W:]
    w = jnp.concatenate([q, k, qc], axis=-1).astype(BF16)
    return w, jnp.transpose(v, (0, 2, 1)).astype(BF16)


def _prep_w_o(w_o):
    depth, _, d = w_o.shape
    order = jnp.array(A_HEAD_ORDER)
    local = w_o[:, :LOCAL_W].reshape(depth, A_Q_HEADS, HEAD_DIM, d)
    local = jnp.where((jnp.arange(depth) % 2 == 0)[:, None, None, None], local[:, order], local)
    return jnp.concatenate([local.reshape(depth, LOCAL_W, d), w_o[:, LOCAL_W:]], axis=1).astype(BF16)


def _prep_b(b_w_in, b_w_q_up, b_w_kv_up):
    nl, d, _ = b_w_in.shape
    c_q = b_w_in[:, :, :Q_LORA]
    c_kv = b_w_in[:, :, Q_LORA:Q_LORA + KV_LORA]
    k_r = b_w_in[:, :, Q_LORA + KV_LORA:Q_LORA + KV_LORA + QK_ROPE]
    qc = b_w_in[:, :, Q_LORA + KV_LORA + QK_ROPE:]
    kr_tile = jnp.pad(_with_rotate_half(k_r), ((0, 0), (0, 0), (QK_NOPE, 0)))
    w_in = jnp.concatenate([c_q, c_kv, qc, kr_tile], axis=-1).astype(BF16)
    wq = b_w_q_up.reshape(nl, Q_LORA, B_HEADS, B_QK)
    wq = jnp.concatenate([wq[..., :QK_NOPE], _with_rotate_half(wq[..., QK_NOPE:])], axis=-1)
    wq = wq.reshape(nl, Q_LORA, B_HEADS * B_HEAD_PAD).astype(BF16)
    wkv = b_w_kv_up.reshape(nl, KV_LORA, B_HEADS, QK_NOPE + V_HEAD)
    wk = jnp.pad(wkv[..., :QK_NOPE], ((0, 0), (0, 0), (0, 0), (0, B_HEAD_PAD - QK_NOPE)))
    wk = wk.reshape(nl, KV_LORA, B_HEADS * B_HEAD_PAD).astype(BF16)
    wvt = jnp.transpose(wkv[..., QK_NOPE:].reshape(nl, KV_LORA, B_HEADS * V_HEAD), (0, 2, 1)).astype(BF16)
    return w_in, wq, wk, wvt


def _trunk(x, mem, w):
    b, s, d = x.shape
    depth = w["mix_norm"].shape[0]
    tab_a = _rope_table(s, HEAD_DIM, 0, HEAD_DIM, 1.0, False)
    tab_bq = _rope_rot_table(s, B_QK ** -0.5 * LOG2E, True)
    tab_bk = _rope_rot_table(s, 1.0, False)
    for i in range(depth):
        x = _ffn(x.reshape(b * s, d), w["ffn1_norm"][i], w["ffn1_w_gu"], w["ffn1_w_down"], i).reshape(b, s, d)
        mk, mvt = _mem_kv(mem, w["mem_norm"][i], w["w_mem_k"][i], w["w_mem_vt"][i])
        j = i // 2
        if i % 2 == 0:
            q, k, vt, qc = _proj_a(x, w["mix_norm"][i], w["a_w_in"][j], w["a_wvt"][j], tab_a)
            local = _win_attn(q, k, vt, w["a_sink"][j])
        else:
            q, k, v, qc = _proj_b(x, w["mix_norm"][i], w["b_w_in"][j], w["b_q_norm"][j], w["b_wq"][j],
                                  w["b_kv_norm"][j], w["b_wk"][j], w["b_wvt"][j], tab_bq, tab_bk)
            local = _mla_attn(q, k, v)
        x = _out_proj(x, local, qc, mk, mvt, w["w_o"][i])
        g_final = w["final_norm"] if i == depth - 1 else None
        x = _ffn(x.reshape(b * s, d), w["ffn2_norm"][i], w["ffn2_w_gu"], w["ffn2_w_down"], i,
                 g_final).reshape(b, s, d)
    return x


def kernel(x_prompt, x_sample, mem_prompt, mem_sample, ffn1_norm, ffn1_w_gu, ffn1_w_down, mix_norm,
           mem_norm, w_mem_kv, a_w_in, a_sink, b_w_in, b_q_norm, b_w_q_up, b_kv_norm, b_w_kv_up,
           w_o, ffn2_norm, ffn2_w_gu, ffn2_w_down, final_norm):
    w = {
        "ffn1_norm": ffn1_norm, "ffn2_norm": ffn2_norm, "mix_norm": mix_norm, "mem_norm": mem_norm,
        "w_mem_k": w_mem_kv[:, :, :MEM_WIDTH].astype(BF16),
        "w_mem_vt": jnp.transpose(w_mem_kv[:, :, MEM_WIDTH:], (0, 2, 1)).astype(BF16), "a_sink": a_sink,
        "b_q_norm": b_q_norm, "b_kv_norm": b_kv_norm, "w_o": _prep_w_o(w_o),
        "final_norm": final_norm,
    }
    w["ffn1_w_gu"], w["ffn1_w_down"] = ffn1_w_gu.astype(BF16), ffn1_w_down.astype(BF16)
    w["ffn2_w_gu"], w["ffn2_w_down"] = ffn2_w_gu.astype(BF16), ffn2_w_down.astype(BF16)
    w["a_w_in"], w["a_wvt"] = _prep_a(a_w_in)
    w["b_w_in"], w["b_wq"], w["b_wk"], w["b_wvt"] = _prep_b(b_w_in, b_w_q_up, b_w_kv_up)
    return (_trunk(x_prompt, mem_prompt, w), _trunk(x_sample, mem_sample, w))
```

```python
import functools
import math

import jax
import jax.numpy as jnp
from jax import lax
from jax.experimental import pallas as pl
from jax.experimental.pallas import tpu as pltpu

D_MODEL = 1024
HEAD_DIM = 64
ROPE_THETA = 10000.0
NORM_EPS = 1e-6
D_FF = 2816
N_MEM = 256
MEM_HEADS = 4
MEM_WIDTH = MEM_HEADS * HEAD_DIM
A_Q_HEADS = 12
A_KV_HEADS = 4
A_GROUP = A_Q_HEADS // A_KV_HEADS
WINDOW = 128
A_Q_W = A_Q_HEADS * HEAD_DIM
A_KV_W = A_KV_HEADS * HEAD_DIM
B_HEADS = 12
Q_LORA = 384
KV_LORA = 256
QK_NOPE = 64
QK_ROPE = 32
V_HEAD = 64
B_QK = QK_NOPE + QK_ROPE
LOCAL_W = A_Q_W
NEG = -1e30
LOG2E = math.log2(math.e)

LANES = 128
B_HEAD_PAD = LANES
FF_CHUNK = 256
OPROJ_SUB = 512
MLA_QB = 512
MLA_KB = 512
MLA_NKV = 8
MLA_CHUNK = 32
MLA_DEN_ROWS = 16
MLA_AHEAD = 2
MLA_SLOTS = 2 * MLA_AHEAD
VMEM_LIMIT = 56 * 1024 * 1024

F32 = jnp.float32
BF16 = jnp.bfloat16


def _params(n_axes):
    return pltpu.CompilerParams(dimension_semantics=("arbitrary",) * n_axes,
                                vmem_limit_bytes=VMEM_LIMIT)


def _rms(x, g):
    return x * lax.rsqrt(jnp.mean(x * x, axis=-1, keepdims=True) + NORM_EPS) * g


def _dot(a, b):
    return jnp.dot(a, b, preferred_element_type=F32)


def _dot_nt(a, b):
    return lax.dot_general(a, b, (((1,), (1,)), ((), ())), preferred_element_type=F32)


def _resident(shape):
    zeros = (0,) * len(shape)
    return pl.BlockSpec(shape, lambda *_: zeros, pipeline_mode=pl.Buffered(1))


def _ffn_body(*refs, final):
    if final:
        x_ref, g_ref, wgu_ref, wd_ref, gf_ref, o_ref, h_ref = refs
    else:
        x_ref, g_ref, wgu_ref, wd_ref, o_ref, h_ref = refs
    h_ref[...] = _rms(x_ref[...], g_ref[...]).astype(BF16)
    acc = None
    for c in range(D_FF // FF_CHUNK):
        cols = slice(FF_CHUNK * c, FF_CHUNK * (c + 1))
        up_cols = slice(D_FF + FF_CHUNK * c, D_FF + FF_CHUNK * (c + 1))
        h = h_ref[...]
        gate = _dot(h, wgu_ref[:, cols])
        up = _dot(h, wgu_ref[:, up_cols])
        a = (gate / (1.0 + jnp.exp(-gate)) * up).astype(BF16)
        down = _dot(a, wd_ref[cols, :])
        acc = down if acc is None else acc + down
    y = x_ref[...] + 0.5 * acc
    if final:
        y = _rms(y, gf_ref[...])
    o_ref[...] = y


def _layer_resident(stacked, layer):
    zeros = (0,) * (stacked.ndim - 1)
    return pl.BlockSpec((None,) + stacked.shape[1:], lambda *_: (layer,) + zeros,
                        pipeline_mode=pl.Buffered(1))


def _ffn(x, g, w_gu, w_down, layer, g_final=None):
    t, d = x.shape
    tm = min(1024, t)
    final = g_final is not None
    in_specs = [
        pl.BlockSpec((tm, d), lambda i: (i, 0)),
        _resident((1, d)),
        _layer_resident(w_gu, layer),
        _layer_resident(w_down, layer),
    ]
    args = [x, g.reshape(1, d), w_gu, w_down]
    if final:
        in_specs.append(_resident((1, d)))
        args.append(g_final.reshape(1, d))
    return pl.pallas_call(
        functools.partial(_ffn_body, final=final),
        grid=(t // tm,),
        in_specs=in_specs,
        out_specs=pl.BlockSpec((tm, d), lambda i: (i, 0)),
        out_shape=jax.ShapeDtypeStruct((t, d), F32),
        scratch_shapes=[pltpu.VMEM((tm, d), BF16)],
        compiler_params=_params(1),
        name="ffn_final" if final else "ffn",
    )(*args)


def _memkv_body(mem_ref, g_ref, wk_ref, wvt_ref, mk_ref, mvt_ref):
    h = _rms(mem_ref[...], g_ref[...]).astype(BF16)
    mk_ref[...] = _dot(h, wk_ref[...]).astype(BF16)
    mvt_ref[...] = _dot_nt(wvt_ref[...], h).astype(BF16)


def _mem_kv(mem, g, wk, wvt):
    b, n, d = mem.shape
    return pl.pallas_call(
        _memkv_body,
        grid=(b,),
        in_specs=[pl.BlockSpec((None, n, d), lambda i: (i, 0, 0)), _resident((1, d)),
                  _resident(wk.shape), _resident(wvt.shape)],
        out_specs=[pl.BlockSpec((None, n, MEM_WIDTH), lambda i: (i, 0, 0)),
                   pl.BlockSpec((None, MEM_WIDTH, n), lambda i: (i, 0, 0))],
        out_shape=[jax.ShapeDtypeStruct((b, n, MEM_WIDTH), BF16),
                   jax.ShapeDtypeStruct((b, MEM_WIDTH, n), BF16)],
        compiler_params=_params(1),
        name="mem_kv",
    )(mem, g.reshape(1, d), wk, wvt)


def _rope_block(xb, tab_ref, shift):
    c = tab_ref[:, 0:LANES]
    s_plus = tab_ref[:, LANES:2 * LANES]
    s_minus = tab_ref[:, 2 * LANES:3 * LANES]
    return (xb * c + pltpu.roll(xb, shift, 1) * s_plus
            + pltpu.roll(xb, LANES - shift, 1) * s_minus)


def _rope_table(seq, dim, lane_start, period, scale, pass_through):
    half = dim // 2
    inv = 1.0 / (ROPE_THETA ** (jnp.arange(0, dim, 2, dtype=F32) / dim))
    ang = jnp.arange(seq, dtype=F32)[:, None] * inv[None, :]
    cos, sin = jnp.cos(ang), jnp.sin(ang)
    zeros_h = jnp.zeros((seq, half), F32)
    lead = jnp.full((seq, lane_start), 1.0 if pass_through else 0.0, F32)
    lead0 = jnp.zeros((seq, lane_start), F32)
    tail0 = jnp.zeros((seq, period - lane_start - dim), F32)
    reps = LANES // period
    c = jnp.tile(jnp.concatenate([lead, cos, cos, tail0], -1), (1, reps))
    s_plus = jnp.tile(jnp.concatenate([lead0, zeros_h, sin, tail0], -1), (1, reps))
    s_minus = jnp.tile(jnp.concatenate([lead0, -sin, zeros_h, tail0], -1), (1, reps))
    return jnp.concatenate([c, s_plus, s_minus], -1) * scale


def _rope_rot_block(xb, tab_ref):
    c = tab_ref[:, 0:LANES]
    s = tab_ref[:, LANES:2 * LANES]
    return xb * c + pltpu.roll(xb, LANES - QK_ROPE, 1) * s


def _rope_rot_table(seq, scale, pass_through):
    inv = 1.0 / (ROPE_THETA ** (jnp.arange(0, QK_ROPE, 2, dtype=F32) / QK_ROPE))
    ang = jnp.arange(seq, dtype=F32)[:, None] * inv[None, :]
    cos, sin = jnp.cos(ang), jnp.sin(ang)
    lead = jnp.full((seq, QK_NOPE), 1.0 if pass_through else 0.0, F32)
    lead0 = jnp.zeros((seq, QK_NOPE), F32)
    tail0 = jnp.zeros((seq, LANES - QK_NOPE - QK_ROPE), F32)
    c = jnp.concatenate([lead, cos, cos, tail0], -1)
    s = jnp.concatenate([lead0, sin, sin, tail0], -1)
    return jnp.concatenate([c, s], -1) * scale


def _with_rotate_half(w_rope):
    x1, x2 = w_rope[..., :QK_ROPE // 2], w_rope[..., QK_ROPE // 2:]
    return jnp.concatenate([w_rope, -x2, x1], axis=-1)


A_HEAD_ORDER = (0, 3, 1, 4, 2, 5, 6, 9, 7, 10, 8, 11)
A_KV_TILES = A_KV_W // LANES
A_TILE_HEADS = A_GROUP
WIN_QBLOCKS = 4


def _proja_body(x_ref, g_ref, w_ref, wvt_ref, tab_ref, q_ref, k_ref, vt_ref, qc_ref, *, qscale):
    h = _rms(x_ref[...], g_ref[...]).astype(BF16)
    proj = _dot(h, w_ref[...])
    nq = A_Q_W // LANES
    nk = A_KV_W // LANES
    for j in range(nq + nk):
        rb = _rope_block(proj[:, LANES * j:LANES * (j + 1)], tab_ref, HEAD_DIM // 2)
        if j < nq:
            q_ref[:, LANES * j:LANES * (j + 1)] = (rb * qscale).astype(BF16)
        else:
            k_ref[:, LANES * (j - nq):LANES * (j - nq + 1)] = rb.astype(BF16)
    qc_ref[...] = (proj[:, A_Q_W + A_KV_W:] * qscale).astype(BF16)
    vt = _dot_nt(wvt_ref[...], h).astype(BF16)
    for i in range(vt_ref.shape[0]):
        vt_ref[i] = vt[:, WINDOW * i:WINDOW * (i + 1)]


def _proj_a(x, g, w, wvt, tab):
    b, s, d = x.shape
    tm = min(512, s)
    nblk = tm // WINDOW

    def out(width):
        return (jax.ShapeDtypeStruct((b, s, width), BF16),
                pl.BlockSpec((None, tm, width), lambda i, j: (i, j, 0)))

    vt = (jax.ShapeDtypeStruct((b, s // WINDOW, A_KV_W, WINDOW), BF16),
          pl.BlockSpec((None, nblk, A_KV_W, WINDOW), lambda i, j: (i, j, 0, 0)))
    shapes, specs = zip(out(A_Q_W), out(A_KV_W), vt, out(MEM_WIDTH))
    return pl.pallas_call(
        functools.partial(_proja_body, qscale=HEAD_DIM ** -0.5 * LOG2E),
        grid=(b, s // tm),
        in_specs=[
            pl.BlockSpec((None, tm, d), lambda i, j: (i, j, 0)),
            _resident((1, d)),
            _resident(w.shape),
            _resident(wvt.shape),
            pl.BlockSpec((tm, 3 * LANES), lambda i, j: (j, 0)),
        ],
        out_specs=list(specs),
        out_shape=list(shapes),
        compiler_params=_params(2),
        name="proj_a",
    )(x, g.reshape(1, d), w, wvt, tab)


def _wattn_body(bias_ref, sink_ref, q_ref, kp_ref, kc_ref, kn_ref, vp_ref, vc_ref, vn_ref, o_ref, *, nsteps):
    step = pl.program_id(1)
    lane = lax.broadcasted_iota(jnp.int32, (WINDOW, LANES), 1)
    ones = jnp.ones((MLA_DEN_ROWS, 3 * WINDOW), BF16)
    zero = jnp.zeros((WINDOW, LANES), BF16)
    for blk in range(WIN_QBLOCKS):
        rows = slice(WINDOW * blk, WINDOW * (blk + 1))
        variant = 1
        if blk == 0:
            variant = jnp.where(step == 0, 0, variant)
        if blk == WIN_QBLOCKS - 1:
            variant = jnp.where(step == nsteps - 1, 2, variant)
        bias = bias_ref[variant]
        for tile in range(A_KV_TILES):
            ts = slice(LANES * tile, LANES * (tile + 1))
            k_blocks = ([kp_ref[:, ts]] + [kc_ref[WINDOW * i:WINDOW * (i + 1), ts] for i in range(WIN_QBLOCKS)]
                        + [kn_ref[:, ts]])
            vt_blocks = [vp_ref[ts, :]] + [vc_ref[i, ts, :] for i in range(WIN_QBLOCKS)] + [vn_ref[ts, :]]
            k_band = jnp.concatenate(k_blocks[blk:blk + 3], axis=0)
            vt_band = jnp.concatenate(vt_blocks[blk:blk + 3], axis=1)
            q_tiles = [q_ref[rows, LANES * (A_TILE_HEADS * tile + r):LANES * (A_TILE_HEADS * tile + r + 1)]
                       for r in range(A_TILE_HEADS)]
            halves = []
            for half in range(2):
                group = 2 * tile + half
                in_half = (lane < HEAD_DIM) if half == 0 else (lane >= HEAD_DIM)
                q_stack = jnp.concatenate([jnp.where(in_half, qt, zero) for qt in q_tiles], axis=0)
                s = _dot_nt(k_band, q_stack) + bias
                sink = sink_ref[group]
                m = jnp.maximum(jnp.max(s, axis=0, keepdims=True), sink)
                p = jnp.exp2(s - m).astype(BF16)
                v_aug = jnp.concatenate([vt_band[HEAD_DIM * half:HEAD_DIM * (half + 1), :], ones], axis=0)
                o_aug = _dot(v_aug, p)
                den = o_aug[HEAD_DIM:HEAD_DIM + 1, :] + jnp.exp2(sink - m)
                halves.append(o_aug[:HEAD_DIM, :] / den)
            for r in range(A_TILE_HEADS):
                cs = slice(WINDOW * r, WINDOW * (r + 1))
                out_t = jnp.concatenate([halves[0][:, cs], halves[1][:, cs]], axis=0)
                j = A_TILE_HEADS * tile + r
                o_ref[rows, LANES * j:LANES * (j + 1)] = out_t.T.astype(BF16)


def _band_bias(dtype=F32):
    kj = jnp.arange(3 * WINDOW)[:, None]
    qi = (jnp.arange(3 * WINDOW) % WINDOW)[None, :]
    band = (kj - qi >= 0) & (kj - qi <= 2 * WINDOW)
    first = band & (kj >= WINDOW)
    last = band & (kj < 2 * WINDOW)
    return jnp.where(jnp.stack([first, band, last]), 0.0, NEG).astype(dtype)


def _win_attn(q, k, vt, sink):
    b, s, _ = q.shape
    nb = s // WINDOW
    nsteps = nb // WIN_QBLOCKS
    assert nb >= 2 and nb % WIN_QBLOCKS == 0
    qrows = WIN_QBLOCKS * WINDOW
    sink_rows = jnp.repeat(sink.reshape(A_KV_HEADS, 1, A_GROUP) * LOG2E, WINDOW, axis=-1)

    def halo(j, shift):
        return jnp.clip(WIN_QBLOCKS * j + shift, 0, nb - 1)

    def k_halo(shift):
        return pl.BlockSpec((None, WINDOW, A_KV_W), lambda i, j: (i, halo(j, shift), 0))

    def vt_halo(shift):
        return pl.BlockSpec((None, None, A_KV_W, WINDOW), lambda i, j: (i, halo(j, shift), 0, 0))

    return pl.pallas_call(
        functools.partial(_wattn_body, nsteps=nsteps),
        grid=(b, nsteps),
        in_specs=[
            _resident((3, 3 * WINDOW, 3 * WINDOW)),
            _resident((A_KV_HEADS, 1, 3 * WINDOW)),
            pl.BlockSpec((None, qrows, A_Q_W), lambda i, j: (i, j, 0)),
            k_halo(-1),
            pl.BlockSpec((None, qrows, A_KV_W), lambda i, j: (i, j, 0)),
            k_halo(WIN_QBLOCKS),
            vt_halo(-1),
            pl.BlockSpec((None, WIN_QBLOCKS, A_KV_W, WINDOW), lambda i, j: (i, j, 0, 0)),
            vt_halo(WIN_QBLOCKS),
        ],
        out_specs=pl.BlockSpec((None, qrows, A_Q_W), lambda i, j: (i, j, 0)),
        out_shape=jax.ShapeDtypeStruct((b, s, A_Q_W), BF16),
        compiler_params=_params(2),
        name="win_attn",
    )(_band_bias(), sink_rows, q, k, k, k, vt, vt, vt)


def _projb_body(x_ref, g_ref, win_ref, gq_ref, wq_ref, gkv_ref, wk_ref, wvt_ref, tabq_ref, tabk_ref,
                q_ref, k_ref, v_ref, qc_ref, *, qcscale):
    h = _rms(x_ref[...], g_ref[...]).astype(BF16)
    proj = _dot(h, win_ref[...])
    c_q = _rms(proj[:, :Q_LORA], gq_ref[...]).astype(BF16)
    kv0 = Q_LORA
    c_kv = _rms(proj[:, kv0:kv0 + KV_LORA], gkv_ref[...]).astype(BF16)
    qc0 = kv0 + KV_LORA
    qc_ref[...] = (proj[:, qc0:qc0 + MEM_WIDTH] * qcscale).astype(BF16)
    kr0 = qc0 + MEM_WIDTH
    k_rope = _rope_rot_block(proj[:, kr0:kr0 + LANES], tabk_ref)
    q_all = _dot(c_q, wq_ref[...])
    k_all = _dot(c_kv, wk_ref[...])
    for hd in range(B_HEADS):
        hs = slice(B_HEAD_PAD * hd, B_HEAD_PAD * (hd + 1))
        q_ref[:, hs] = _rope_rot_block(q_all[:, hs], tabq_ref).astype(BF16)
        k_ref[:, hs] = (k_all[:, hs] + k_rope).astype(BF16)
    v_ref[...] = _dot_nt(wvt_ref[...], c_kv).astype(BF16)


def _proj_b(x, g, w_in, gq, wq, gkv, wk, wvt, tabq, tabk):
    b, s, d = x.shape
    tm = min(max(MLA_KB, s // MLA_NKV), s)

    def out(width):
        return (jax.ShapeDtypeStruct((b, s, width), BF16),
                pl.BlockSpec((None, tm, width), lambda i, j: (i, j, 0)))

    vt = (jax.ShapeDtypeStruct((b, s // tm, B_HEADS * V_HEAD, tm), BF16),
          pl.BlockSpec((None, None, B_HEADS * V_HEAD, tm), lambda i, j: (i, j, 0, 0)))
    shapes, specs = zip(out(B_HEADS * B_HEAD_PAD), out(B_HEADS * B_HEAD_PAD), vt, out(MEM_WIDTH))
    tab_spec = pl.BlockSpec((tm, 2 * LANES), lambda i, j: (j, 0))
    return pl.pallas_call(
        functools.partial(_projb_body, qcscale=HEAD_DIM ** -0.5 * LOG2E),
        grid=(b, s // tm),
        in_specs=[
            pl.BlockSpec((None, tm, d), lambda i, j: (i, j, 0)),
            _resident((1, d)),
            _resident(w_in.shape),
            _resident((1, Q_LORA)),
            _resident(wq.shape),
            _resident((1, KV_LORA)),
            _resident(wk.shape),
            _resident(wvt.shape),
            tab_spec, tab_spec,
        ],
        out_specs=list(specs),
        out_shape=list(shapes),
        compiler_params=_params(2),
        name="proj_b",
    )(x, g.reshape(1, d), w_in, gq.reshape(1, -1), wq, gkv.reshape(1, -1), wk, wvt, tabq, tabk)


def _mla_body(q_ref, k_ref, vt_ref, o_ref, *scratch, kb, nkv):
    per_slot = 3 * 2
    slots = [scratch[per_slot * i:per_slot * (i + 1)] for i in range(MLA_SLOTS)]
    s_refs = [sl[0:2] for sl in slots]
    p_refs = [sl[2:4] for sl in slots]
    alpha_refs = [sl[4:6] for sl in slots]
    m_refs, acc_refs = (scratch[per_slot * MLA_SLOTS + 2 * i:per_slot * MLA_SLOTS + 2 * (i + 1)]
                        for i in range(2))
    hslices = [slice(B_HEAD_PAD * hh, B_HEAD_PAD * (hh + 1)) for hh in range(2)]
    vslices = [slice(V_HEAD * hh, V_HEAD * (hh + 1)) for hh in range(2)]
    chunks = [slice(c, c + MLA_CHUNK) for c in range(0, kb, MLA_CHUNK)]

    def scores(t, slot):
        rows = pl.ds(pl.multiple_of(t * kb, kb), kb)
        for hh, hs in enumerate(hslices):
            s_refs[slot][hh][...] = _dot_nt(k_ref[rows, hs], q_ref[:, hs])

    def softmax(slot):
        for hh in range(2):
            s_ref, p_ref = s_refs[slot][hh], p_refs[slot][hh]
            cmax = s_ref[chunks[0], :]
            for ch in chunks[1:]:
                cmax = jnp.maximum(cmax, s_ref[ch, :])
            m = m_refs[hh][...]
            m_new = jnp.maximum(m, jnp.max(cmax, axis=0, keepdims=True))
            alpha = jnp.exp2(m - m_new)
            for ch in chunks:
                p_ref[ch, :] = jnp.exp2(s_ref[ch, :] - m_new).astype(BF16)
            alpha_refs[slot][hh][...] = alpha
            m_refs[hh][...] = m_new

    def accumulate(t, slot):
        ones = jnp.ones((MLA_DEN_ROWS, kb), BF16)
        for hh, vs in enumerate(vslices):
            v_aug = jnp.concatenate([vt_ref[t, vs, :], ones], axis=0)
            acc_refs[hh][...] = (alpha_refs[slot][hh][...] * acc_refs[hh][...]
                                 + _dot(v_aug, p_refs[slot][hh][...]))

    def stage(t, phase, ahead=True, behind=True):
        if ahead:
            scores(t + MLA_AHEAD, (phase + MLA_AHEAD) % MLA_SLOTS)
        if behind:
            accumulate(t - MLA_AHEAD, (phase - MLA_AHEAD) % MLA_SLOTS)
        softmax(phase)

    lo = MLA_AHEAD
    trips = max(nkv - 2 * MLA_AHEAD, 0) // MLA_SLOTS
    tail = range(lo + trips * MLA_SLOTS, nkv)
    assert all(t + MLA_AHEAD >= nkv for t in tail)

    def fill_and_loop():
        for hh in range(2):
            m_refs[hh][...] = jnp.full(m_refs[hh].shape, NEG, F32)
            acc_refs[hh][...] = jnp.zeros(acc_refs[hh].shape, F32)
        for t in range(min(MLA_AHEAD, nkv)):
            scores(t, t % MLA_SLOTS)
        for t in range(min(lo, nkv)):
            stage(t, t % MLA_SLOTS, ahead=t + MLA_AHEAD < nkv, behind=False)

        def full_stages(i, carry):
            for j in range(MLA_SLOTS):
                stage(lo + MLA_SLOTS * i + j, (lo + j) % MLA_SLOTS)
            return carry

        lax.fori_loop(0, trips, full_stages, 0)

    def drain():
        for t in tail:
            stage(t, t % MLA_SLOTS, ahead=False, behind=t >= MLA_AHEAD)
        for t in range(max(nkv - MLA_AHEAD, 0), nkv):
            accumulate(t, t % MLA_SLOTS)
        out_t = jnp.concatenate(
            [acc_refs[hh][:V_HEAD, :] / acc_refs[hh][V_HEAD:V_HEAD + 1, :] for hh in range(2)], axis=0)
        o_ref[...] = out_t.T.astype(BF16)

    n = pl.program_id(2)
    last = pl.num_programs(2) - 1

    @pl.when(n == 0)
    def _():
        fill_and_loop()

    @pl.when(jnp.logical_and(n > 0, n < last))
    def _():
        drain()
        fill_and_loop()

    @pl.when(n == last)
    def _():
        drain()


def _mla_attn(q, k, vt):
    b, s, _ = q.shape
    _, nkv, _, kb = vt.shape
    qb = min(MLA_QB, s)
    nq = s // qb
    npair = B_HEADS // 2
    per_slot = ([pltpu.VMEM((kb, qb), F32)] * 2
                + [pltpu.VMEM((kb, qb), BF16)] * 2
                + [pltpu.VMEM((1, qb), F32)] * 2)
    state = ([pltpu.VMEM((1, qb), F32)] * 2
             + [pltpu.VMEM((V_HEAD + MLA_DEN_ROWS, qb), F32)] * 2)
    return pl.pallas_call(
        functools.partial(_mla_body, kb=kb, nkv=nkv),
        grid=(b, npair, nq + 1),
        in_specs=[
            pl.BlockSpec((None, qb, 2 * B_HEAD_PAD), lambda i, j, n: (i, jnp.minimum(n, nq - 1), j)),
            pl.BlockSpec((None, s, 2 * B_HEAD_PAD), lambda i, j, n: (i, 0, j)),
            pl.BlockSpec((None, nkv, 2 * V_HEAD, kb), lambda i, j, n: (i, 0, j, 0)),
        ],
        out_specs=pl.BlockSpec((None, qb, 2 * V_HEAD), lambda i, j, n: (i, jnp.maximum(n - 1, 0), j)),
        out_shape=jax.ShapeDtypeStruct((b, s, B_HEADS * V_HEAD), BF16),
        scratch_shapes=per_slot * MLA_SLOTS + state,
        compiler_params=_params(3),
        name="mla_attn",
    )(q, k, vt)


def _oproj_body(x_ref, loc_ref, qc_ref, mk_ref, mvt_ref, wo_ref, o_ref):
    sub = min(OPROJ_SUB, qc_ref.shape[0])
    lane = lax.broadcasted_iota(jnp.int32, (sub, LANES), 1)
    ones = jnp.ones((MLA_DEN_ROWS, N_MEM), BF16)
    zero = jnp.zeros((sub, LANES), BF16)
    for r0 in range(0, qc_ref.shape[0], sub):
        rows = slice(r0, r0 + sub)
        tiles = []
        for tile in range(MEM_WIDTH // LANES):
            ts = slice(LANES * tile, LANES * (tile + 1))
            halves = []
            for half in range(2):
                hd = 2 * tile + half
                in_half = (lane < HEAD_DIM) if half == 0 else (lane >= HEAD_DIM)
                s = _dot_nt(mk_ref[:, ts], jnp.where(in_half, qc_ref[rows, ts], zero))
                p = jnp.exp2(s - jnp.max(s, axis=0, keepdims=True)).astype(BF16)
                v_aug = jnp.concatenate([mvt_ref[HEAD_DIM * hd:HEAD_DIM * (hd + 1), :], ones], axis=0)
                o_aug = _dot(v_aug, p)
                halves.append(o_aug[:HEAD_DIM, :] / o_aug[HEAD_DIM:HEAD_DIM + 1, :])
            tiles.append(jnp.concatenate(halves, axis=0).T.astype(BF16))
        cross = jnp.concatenate(tiles, axis=-1)
        y = _dot(loc_ref[rows, :], wo_ref[:LOCAL_W, :]) + _dot(cross, wo_ref[LOCAL_W:, :])
        o_ref[rows, :] = x_ref[rows, :] + y


def _out_proj(x, local, qc, mk, mvt, wo):
    b, s, d = x.shape
    tm = min(2 * OPROJ_SUB, s)

    def tok(width):
        return pl.BlockSpec((None, tm, width), lambda i, j: (i, j, 0))

    return pl.pallas_call(
        _oproj_body,
        grid=(b, s // tm),
        in_specs=[tok(d), tok(LOCAL_W), tok(MEM_WIDTH),
                  pl.BlockSpec((None, N_MEM, MEM_WIDTH), lambda i, j: (i, 0, 0)),
                  pl.BlockSpec((None, MEM_WIDTH, N_MEM), lambda i, j: (i, 0, 0)),
                  _resident(wo.shape)],
        out_specs=tok(d),
        out_shape=jax.ShapeDtypeStruct((b, s, d), F32),
        compiler_params=_params(2),
        name="out_proj",
    )(x, local, qc, mk, mvt, wo)


def _prep_a(a_w_in):
    nl, d, _ = a_w_in.shape
    order = jnp.array(A_HEAD_ORDER)
    q = a_w_in[:, :, :A_Q_W].reshape(nl, d, A_Q_HEADS, HEAD_DIM)[:, :, order].reshape(nl, d, A_Q_W)
    k = a_w_in[:, :, A_Q_W:A_Q_W + A_KV_W]
    v = a_w_in[:, :, A_Q_W + A_KV_W:A_Q_W + 2 * A_KV_W]
    qc = a_w_in[:, :, A_Q_W + 2 * A_KV_W:]
    w = jnp.concatenate([q, k, qc], axis=-1).astype(BF16)
    return w, jnp.transpose(v, (0, 2, 1)).astype(BF16)


def _prep_w_o(w_o):
    depth, _, d = w_o.shape
    order = jnp.array(A_HEAD_ORDER)
    local = w_o[:, :LOCAL_W].reshape(depth, A_Q_HEADS, HEAD_DIM, d)
    local = jnp.where((jnp.arange(depth) % 2 == 0)[:, None, None, None], local[:, order], local)
    return jnp.concatenate([local.reshape(depth, LOCAL_W, d), w_o[:, LOCAL_W:]], axis=1).astype(BF16)


def _prep_b(b_w_in, b_w_q_up, b_w_kv_up):
    nl, d, _ = b_w_in.shape
    c_q = b_w_in[:, :, :Q_LORA]
    c_kv = b_w_in[:, :, Q_LORA:Q_LORA + KV_LORA]
    k_r = b_w_in[:, :, Q_LORA + KV_LORA:Q_LORA + KV_LORA + QK_ROPE]
    qc = b_w_in[:, :, Q_LORA + KV_LORA + QK_ROPE:]
    kr_tile = jnp.pad(_with_rotate_half(k_r), ((0, 0), (0, 0), (QK_NOPE, 0)))
    w_in = jnp.concatenate([c_q, c_kv, qc, kr_tile], axis=-1).astype(BF16)
    wq = b_w_q_up.reshape(nl, Q_LORA, B_HEADS, B_QK)
    wq = jnp.concatenate([wq[..., :QK_NOPE], _with_rotate_half(wq[..., QK_NOPE:])], axis=-1)
    wq = wq.reshape(nl, Q_LORA, B_HEADS * B_HEAD_PAD).astype(BF16)
    wkv = b_w_kv_up.reshape(nl, KV_LORA, B_HEADS, QK_NOPE + V_HEAD)
    wk = jnp.pad(wkv[..., :QK_NOPE], ((0, 0), (0, 0), (0, 0), (0, B_HEAD_PAD - QK_NOPE)))
    wk = wk.reshape(nl, KV_LORA, B_HEADS * B_HEAD_PAD).astype(BF16)
    wvt = jnp.transpose(wkv[..., QK_NOPE:].reshape(nl, KV_LORA, B_HEADS * V_HEAD), (0, 2, 1)).astype(BF16)
    return w_in, wq, wk, wvt


def _trunk(x, mem, w):
    b, s, d = x.shape
    depth = w["mix_norm"].shape[0]
    tab_a = _rope_table(s, HEAD_DIM, 0, HEAD_DIM, 1.0, False)
    tab_bq = _rope_rot_table(s, B_QK ** -0.5 * LOG2E, True)
    tab_bk = _rope_rot_table(s, 1.0, False)
    for i in range(depth):
        x = _ffn(x.reshape(b * s, d), w["ffn1_norm"][i], w["ffn1_w_gu"], w["ffn1_w_down"], i).reshape(b, s, d)
        mk, mvt = _mem_kv(mem, w["mem_norm"][i], w["w_mem_k"][i], w["w_mem_vt"][i])
        j = i // 2
        if i % 2 == 0:
            q, k, vt, qc = _proj_a(x, w["mix_norm"][i], w["a_w_in"][j], w["a_wvt"][j], tab_a)
            local = _win_attn(q, k, vt, w["a_sink"][j])
        else:
            q, k, v, qc = _proj_b(x, w["mix_norm"][i], w["b_w_in"][j], w["b_q_norm"][j], w["b_wq"][j],
                                  w["b_kv_norm"][j], w["b_wk"][j], w["b_wvt"][j], tab_bq, tab_bk)
            local = _mla_attn(q, k, v)
        x = _out_proj(x, local, qc, mk, mvt, w["w_o"][i])
        g_final = w["final_norm"] if i == depth - 1 else None
        x = _ffn(x.reshape(b * s, d), w["ffn2_norm"][i], w["ffn2_w_gu"], w["ffn2_w_down"], i,
                 g_final).reshape(b, s, d)
    return x


def kernel(x_prompt, x_sample, mem_prompt, mem_sample, ffn1_norm, ffn1_w_gu, ffn1_w_down, mix_norm,
           mem_norm, w_mem_kv, a_w_in, a_sink, b_w_in, b_q_norm, b_w_q_up, b_kv_norm, b_w_kv_up,
           w_o, ffn2_norm, ffn2_w_gu, ffn2_w_down, final_norm):
    w = {
        "ffn1_norm": ffn1_norm, "ffn2_norm": ffn2_norm, "mix_norm": mix_norm, "mem_norm": mem_norm,
        "w_mem_k": w_mem_kv[:, :, :MEM_WIDTH].astype(BF16),
        "w_mem_vt": jnp.transpose(w_mem_kv[:, :, MEM_WIDTH:], (0, 2, 1)).astype(BF16), "a_sink": a_sink,
        "b_q_norm": b_q_norm, "b_kv_norm": b_kv_norm, "w_o": _prep_w_o(w_o),
        "final_norm": final_norm,
    }
    w["ffn1_w_gu"], w["ffn1_w_down"] = ffn1_w_gu.astype(BF16), ffn1_w_down.astype(BF16)
    w["ffn2_w_gu"], w["ffn2_w_down"] = ffn2_w_gu.astype(BF16), ffn2_w_down.astype(BF16)
    w["a_w_in"], w["a_wvt"] = _prep_a(a_w_in)
    w["b_w_in"], w["b_wq"], w["b_wk"], w["b_wvt"] = _prep_b(b_w_in, b_w_q_up, b_w_kv_up)
    return (_trunk(x_prompt, mem_prompt, w), _trunk(x_sample, mem_sample, w))
```

```python
import functools
import math

import jax
import jax.numpy as jnp
from jax import lax
from jax.experimental import pallas as pl
from jax.experimental.pallas import tpu as pltpu

D_MODEL = 1024
HEAD_DIM = 64
ROPE_THETA = 10000.0
NORM_EPS = 1e-6
D_FF = 2816
N_MEM = 256
MEM_HEADS = 4
MEM_WIDTH = MEM_HEADS * HEAD_DIM
A_Q_HEADS = 12
A_KV_HEADS = 4
A_GROUP = A_Q_HEADS // A_KV_HEADS
WINDOW = 128
A_Q_W = A_Q_HEADS * HEAD_DIM
A_KV_W = A_KV_HEADS * HEAD_DIM
B_HEADS = 12
Q_LORA = 384
KV_LORA = 256
QK_NOPE = 64
QK_ROPE = 32
V_HEAD = 64
B_QK = QK_NOPE + QK_ROPE
LOCAL_W = A_Q_W
NEG = -1e30
LOG2E = math.log2(math.e)

LANES = 128
B_HEAD_PAD = LANES
FF_CHUNK = 256
OPROJ_SUB = 512
OPROJ_AHEAD = 3
MLA_QB = 512
MLA_KB = 512
MLA_NKV = 8
MLA_CHUNK = 32
MLA_DEN_ROWS = 16
MLA_AHEAD = 2
MLA_SLOTS = 2 * MLA_AHEAD
VMEM_LIMIT = 56 * 1024 * 1024

F32 = jnp.float32
BF16 = jnp.bfloat16


def _params(n_axes):
    return pltpu.CompilerParams(dimension_semantics=("arbitrary",) * n_axes,
                                vmem_limit_bytes=VMEM_LIMIT)


def _rms(x, g):
    return x * lax.rsqrt(jnp.mean(x * x, axis=-1, keepdims=True) + NORM_EPS) * g


def _dot(a, b):
    return jnp.dot(a, b, preferred_element_type=F32)


def _dot_nt(a, b):
    return lax.dot_general(a, b, (((1,), (1,)), ((), ())), preferred_element_type=F32)


def _resident(shape):
    zeros = (0,) * len(shape)
    return pl.BlockSpec(shape, lambda *_: zeros, pipeline_mode=pl.Buffered(1))


def _ffn_body(*refs, final):
    if final:
        x_ref, g_ref, wgu_ref, wd_ref, gf_ref, o_ref, h_ref = refs
    else:
        x_ref, g_ref, wgu_ref, wd_ref, o_ref, h_ref = refs
    h_ref[...] = _rms(x_ref[...], g_ref[...]).astype(BF16)
    nch = D_FF // FF_CHUNK

    def gate_up(c):
        h = h_ref[...]
        gate = _dot(h, wgu_ref[:, FF_CHUNK * c:FF_CHUNK * (c + 1)])
        up = _dot(h, wgu_ref[:, D_FF + FF_CHUNK * c:D_FF + FF_CHUNK * (c + 1)])
        return gate, up

    acc = None
    nxt = gate_up(0)
    for c in range(nch):
        gate, up = nxt
        if c + 1 < nch:
            nxt = gate_up(c + 1)
        a = (gate / (1.0 + jnp.exp(-gate)) * up).astype(BF16)
        down = _dot(a, wd_ref[FF_CHUNK * c:FF_CHUNK * (c + 1), :])
        acc = down if acc is None else acc + down
    y = x_ref[...] + 0.5 * acc
    if final:
        y = _rms(y, gf_ref[...])
    o_ref[...] = y


def _layer_resident(stacked, layer):
    zeros = (0,) * (stacked.ndim - 1)
    return pl.BlockSpec((None,) + stacked.shape[1:], lambda *_: (layer,) + zeros,
                        pipeline_mode=pl.Buffered(1))


def _ffn(x, g, w_gu, w_down, layer, g_final=None):
    t, d = x.shape
    tm = min(1024, t)
    final = g_final is not None
    in_specs = [
        pl.BlockSpec((tm, d), lambda i: (i, 0)),
        _resident((1, d)),
        _layer_resident(w_gu, layer),
        _layer_resident(w_down, layer),
    ]
    args = [x, g.reshape(1, d), w_gu, w_down]
    if final:
        in_specs.append(_resident((1, d)))
        args.append(g_final.reshape(1, d))
    return pl.pallas_call(
        functools.partial(_ffn_body, final=final),
        grid=(t // tm,),
        in_specs=in_specs,
        out_specs=pl.BlockSpec((tm, d), lambda i: (i, 0)),
        out_shape=jax.ShapeDtypeStruct((t, d), F32),
        scratch_shapes=[pltpu.VMEM((tm, d), BF16)],
        compiler_params=_params(1),
        name="ffn_final" if final else "ffn",
    )(*args)


def _memkv_body(mem_ref, g_ref, wk_ref, wvt_ref, mk_ref, mvt_ref):
    h = _rms(mem_ref[...], g_ref[...]).astype(BF16)
    mk_ref[...] = _dot(h, wk_ref[...]).astype(BF16)
    mvt_ref[...] = _dot_nt(wvt_ref[...], h).astype(BF16)


def _mem_kv(mem, g, wk, wvt):
    b, n, d = mem.shape
    return pl.pallas_call(
        _memkv_body,
        grid=(b,),
        in_specs=[pl.BlockSpec((None, n, d), lambda i: (i, 0, 0)), _resident((1, d)),
                  _resident(wk.shape), _resident(wvt.shape)],
        out_specs=[pl.BlockSpec((None, n, MEM_WIDTH), lambda i: (i, 0, 0)),
                   pl.BlockSpec((None, MEM_WIDTH, n), lambda i: (i, 0, 0))],
        out_shape=[jax.ShapeDtypeStruct((b, n, MEM_WIDTH), BF16),
                   jax.ShapeDtypeStruct((b, MEM_WIDTH, n), BF16)],
        compiler_params=_params(1),
        name="mem_kv",
    )(mem, g.reshape(1, d), wk, wvt)


def _rope_block(xb, tab_ref, shift):
    c = tab_ref[:, 0:LANES]
    s_plus = tab_ref[:, LANES:2 * LANES]
    s_minus = tab_ref[:, 2 * LANES:3 * LANES]
    return (xb * c + pltpu.roll(xb, shift, 1) * s_plus
            + pltpu.roll(xb, LANES - shift, 1) * s_minus)


def _rope_table(seq, dim, lane_start, period, scale, pass_through):
    half = dim // 2
    inv = 1.0 / (ROPE_THETA ** (jnp.arange(0, dim, 2, dtype=F32) / dim))
    ang = jnp.arange(seq, dtype=F32)[:, None] * inv[None, :]
    cos, sin = jnp.cos(ang), jnp.sin(ang)
    zeros_h = jnp.zeros((seq, half), F32)
    lead = jnp.full((seq, lane_start), 1.0 if pass_through else 0.0, F32)
    lead0 = jnp.zeros((seq, lane_start), F32)
    tail0 = jnp.zeros((seq, period - lane_start - dim), F32)
    reps = LANES // period
    c = jnp.tile(jnp.concatenate([lead, cos, cos, tail0], -1), (1, reps))
    s_plus = jnp.tile(jnp.concatenate([lead0, zeros_h, sin, tail0], -1), (1, reps))
    s_minus = jnp.tile(jnp.concatenate([lead0, -sin, zeros_h, tail0], -1), (1, reps))
    return jnp.concatenate([c, s_plus, s_minus], -1) * scale


def _rope_rot_block(xb, tab_ref):
    c = tab_ref[:, 0:LANES]
    s = tab_ref[:, LANES:2 * LANES]
    return xb * c + pltpu.roll(xb, LANES - QK_ROPE, 1) * s


def _rope_rot_table(seq, scale, pass_through):
    inv = 1.0 / (ROPE_THETA ** (jnp.arange(0, QK_ROPE, 2, dtype=F32) / QK_ROPE))
    ang = jnp.arange(seq, dtype=F32)[:, None] * inv[None, :]
    cos, sin = jnp.cos(ang), jnp.sin(ang)
    lead = jnp.full((seq, QK_NOPE), 1.0 if pass_through else 0.0, F32)
    lead0 = jnp.zeros((seq, QK_NOPE), F32)
    tail0 = jnp.zeros((seq, LANES - QK_NOPE - QK_ROPE), F32)
    c = jnp.concatenate([lead, cos, cos, tail0], -1)
    s = jnp.concatenate([lead0, sin, sin, tail0], -1)
    return jnp.concatenate([c, s], -1) * scale


def _with_rotate_half(w_rope):
    x1, x2 = w_rope[..., :QK_ROPE // 2], w_rope[..., QK_ROPE // 2:]
    return jnp.concatenate([w_rope, -x2, x1], axis=-1)


A_HEAD_ORDER = (0, 3, 1, 4, 2, 5, 6, 9, 7, 10, 8, 11)
A_KV_TILES = A_KV_W // LANES
A_TILE_HEADS = A_GROUP
WIN_QBLOCKS = 4
WIN_AHEAD = 3


def _proja_body(x_ref, g_ref, w_ref, wvt_ref, tab_ref, q_ref, k_ref, vt_ref, qc_ref, *, qscale):
    h = _rms(x_ref[...], g_ref[...]).astype(BF16)
    proj = _dot(h, w_ref[...])
    nq = A_Q_W // LANES
    nk = A_KV_W // LANES
    for j in range(nq + nk):
        rb = _rope_block(proj[:, LANES * j:LANES * (j + 1)], tab_ref, HEAD_DIM // 2)
        if j < nq:
            q_ref[:, LANES * j:LANES * (j + 1)] = (rb * qscale).astype(BF16)
        else:
            k_ref[:, LANES * (j - nq):LANES * (j - nq + 1)] = rb.astype(BF16)
    qc_ref[...] = (proj[:, A_Q_W + A_KV_W:] * qscale).astype(BF16)
    vt = _dot_nt(wvt_ref[...], h).astype(BF16)
    for i in range(vt_ref.shape[0]):
        vt_ref[i] = vt[:, WINDOW * i:WINDOW * (i + 1)]


def _proj_a(x, g, w, wvt, tab):
    b, s, d = x.shape
    tm = min(512, s)
    nblk = tm // WINDOW

    def out(width):
        return (jax.ShapeDtypeStruct((b, s, width), BF16),
                pl.BlockSpec((None, tm, width), lambda i, j: (i, j, 0)))

    vt = (jax.ShapeDtypeStruct((b, s // WINDOW, A_KV_W, WINDOW), BF16),
          pl.BlockSpec((None, nblk, A_KV_W, WINDOW), lambda i, j: (i, j, 0, 0)))
    shapes, specs = zip(out(A_Q_W), out(A_KV_W), vt, out(MEM_WIDTH))
    return pl.pallas_call(
        functools.partial(_proja_body, qscale=HEAD_DIM ** -0.5 * LOG2E),
        grid=(b, s // tm),
        in_specs=[
            pl.BlockSpec((None, tm, d), lambda i, j: (i, j, 0)),
            _resident((1, d)),
            _resident(w.shape),
            _resident(wvt.shape),
            pl.BlockSpec((tm, 3 * LANES), lambda i, j: (j, 0)),
        ],
        out_specs=list(specs),
        out_shape=list(shapes),
        compiler_params=_params(2),
        name="proj_a",
    )(x, g.reshape(1, d), w, wvt, tab)


def _wattn_body(bias_ref, sink_ref, q_ref, kp_ref, kc_ref, kn_ref, vp_ref, vc_ref, vn_ref, o_ref, *, nsteps):
    step = pl.program_id(1)
    lane = lax.broadcasted_iota(jnp.int32, (WINDOW, LANES), 1)
    ones = jnp.ones((MLA_DEN_ROWS, 3 * WINDOW), BF16)
    zero = jnp.zeros((WINDOW, LANES), BF16)
    def operands(blk, tile):
        ts = slice(LANES * tile, LANES * (tile + 1))
        k_blocks = ([kp_ref[:, ts]] + [kc_ref[WINDOW * i:WINDOW * (i + 1), ts] for i in range(WIN_QBLOCKS)]
                    + [kn_ref[:, ts]])
        vt_blocks = [vp_ref[ts, :]] + [vc_ref[i, ts, :] for i in range(WIN_QBLOCKS)] + [vn_ref[ts, :]]
        k_band = jnp.concatenate(k_blocks[blk:blk + 3], axis=0)
        vt_band = jnp.concatenate(vt_blocks[blk:blk + 3], axis=1)
        return k_band, vt_band

    def scores(blk, tile, half):
        variant = 1
        if blk == 0:
            variant = jnp.where(step == 0, 0, variant)
        if blk == WIN_QBLOCKS - 1:
            variant = jnp.where(step == nsteps - 1, 2, variant)
        rows = slice(WINDOW * blk, WINDOW * (blk + 1))
        q_tiles = [q_ref[rows, LANES * (A_TILE_HEADS * tile + r):LANES * (A_TILE_HEADS * tile + r + 1)]
                   for r in range(A_TILE_HEADS)]
        in_half = (lane < HEAD_DIM) if half == 0 else (lane >= HEAD_DIM)
        q_stack = jnp.concatenate([jnp.where(in_half, qt, zero) for qt in q_tiles], axis=0)
        return _dot_nt(operands(blk, tile)[0], q_stack) + bias_ref[variant]

    def attend(blk, tile, half, s):
        sink = sink_ref[2 * tile + half]
        m = jnp.maximum(jnp.max(s, axis=0, keepdims=True), sink)
        p = jnp.exp2(s - m).astype(BF16)
        vt_band = operands(blk, tile)[1]
        v_aug = jnp.concatenate([vt_band[HEAD_DIM * half:HEAD_DIM * (half + 1), :], ones], axis=0)
        o_aug = _dot(v_aug, p)
        den = o_aug[HEAD_DIM:HEAD_DIM + 1, :] + jnp.exp2(sink - m)
        return o_aug[:HEAD_DIM, :] / den

    def emit(blk, tile, halves):
        rows = slice(WINDOW * blk, WINDOW * (blk + 1))
        for r in range(A_TILE_HEADS):
            cs = slice(WINDOW * r, WINDOW * (r + 1))
            out_t = jnp.concatenate([halves[0][:, cs], halves[1][:, cs]], axis=0)
            j = A_TILE_HEADS * tile + r
            o_ref[rows, LANES * j:LANES * (j + 1)] = out_t.T.astype(BF16)

    chains = [(blk, tile, half) for blk in range(WIN_QBLOCKS) for tile in range(A_KV_TILES) for half in range(2)]
    pending, done = {}, {}
    for i in range(len(chains) + WIN_AHEAD):
        if i < len(chains):
            pending[chains[i]] = scores(*chains[i])
        if i >= WIN_AHEAD:
            blk, tile, half = chains[i - WIN_AHEAD]
            done[(blk, tile, half)] = attend(blk, tile, half, pending.pop((blk, tile, half)))
            if half == 1:
                emit(blk, tile, [done.pop((blk, tile, 0)), done.pop((blk, tile, 1))])


def _band_bias(dtype=F32):
    kj = jnp.arange(3 * WINDOW)[:, None]
    qi = (jnp.arange(3 * WINDOW) % WINDOW)[None, :]
    band = (kj - qi >= 0) & (kj - qi <= 2 * WINDOW)
    first = band & (kj >= WINDOW)
    last = band & (kj < 2 * WINDOW)
    return jnp.where(jnp.stack([first, band, last]), 0.0, NEG).astype(dtype)


def _win_attn(q, k, vt, sink):
    b, s, _ = q.shape
    nb = s // WINDOW
    nsteps = nb // WIN_QBLOCKS
    assert nb >= 2 and nb % WIN_QBLOCKS == 0
    qrows = WIN_QBLOCKS * WINDOW
    sink_rows = jnp.repeat(sink.reshape(A_KV_HEADS, 1, A_GROUP) * LOG2E, WINDOW, axis=-1)

    def halo(j, shift):
        return jnp.clip(WIN_QBLOCKS * j + shift, 0, nb - 1)

    def k_halo(shift):
        return pl.BlockSpec((None, WINDOW, A_KV_W), lambda i, j: (i, halo(j, shift), 0))

    def vt_halo(shift):
        return pl.BlockSpec((None, None, A_KV_W, WINDOW), lambda i, j: (i, halo(j, shift), 0, 0))

    return pl.pallas_call(
        functools.partial(_wattn_body, nsteps=nsteps),
        grid=(b, nsteps),
        in_specs=[
            _resident((3, 3 * WINDOW, 3 * WINDOW)),
            _resident((A_KV_HEADS, 1, 3 * WINDOW)),
            pl.BlockSpec((None, qrows, A_Q_W), lambda i, j: (i, j, 0)),
            k_halo(-1),
            pl.BlockSpec((None, qrows, A_KV_W), lambda i, j: (i, j, 0)),
            k_halo(WIN_QBLOCKS),
            vt_halo(-1),
            pl.BlockSpec((None, WIN_QBLOCKS, A_KV_W, WINDOW), lambda i, j: (i, j, 0, 0)),
            vt_halo(WIN_QBLOCKS),
        ],
        out_specs=pl.BlockSpec((None, qrows, A_Q_W), lambda i, j: (i, j, 0)),
        out_shape=jax.ShapeDtypeStruct((b, s, A_Q_W), BF16),
        compiler_params=_params(2),
        name="win_attn",
    )(_band_bias(), sink_rows, q, k, k, k, vt, vt, vt)


def _projb_body(x_ref, g_ref, win_ref, gq_ref, wq_ref, gkv_ref, wk_ref, wvt_ref, tabq_ref, tabk_ref,
                q_ref, k_ref, v_ref, qc_ref, *, qcscale):
    h = _rms(x_ref[...], g_ref[...]).astype(BF16)
    proj = _dot(h, win_ref[...])
    c_q = _rms(proj[:, :Q_LORA], gq_ref[...]).astype(BF16)
    kv0 = Q_LORA
    c_kv = _rms(proj[:, kv0:kv0 + KV_LORA], gkv_ref[...]).astype(BF16)
    qc0 = kv0 + KV_LORA
    qc_ref[...] = (proj[:, qc0:qc0 + MEM_WIDTH] * qcscale).astype(BF16)
    kr0 = qc0 + MEM_WIDTH
    k_rope = _rope_rot_block(proj[:, kr0:kr0 + LANES], tabk_ref)
    q_all = _dot(c_q, wq_ref[...])
    k_all = _dot(c_kv, wk_ref[...])
    for hd in range(B_HEADS):
        hs = slice(B_HEAD_PAD * hd, B_HEAD_PAD * (hd + 1))
        q_ref[:, hs] = _rope_rot_block(q_all[:, hs], tabq_ref).astype(BF16)
        k_ref[:, hs] = (k_all[:, hs] + k_rope).astype(BF16)
    v_ref[...] = _dot_nt(wvt_ref[...], c_kv).astype(BF16)


def _proj_b(x, g, w_in, gq, wq, gkv, wk, wvt, tabq, tabk):
    b, s, d = x.shape
    tm = min(max(MLA_KB, s // MLA_NKV), s)

    def out(width):
        return (jax.ShapeDtypeStruct((b, s, width), BF16),
                pl.BlockSpec((None, tm, width), lambda i, j: (i, j, 0)))

    vt = (jax.ShapeDtypeStruct((b, s // tm, B_HEADS * V_HEAD, tm), BF16),
          pl.BlockSpec((None, None, B_HEADS * V_HEAD, tm), lambda i, j: (i, j, 0, 0)))
    shapes, specs = zip(out(B_HEADS * B_HEAD_PAD), out(B_HEADS * B_HEAD_PAD), vt, out(MEM_WIDTH))
    tab_spec = pl.BlockSpec((tm, 2 * LANES), lambda i, j: (j, 0))
    return pl.pallas_call(
        functools.partial(_projb_body, qcscale=HEAD_DIM ** -0.5 * LOG2E),
        grid=(b, s // tm),
        in_specs=[
            pl.BlockSpec((None, tm, d), lambda i, j: (i, j, 0)),
            _resident((1, d)),
            _resident(w_in.shape),
            _resident((1, Q_LORA)),
            _resident(wq.shape),
            _resident((1, KV_LORA)),
            _resident(wk.shape),
            _resident(wvt.shape),
            tab_spec, tab_spec,
        ],
        out_specs=list(specs),
        out_shape=list(shapes),
        compiler_params=_params(2),
        name="proj_b",
    )(x, g.reshape(1, d), w_in, gq.reshape(1, -1), wq, gkv.reshape(1, -1), wk, wvt, tabq, tabk)


def _mla_body(q_ref, k_ref, vt_ref, o_ref, *scratch, kb, nkv):
    per_slot = 3 * 2
    slots = [scratch[per_slot * i:per_slot * (i + 1)] for i in range(MLA_SLOTS)]
    s_refs = [sl[0:2] for sl in slots]
    p_refs = [sl[2:4] for sl in slots]
    alpha_refs = [sl[4:6] for sl in slots]
    m_refs, acc_refs = (scratch[per_slot * MLA_SLOTS + 2 * i:per_slot * MLA_SLOTS + 2 * (i + 1)]
                        for i in range(2))
    hslices = [slice(B_HEAD_PAD * hh, B_HEAD_PAD * (hh + 1)) for hh in range(2)]
    vslices = [slice(V_HEAD * hh, V_HEAD * (hh + 1)) for hh in range(2)]
    chunks = [slice(c, c + MLA_CHUNK) for c in range(0, kb, MLA_CHUNK)]

    def scores(t, slot):
        rows = pl.ds(pl.multiple_of(t * kb, kb), kb)
        for hh, hs in enumerate(hslices):
            s_refs[slot][hh][...] = _dot_nt(k_ref[rows, hs], q_ref[:, hs])

    def softmax(slot):
        for hh in range(2):
            s_ref, p_ref = s_refs[slot][hh], p_refs[slot][hh]
            cmax = s_ref[chunks[0], :]
            for ch in chunks[1:]:
                cmax = jnp.maximum(cmax, s_ref[ch, :])
            m = m_refs[hh][...]
            m_new = jnp.maximum(m, jnp.max(cmax, axis=0, keepdims=True))
            alpha = jnp.exp2(m - m_new)
            for ch in chunks:
                p_ref[ch, :] = jnp.exp2(s_ref[ch, :] - m_new).astype(BF16)
            alpha_refs[slot][hh][...] = alpha
            m_refs[hh][...] = m_new

    def accumulate(t, slot):
        ones = jnp.ones((MLA_DEN_ROWS, kb), BF16)
        for hh, vs in enumerate(vslices):
            v_aug = jnp.concatenate([vt_ref[t, vs, :], ones], axis=0)
            acc_refs[hh][...] = (alpha_refs[slot][hh][...] * acc_refs[hh][...]
                                 + _dot(v_aug, p_refs[slot][hh][...]))

    def stage(t, phase, ahead=True, behind=True):
        if ahead:
            scores(t + MLA_AHEAD, (phase + MLA_AHEAD) % MLA_SLOTS)
        if behind:
            accumulate(t - MLA_AHEAD, (phase - MLA_AHEAD) % MLA_SLOTS)
        softmax(phase)

    lo = MLA_AHEAD
    trips = max(nkv - 2 * MLA_AHEAD, 0) // MLA_SLOTS
    tail = range(lo + trips * MLA_SLOTS, nkv)
    assert all(t + MLA_AHEAD >= nkv for t in tail)

    def init_m():
        for hh in range(2):
            m_refs[hh][...] = jnp.full(m_refs[hh].shape, NEG, F32)

    def init_acc():
        for hh in range(2):
            acc_refs[hh][...] = jnp.zeros(acc_refs[hh].shape, F32)

    def output():
        out_t = jnp.concatenate(
            [acc_refs[hh][:V_HEAD, :] / acc_refs[hh][V_HEAD:V_HEAD + 1, :] for hh in range(2)], axis=0)
        o_ref[...] = out_t.T.astype(BF16)

    def loop():
        def full_stages(i, carry):
            for j in range(MLA_SLOTS):
                stage(lo + MLA_SLOTS * i + j, (lo + j) % MLA_SLOTS)
            return carry

        lax.fori_loop(0, trips, full_stages, 0)

    fill_scores = [functools.partial(scores, t, t % MLA_SLOTS) for t in range(min(MLA_AHEAD, nkv))]
    fill_stages = [functools.partial(stage, t, t % MLA_SLOTS, ahead=t + MLA_AHEAD < nkv, behind=False)
                   for t in range(min(lo, nkv))]
    drain_stages = [functools.partial(stage, t, t % MLA_SLOTS, ahead=False, behind=t >= MLA_AHEAD) for t in tail]
    drain_accs = [functools.partial(accumulate, t, t % MLA_SLOTS) for t in range(max(nkv - MLA_AHEAD, 0), nkv)]

    def run(pieces):
        for piece in pieces:
            piece()

    n = pl.program_id(2)
    last = pl.num_programs(2) - 1

    def fill_and_loop():
        run(fill_scores)
        init_m()
        init_acc()
        run(fill_stages)
        loop()

    def drain():
        run(drain_stages)
        run(drain_accs)
        output()

    @pl.when(n == 0)
    def _():
        fill_and_loop()

    @pl.when(jnp.logical_and(n > 0, n < last))
    def _():
        drain()
        fill_and_loop()

    @pl.when(n == last)
    def _():
        drain()


def _mla_attn(q, k, vt):
    b, s, _ = q.shape
    _, nkv, _, kb = vt.shape
    qb = min(MLA_QB, s)
    nq = s // qb
    npair = B_HEADS // 2
    per_slot = ([pltpu.VMEM((kb, qb), F32)] * 2
                + [pltpu.VMEM((kb, qb), BF16)] * 2
                + [pltpu.VMEM((1, qb), F32)] * 2)
    state = ([pltpu.VMEM((1, qb), F32)] * 2
             + [pltpu.VMEM((V_HEAD + MLA_DEN_ROWS, qb), F32)] * 2)
    return pl.pallas_call(
        functools.partial(_mla_body, kb=kb, nkv=nkv),
        grid=(b, npair, nq + 1),
        in_specs=[
            pl.BlockSpec((None, qb, 2 * B_HEAD_PAD), lambda i, j, n: (i, jnp.minimum(n, nq - 1), j)),
            pl.BlockSpec((None, s, 2 * B_HEAD_PAD), lambda i, j, n: (i, 0, j)),
            pl.BlockSpec((None, nkv, 2 * V_HEAD, kb), lambda i, j, n: (i, 0, j, 0)),
        ],
        out_specs=pl.BlockSpec((None, qb, 2 * V_HEAD), lambda i, j, n: (i, jnp.maximum(n - 1, 0), j)),
        out_shape=jax.ShapeDtypeStruct((b, s, B_HEADS * V_HEAD), BF16),
        scratch_shapes=per_slot * MLA_SLOTS + state,
        compiler_params=_params(3),
        name="mla_attn",
    )(q, k, vt)


def _oproj_body(x_ref, loc_ref, qc_ref, mk_ref, mvt_ref, wo_ref, o_ref):
    sub = min(OPROJ_SUB, qc_ref.shape[0])
    lane = lax.broadcasted_iota(jnp.int32, (sub, LANES), 1)
    ones = jnp.ones((MLA_DEN_ROWS, N_MEM), BF16)
    zero = jnp.zeros((sub, LANES), BF16)
    subs = [slice(r0, r0 + sub) for r0 in range(0, qc_ref.shape[0], sub)]

    def scores(rows, hd):
        ts = slice(LANES * (hd // 2), LANES * (hd // 2 + 1))
        in_half = (lane < HEAD_DIM) if hd % 2 == 0 else (lane >= HEAD_DIM)
        return _dot_nt(mk_ref[:, ts], jnp.where(in_half, qc_ref[rows, ts], zero))

    def attend(hd, s):
        p = jnp.exp2(s - jnp.max(s, axis=0, keepdims=True)).astype(BF16)
        v_aug = jnp.concatenate([mvt_ref[HEAD_DIM * hd:HEAD_DIM * (hd + 1), :], ones], axis=0)
        o_aug = _dot(v_aug, p)
        return o_aug[:HEAD_DIM, :] / o_aug[HEAD_DIM:HEAD_DIM + 1, :]

    y_local = [_dot(loc_ref[rows, :], wo_ref[:LOCAL_W, :]) for rows in subs]
    chains = [(i, hd) for i in range(len(subs)) for hd in range(MEM_HEADS)]
    pending, heads = {}, {}
    for c in range(len(chains) + OPROJ_AHEAD):
        if c < len(chains):
            i, hd = chains[c]
            pending[(i, hd)] = scores(subs[i], hd)
        if c >= OPROJ_AHEAD:
            i, hd = chains[c - OPROJ_AHEAD]
            heads[(i, hd)] = attend(hd, pending.pop((i, hd)))
            if hd == MEM_HEADS - 1:
                tiles = [jnp.concatenate([heads.pop((i, 2 * t)), heads.pop((i, 2 * t + 1))], axis=0).T.astype(BF16)
                         for t in range(MEM_HEADS // 2)]
                cross = jnp.concatenate(tiles, axis=-1)
                o_ref[subs[i], :] = x_ref[subs[i], :] + (y_local[i] + _dot(cross, wo_ref[LOCAL_W:, :]))


def _out_proj(x, local, qc, mk, mvt, wo):
    b, s, d = x.shape
    tm = min(2 * OPROJ_SUB, s)

    def tok(width):
        return pl.BlockSpec((None, tm, width), lambda i, j: (i, j, 0))

    return pl.pallas_call(
        _oproj_body,
        grid=(b, s // tm),
        in_specs=[tok(d), tok(LOCAL_W), tok(MEM_WIDTH),
                  pl.BlockSpec((None, N_MEM, MEM_WIDTH), lambda i, j: (i, 0, 0)),
                  pl.BlockSpec((None, MEM_WIDTH, N_MEM), lambda i, j: (i, 0, 0)),
                  _resident(wo.shape)],
        out_specs=tok(d),
        out_shape=jax.ShapeDtypeStruct((b, s, d), F32),
        compiler_params=_params(2),
        name="out_proj",
    )(x, local, qc, mk, mvt, wo)


def _prep_a(a_w_in):
    nl, d, _ = a_w_in.shape
    order = jnp.array(A_HEAD_ORDER)
    q = a_w_in[:, :, :A_Q_W].reshape(nl, d, A_Q_HEADS, HEAD_DIM)[:, :, order].reshape(nl, d, A_Q_W)
    k = a_w_in[:, :, A_Q_W:A_Q_W + A_KV_W]
    v = a_w_in[:, :, A_Q_W + A_KV_W:A_Q_W + 2 * A_KV_W]
    qc = a_w_in[:, :, A_Q_W + 2 * A_KV_W:]
    w = jnp.concatenate([q, k, qc], axis=-1).astype(BF16)
    return w, jnp.transpose(v, (0, 2, 1)).astype(BF16)


def _prep_w_o(w_o):
    depth, _, d = w_o.shape
    order = jnp.array(A_HEAD_ORDER)
    local = w_o[:, :LOCAL_W].reshape(depth, A_Q_HEADS, HEAD_DIM, d)
    local = jnp.where((jnp.arange(depth) % 2 == 0)[:, None, None, None], local[:, order], local)
    return jnp.concatenate([local.reshape(depth, LOCAL_W, d), w_o[:, LOCAL_W:]], axis=1).astype(BF16)


def _prep_b(b_w_in, b_w_q_up, b_w_kv_up):
    nl, d, _ = b_w_in.shape
    c_q = b_w_in[:, :, :Q_LORA]
    c_kv = b_w_in[:, :, Q_LORA:Q_LORA + KV_LORA]
    k_r = b_w_in[:, :, Q_LORA + KV_LORA:Q_LORA + KV_LORA + QK_ROPE]
    qc = b_w_in[:, :, Q_LORA + KV_LORA + QK_ROPE:]
    kr_tile = jnp.pad(_with_rotate_half(k_r), ((0, 0), (0, 0), (QK_NOPE, 0)))
    w_in = jnp.concatenate([c_q, c_kv, qc, kr_tile], axis=-1).astype(BF16)
    wq = b_w_q_up.reshape(nl, Q_LORA, B_HEADS, B_QK)
    wq = jnp.concatenate([wq[..., :QK_NOPE], _with_rotate_half(wq[..., QK_NOPE:])], axis=-1)
    wq = wq.reshape(nl, Q_LORA, B_HEADS * B_HEAD_PAD).astype(BF16)
    wkv = b_w_kv_up.reshape(nl, KV_LORA, B_HEADS, QK_NOPE + V_HEAD)
    wk = jnp.pad(wkv[..., :QK_NOPE], ((0, 0), (0, 0), (0, 0), (0, B_HEAD_PAD - QK_NOPE)))
    wk = wk.reshape(nl, KV_LORA, B_HEADS * B_HEAD_PAD).astype(BF16)
    wvt = jnp.transpose(wkv[..., QK_NOPE:].reshape(nl, KV_LORA, B_HEADS * V_HEAD), (0, 2, 1)).astype(BF16)
    return w_in, wq, wk, wvt


def _trunk(x, mem, w):
    b, s, d = x.shape
    depth = w["mix_norm"].shape[0]
    tab_a = _rope_table(s, HEAD_DIM, 0, HEAD_DIM, 1.0, False)
    tab_bq = _rope_rot_table(s, B_QK ** -0.5 * LOG2E, True)
    tab_bk = _rope_rot_table(s, 1.0, False)
    for i in range(depth):
        x = _ffn(x.reshape(b * s, d), w["ffn1_norm"][i], w["ffn1_w_gu"], w["ffn1_w_down"], i).reshape(b, s, d)
        mk, mvt = _mem_kv(mem, w["mem_norm"][i], w["w_mem_k"][i], w["w_mem_vt"][i])
        j = i // 2
        if i % 2 == 0:
            q, k, vt, qc = _proj_a(x, w["mix_norm"][i], w["a_w_in"][j], w["a_wvt"][j], tab_a)
            local = _win_attn(q, k, vt, w["a_sink"][j])
        else:
            q, k, v, qc = _proj_b(x, w["mix_norm"][i], w["b_w_in"][j], w["b_q_norm"][j], w["b_wq"][j],
                                  w["b_kv_norm"][j], w["b_wk"][j], w["b_wvt"][j], tab_bq, tab_bk)
            local = _mla_attn(q, k, v)
        x = _out_proj(x, local, qc, mk, mvt, w["w_o"][i])
        g_final = w["final_norm"] if i == depth - 1 else None
        x = _ffn(x.reshape(b * s, d), w["ffn2_norm"][i], w["ffn2_w_gu"], w["ffn2_w_down"], i,
                 g_final).reshape(b, s, d)
    return x


def kernel(x_prompt, x_sample, mem_prompt, mem_sample, ffn1_norm, ffn1_w_gu, ffn1_w_down, mix_norm,
           mem_norm, w_mem_kv, a_w_in, a_sink, b_w_in, b_q_norm, b_w_q_up, b_kv_norm, b_w_kv_up,
           w_o, ffn2_norm, ffn2_w_gu, ffn2_w_down, final_norm):
    w = {
        "ffn1_norm": ffn1_norm, "ffn2_norm": ffn2_norm, "mix_norm": mix_norm, "mem_norm": mem_norm,
        "w_mem_k": w_mem_kv[:, :, :MEM_WIDTH].astype(BF16),
        "w_mem_vt": jnp.transpose(w_mem_kv[:, :, MEM_WIDTH:], (0, 2, 1)).astype(BF16), "a_sink": a_sink,
        "b_q_norm": b_q_norm, "b_kv_norm": b_kv_norm, "w_o": _prep_w_o(w_o),
        "final_norm": final_norm,
    }
    w["ffn1_w_gu"], w["ffn1_w_down"] = ffn1_w_gu.astype(BF16), ffn1_w_down.astype(BF16)
    w["ffn2_w_gu"], w["ffn2_w_down"] = ffn2_w_gu.astype(BF16), ffn2_w_down.astype(BF16)
    w["a_w_in"], w["a_wvt"] = _prep_a(a_w_in)
    w["b_w_in"], w["b_wq"], w["b_wk"], w["b_wvt"] = _prep_b(b_w_in, b_w_q_up, b_w_kv_up)
    return (_trunk(x_prompt, mem_prompt, w), _trunk(x_sample, mem_sample, w))
```

```python
import functools
import math

import jax
import jax.numpy as jnp
from jax import lax
from jax.experimental import pallas as pl
from jax.experimental.pallas import tpu as pltpu

D_MODEL = 1024
HEAD_DIM = 64
ROPE_THETA = 10000.0
NORM_EPS = 1e-6
D_FF = 2816
N_MEM = 256
MEM_HEADS = 4
MEM_WIDTH = MEM_HEADS * HEAD_DIM
A_Q_HEADS = 12
A_KV_HEADS = 4
A_GROUP = A_Q_HEADS // A_KV_HEADS
WINDOW = 128
A_Q_W = A_Q_HEADS * HEAD_DIM
A_KV_W = A_KV_HEADS * HEAD_DIM
B_HEADS = 12
Q_LORA = 384
KV_LORA = 256
QK_NOPE = 64
QK_ROPE = 32
V_HEAD = 64
B_QK = QK_NOPE + QK_ROPE
LOCAL_W = A_Q_W
NEG = -1e30
LOG2E = math.log2(math.e)

LANES = 128
B_HEAD_PAD = LANES
FF_CHUNK = 256
PROJB_SUB = 512
OPROJ_SUB = 512
OPROJ_AHEAD = 3
MLA_QB = 512
MLA_KB = 512
MLA_NKV = 8
MLA_CHUNK = 32
MLA_DEN_ROWS = 16
MLA_AHEAD = 2
MLA_SLOTS = 2 * MLA_AHEAD
VMEM_LIMIT = 56 * 1024 * 1024

F32 = jnp.float32
BF16 = jnp.bfloat16


def _params(n_axes):
    return pltpu.CompilerParams(dimension_semantics=("arbitrary",) * n_axes,
                                vmem_limit_bytes=VMEM_LIMIT)


def _rms(x, g):
    return x * lax.rsqrt(jnp.mean(x * x, axis=-1, keepdims=True) + NORM_EPS) * g


def _dot(a, b):
    return jnp.dot(a, b, preferred_element_type=F32)


def _dot_nt(a, b):
    return lax.dot_general(a, b, (((1,), (1,)), ((), ())), preferred_element_type=F32)


def _resident(shape):
    zeros = (0,) * len(shape)
    return pl.BlockSpec(shape, lambda *_: zeros, pipeline_mode=pl.Buffered(1))


def _ffn_body(*refs, final):
    if final:
        x_ref, g_ref, wgu_ref, wd_ref, gf_ref, o_ref, h_ref = refs
    else:
        x_ref, g_ref, wgu_ref, wd_ref, o_ref, h_ref = refs
    h_ref[...] = _rms(x_ref[...], g_ref[...]).astype(BF16)
    nch = D_FF // FF_CHUNK

    def gate_up(c):
        h = h_ref[...]
        gate = _dot(h, wgu_ref[:, FF_CHUNK * c:FF_CHUNK * (c + 1)])
        up = _dot(h, wgu_ref[:, D_FF + FF_CHUNK * c:D_FF + FF_CHUNK * (c + 1)])
        return gate, up

    acc = None
    nxt = gate_up(0)
    for c in range(nch):
        gate, up = nxt
        if c + 1 < nch:
            nxt = gate_up(c + 1)
        a = (gate / (1.0 + jnp.exp(-gate)) * up).astype(BF16)
        down = _dot(a, wd_ref[FF_CHUNK * c:FF_CHUNK * (c + 1), :])
        acc = down if acc is None else acc + down
    y = x_ref[...] + 0.5 * acc
    if final:
        y = _rms(y, gf_ref[...])
    o_ref[...] = y


def _layer_resident(stacked, layer):
    zeros = (0,) * (stacked.ndim - 1)
    return pl.BlockSpec((None,) + stacked.shape[1:], lambda *_: (layer,) + zeros,
                        pipeline_mode=pl.Buffered(1))


def _ffn(x, g, w_gu, w_down, layer, g_final=None):
    t, d = x.shape
    tm = min(1024, t)
    final = g_final is not None
    in_specs = [
        pl.BlockSpec((tm, d), lambda i: (i, 0)),
        _resident((1, d)),
        _layer_resident(w_gu, layer),
        _layer_resident(w_down, layer),
    ]
    args = [x, g.reshape(1, d), w_gu, w_down]
    if final:
        in_specs.append(_resident((1, d)))
        args.append(g_final.reshape(1, d))
    return pl.pallas_call(
        functools.partial(_ffn_body, final=final),
        grid=(t // tm,),
        in_specs=in_specs,
        out_specs=pl.BlockSpec((tm, d), lambda i: (i, 0)),
        out_shape=jax.ShapeDtypeStruct((t, d), F32),
        scratch_shapes=[pltpu.VMEM((tm, d), BF16)],
        compiler_params=_params(1),
        name="ffn_final" if final else "ffn",
    )(*args)


def _memkv_body(mem_ref, g_ref, wk_ref, wvt_ref, mk_ref, mvt_ref):
    h = _rms(mem_ref[...], g_ref[...]).astype(BF16)
    mk_ref[...] = _dot(h, wk_ref[...]).astype(BF16)
    mvt_ref[...] = _dot_nt(wvt_ref[...], h).astype(BF16)


def _mem_kv(mem, g, wk, wvt):
    b, n, d = mem.shape
    return pl.pallas_call(
        _memkv_body,
        grid=(b,),
        in_specs=[pl.BlockSpec((None, n, d), lambda i: (i, 0, 0)), _resident((1, d)),
                  _resident(wk.shape), _resident(wvt.shape)],
        out_specs=[pl.BlockSpec((None, n, MEM_WIDTH), lambda i: (i, 0, 0)),
                   pl.BlockSpec((None, MEM_WIDTH, n), lambda i: (i, 0, 0))],
        out_shape=[jax.ShapeDtypeStruct((b, n, MEM_WIDTH), BF16),
                   jax.ShapeDtypeStruct((b, MEM_WIDTH, n), BF16)],
        compiler_params=_params(1),
        name="mem_kv",
    )(mem, g.reshape(1, d), wk, wvt)


def _rope_block(xb, tab_ref, shift):
    c = tab_ref[:, 0:LANES]
    s_plus = tab_ref[:, LANES:2 * LANES]
    s_minus = tab_ref[:, 2 * LANES:3 * LANES]
    return (xb * c + pltpu.roll(xb, shift, 1) * s_plus
            + pltpu.roll(xb, LANES - shift, 1) * s_minus)


def _rope_table(seq, dim, lane_start, period, scale, pass_through):
    half = dim // 2
    inv = 1.0 / (ROPE_THETA ** (jnp.arange(0, dim, 2, dtype=F32) / dim))
    ang = jnp.arange(seq, dtype=F32)[:, None] * inv[None, :]
    cos, sin = jnp.cos(ang), jnp.sin(ang)
    zeros_h = jnp.zeros((seq, half), F32)
    lead = jnp.full((seq, lane_start), 1.0 if pass_through else 0.0, F32)
    lead0 = jnp.zeros((seq, lane_start), F32)
    tail0 = jnp.zeros((seq, period - lane_start - dim), F32)
    reps = LANES // period
    c = jnp.tile(jnp.concatenate([lead, cos, cos, tail0], -1), (1, reps))
    s_plus = jnp.tile(jnp.concatenate([lead0, zeros_h, sin, tail0], -1), (1, reps))
    s_minus = jnp.tile(jnp.concatenate([lead0, -sin, zeros_h, tail0], -1), (1, reps))
    return jnp.concatenate([c, s_plus, s_minus], -1) * scale


def _rope_rot_block(xb, tab_ref):
    c = tab_ref[:, 0:LANES]
    s = tab_ref[:, LANES:2 * LANES]
    return xb * c + pltpu.roll(xb, LANES - QK_ROPE, 1) * s


def _rope_rot_table(seq, scale, pass_through):
    inv = 1.0 / (ROPE_THETA ** (jnp.arange(0, QK_ROPE, 2, dtype=F32) / QK_ROPE))
    ang = jnp.arange(seq, dtype=F32)[:, None] * inv[None, :]
    cos, sin = jnp.cos(ang), jnp.sin(ang)
    lead = jnp.full((seq, QK_NOPE), 1.0 if pass_through else 0.0, F32)
    lead0 = jnp.zeros((seq, QK_NOPE), F32)
    tail0 = jnp.zeros((seq, LANES - QK_NOPE - QK_ROPE), F32)
    c = jnp.concatenate([lead, cos, cos, tail0], -1)
    s = jnp.concatenate([lead0, sin, sin, tail0], -1)
    return jnp.concatenate([c, s], -1) * scale


def _with_rotate_half(w_rope):
    x1, x2 = w_rope[..., :QK_ROPE // 2], w_rope[..., QK_ROPE // 2:]
    return jnp.concatenate([w_rope, -x2, x1], axis=-1)


A_HEAD_ORDER = (0, 3, 1, 4, 2, 5, 6, 9, 7, 10, 8, 11)
A_KV_TILES = A_KV_W // LANES
A_TILE_HEADS = A_GROUP
WIN_QBLOCKS = 4
WIN_AHEAD = 3


def _proja_body(x_ref, g_ref, w_ref, wvt_ref, tab_ref, q_ref, k_ref, vt_ref, qc_ref, *, qscale):
    h = _rms(x_ref[...], g_ref[...]).astype(BF16)
    proj = _dot(h, w_ref[...])
    nq = A_Q_W // LANES
    nk = A_KV_W // LANES
    for j in range(nq + nk):
        rb = _rope_block(proj[:, LANES * j:LANES * (j + 1)], tab_ref, HEAD_DIM // 2)
        if j < nq:
            q_ref[:, LANES * j:LANES * (j + 1)] = (rb * qscale).astype(BF16)
        else:
            k_ref[:, LANES * (j - nq):LANES * (j - nq + 1)] = rb.astype(BF16)
    qc_ref[...] = (proj[:, A_Q_W + A_KV_W:] * qscale).astype(BF16)
    vt = _dot_nt(wvt_ref[...], h).astype(BF16)
    for i in range(vt_ref.shape[0]):
        vt_ref[i] = vt[:, WINDOW * i:WINDOW * (i + 1)]


def _proj_a(x, g, w, wvt, tab):
    b, s, d = x.shape
    tm = min(512, s)
    nblk = tm // WINDOW

    def out(width):
        return (jax.ShapeDtypeStruct((b, s, width), BF16),
                pl.BlockSpec((None, tm, width), lambda i, j: (i, j, 0)))

    vt = (jax.ShapeDtypeStruct((b, s // WINDOW, A_KV_W, WINDOW), BF16),
          pl.BlockSpec((None, nblk, A_KV_W, WINDOW), lambda i, j: (i, j, 0, 0)))
    shapes, specs = zip(out(A_Q_W), out(A_KV_W), vt, out(MEM_WIDTH))
    return pl.pallas_call(
        functools.partial(_proja_body, qscale=HEAD_DIM ** -0.5 * LOG2E),
        grid=(b, s // tm),
        in_specs=[
            pl.BlockSpec((None, tm, d), lambda i, j: (i, j, 0)),
            _resident((1, d)),
            _resident(w.shape),
            _resident(wvt.shape),
            pl.BlockSpec((tm, 3 * LANES), lambda i, j: (j, 0)),
        ],
        out_specs=list(specs),
        out_shape=list(shapes),
        compiler_params=_params(2),
        name="proj_a",
    )(x, g.reshape(1, d), w, wvt, tab)


def _wattn_body(bias_ref, sink_ref, q_ref, kp_ref, kc_ref, kn_ref, vp_ref, vc_ref, vn_ref, o_ref, *, nsteps):
    step = pl.program_id(1)
    lane = lax.broadcasted_iota(jnp.int32, (WINDOW, LANES), 1)
    ones = jnp.ones((MLA_DEN_ROWS, 3 * WINDOW), BF16)
    zero = jnp.zeros((WINDOW, LANES), BF16)
    def operands(blk, tile):
        ts = slice(LANES * tile, LANES * (tile + 1))
        k_blocks = ([kp_ref[:, ts]] + [kc_ref[WINDOW * i:WINDOW * (i + 1), ts] for i in range(WIN_QBLOCKS)]
                    + [kn_ref[:, ts]])
        vt_blocks = [vp_ref[ts, :]] + [vc_ref[i, ts, :] for i in range(WIN_QBLOCKS)] + [vn_ref[ts, :]]
        k_band = jnp.concatenate(k_blocks[blk:blk + 3], axis=0)
        vt_band = jnp.concatenate(vt_blocks[blk:blk + 3], axis=1)
        return k_band, vt_band

    def scores(blk, tile, half):
        variant = 1
        if blk == 0:
            variant = jnp.where(step == 0, 0, variant)
        if blk == WIN_QBLOCKS - 1:
            variant = jnp.where(step == nsteps - 1, 2, variant)
        rows = slice(WINDOW * blk, WINDOW * (blk + 1))
        q_tiles = [q_ref[rows, LANES * (A_TILE_HEADS * tile + r):LANES * (A_TILE_HEADS * tile + r + 1)]
                   for r in range(A_TILE_HEADS)]
        in_half = (lane < HEAD_DIM) if half == 0 else (lane >= HEAD_DIM)
        q_stack = jnp.concatenate([jnp.where(in_half, qt, zero) for qt in q_tiles], axis=0)
        return _dot_nt(operands(blk, tile)[0], q_stack) + bias_ref[variant]

    def attend(blk, tile, half, s):
        sink = sink_ref[2 * tile + half]
        m = jnp.maximum(jnp.max(s, axis=0, keepdims=True), sink)
        p = jnp.exp2(s - m).astype(BF16)
        vt_band = operands(blk, tile)[1]
        v_aug = jnp.concatenate([vt_band[HEAD_DIM * half:HEAD_DIM * (half + 1), :], ones], axis=0)
        o_aug = _dot(v_aug, p)
        den = o_aug[HEAD_DIM:HEAD_DIM + 1, :] + jnp.exp2(sink - m)
        return o_aug[:HEAD_DIM, :] / den

    def emit(blk, tile, halves):
        rows = slice(WINDOW * blk, WINDOW * (blk + 1))
        for r in range(A_TILE_HEADS):
            cs = slice(WINDOW * r, WINDOW * (r + 1))
            out_t = jnp.concatenate([halves[0][:, cs], halves[1][:, cs]], axis=0)
            j = A_TILE_HEADS * tile + r
            o_ref[rows, LANES * j:LANES * (j + 1)] = out_t.T.astype(BF16)

    chains = [(blk, tile, half) for blk in range(WIN_QBLOCKS) for tile in range(A_KV_TILES) for half in range(2)]
    pending, done = {}, {}
    for i in range(len(chains) + WIN_AHEAD):
        if i < len(chains):
            pending[chains[i]] = scores(*chains[i])
        if i >= WIN_AHEAD:
            blk, tile, half = chains[i - WIN_AHEAD]
            done[(blk, tile, half)] = attend(blk, tile, half, pending.pop((blk, tile, half)))
            if half == 1:
                emit(blk, tile, [done.pop((blk, tile, 0)), done.pop((blk, tile, 1))])


def _band_bias(dtype=F32):
    kj = jnp.arange(3 * WINDOW)[:, None]
    qi = (jnp.arange(3 * WINDOW) % WINDOW)[None, :]
    band = (kj - qi >= 0) & (kj - qi <= 2 * WINDOW)
    first = band & (kj >= WINDOW)
    last = band & (kj < 2 * WINDOW)
    return jnp.where(jnp.stack([first, band, last]), 0.0, NEG).astype(dtype)


def _win_attn(q, k, vt, sink):
    b, s, _ = q.shape
    nb = s // WINDOW
    nsteps = nb // WIN_QBLOCKS
    assert nb >= 2 and nb % WIN_QBLOCKS == 0
    qrows = WIN_QBLOCKS * WINDOW
    sink_rows = jnp.repeat(sink.reshape(A_KV_HEADS, 1, A_GROUP) * LOG2E, WINDOW, axis=-1)

    def halo(j, shift):
        return jnp.clip(WIN_QBLOCKS * j + shift, 0, nb - 1)

    def k_halo(shift):
        return pl.BlockSpec((None, WINDOW, A_KV_W), lambda i, j: (i, halo(j, shift), 0))

    def vt_halo(shift):
        return pl.BlockSpec((None, None, A_KV_W, WINDOW), lambda i, j: (i, halo(j, shift), 0, 0))

    return pl.pallas_call(
        functools.partial(_wattn_body, nsteps=nsteps),
        grid=(b, nsteps),
        in_specs=[
            _resident((3, 3 * WINDOW, 3 * WINDOW)),
            _resident((A_KV_HEADS, 1, 3 * WINDOW)),
            pl.BlockSpec((None, qrows, A_Q_W), lambda i, j: (i, j, 0)),
            k_halo(-1),
            pl.BlockSpec((None, qrows, A_KV_W), lambda i, j: (i, j, 0)),
            k_halo(WIN_QBLOCKS),
            vt_halo(-1),
            pl.BlockSpec((None, WIN_QBLOCKS, A_KV_W, WINDOW), lambda i, j: (i, j, 0, 0)),
            vt_halo(WIN_QBLOCKS),
        ],
        out_specs=pl.BlockSpec((None, qrows, A_Q_W), lambda i, j: (i, j, 0)),
        out_shape=jax.ShapeDtypeStruct((b, s, A_Q_W), BF16),
        compiler_params=_params(2),
        name="win_attn",
    )(_band_bias(), sink_rows, q, k, k, k, vt, vt, vt)


def _projb_body(x_ref, g_ref, win_ref, gq_ref, wq_ref, gkv_ref, wk_ref, wvt_ref, tabq_ref, tabk_ref,
                q_ref, k_ref, vt_ref, qc_ref, *, qcscale, sub):
    kb = vt_ref.shape[-1]
    subs = [slice(r0, r0 + sub) for r0 in range(0, x_ref.shape[0], sub)]

    def latents(rows):
        h = _rms(x_ref[rows, :], g_ref[...]).astype(BF16)
        proj = _dot(h, win_ref[...])
        c_q = _rms(proj[:, :Q_LORA], gq_ref[...]).astype(BF16)
        kv0 = Q_LORA
        c_kv = _rms(proj[:, kv0:kv0 + KV_LORA], gkv_ref[...]).astype(BF16)
        qc0 = kv0 + KV_LORA
        qc_ref[rows, :] = (proj[:, qc0:qc0 + MEM_WIDTH] * qcscale).astype(BF16)
        kr0 = qc0 + MEM_WIDTH
        return c_q, c_kv, proj[:, kr0:kr0 + LANES]

    def up_project(c_q, c_kv):
        q_all = _dot(c_q, wq_ref[...])
        k_all = _dot(c_kv, wk_ref[...])
        vt = _dot_nt(wvt_ref[...], c_kv)
        return q_all, k_all, vt

    def finish(rows, q_all, k_all, vt, kr_tile):
        r0 = rows.start
        vt_ref[r0 // kb, :, r0 % kb:r0 % kb + sub] = vt.astype(BF16)
        k_rope = _rope_rot_block(kr_tile, tabk_ref.at[rows, :])
        for hd in range(B_HEADS):
            hs = slice(B_HEAD_PAD * hd, B_HEAD_PAD * (hd + 1))
            q_ref[rows, hs] = _rope_rot_block(q_all[:, hs], tabq_ref.at[rows, :]).astype(BF16)
            k_ref[rows, hs] = (k_all[:, hs] + k_rope).astype(BF16)

    lat = [latents(rows) for rows in subs]
    ups = [up_project(c_q, c_kv) for c_q, c_kv, _ in lat]
    for rows, (q_all, k_all, vt), (_, _, kr_tile) in zip(subs, ups, lat):
        finish(rows, q_all, k_all, vt, kr_tile)


def _proj_b(x, g, w_in, gq, wq, gkv, wk, wvt, tabq, tabk):
    b, s, d = x.shape
    kb = min(max(MLA_KB, s // MLA_NKV), s)
    tm = min(2 * PROJB_SUB, s)
    sub = min(PROJB_SUB, tm)
    assert tm % kb == 0 and kb % sub == 0

    def out(width):
        return (jax.ShapeDtypeStruct((b, s, width), BF16),
                pl.BlockSpec((None, tm, width), lambda i, j: (i, j, 0)))

    vt = (jax.ShapeDtypeStruct((b, s // kb, B_HEADS * V_HEAD, kb), BF16),
          pl.BlockSpec((None, tm // kb, B_HEADS * V_HEAD, kb), lambda i, j: (i, j, 0, 0)))
    shapes, specs = zip(out(B_HEADS * B_HEAD_PAD), out(B_HEADS * B_HEAD_PAD), vt, out(MEM_WIDTH))
    tab_spec = pl.BlockSpec((tm, 2 * LANES), lambda i, j: (j, 0))
    return pl.pallas_call(
        functools.partial(_projb_body, qcscale=HEAD_DIM ** -0.5 * LOG2E, sub=sub),
        grid=(b, s // tm),
        in_specs=[
            pl.BlockSpec((None, tm, d), lambda i, j: (i, j, 0)),
            _resident((1, d)),
            _resident(w_in.shape),
            _resident((1, Q_LORA)),
            _resident(wq.shape),
            _resident((1, KV_LORA)),
            _resident(wk.shape),
            _resident(wvt.shape),
            tab_spec, tab_spec,
        ],
        out_specs=list(specs),
        out_shape=list(shapes),
        compiler_params=_params(2),
        name="proj_b",
    )(x, g.reshape(1, d), w_in, gq.reshape(1, -1), wq, gkv.reshape(1, -1), wk, wvt, tabq, tabk)


def _mla_body(q_ref, k_ref, vt_ref, o_ref, *scratch, kb, nkv):
    per_slot = 3 * 2
    slots = [scratch[per_slot * i:per_slot * (i + 1)] for i in range(MLA_SLOTS)]
    s_refs = [sl[0:2] for sl in slots]
    p_refs = [sl[2:4] for sl in slots]
    alpha_refs = [sl[4:6] for sl in slots]
    m_refs, acc_refs = (scratch[per_slot * MLA_SLOTS + 2 * i:per_slot * MLA_SLOTS + 2 * (i + 1)]
                        for i in range(2))
    hslices = [slice(B_HEAD_PAD * hh, B_HEAD_PAD * (hh + 1)) for hh in range(2)]
    vslices = [slice(V_HEAD * hh, V_HEAD * (hh + 1)) for hh in range(2)]
    chunks = [slice(c, c + MLA_CHUNK) for c in range(0, kb, MLA_CHUNK)]

    def scores(t, slot):
        rows = pl.ds(pl.multiple_of(t * kb, kb), kb)
        for hh, hs in enumerate(hslices):
            s_refs[slot][hh][...] = _dot_nt(k_ref[rows, hs], q_ref[:, hs])

    def softmax(slot):
        for hh in range(2):
            s_ref, p_ref = s_refs[slot][hh], p_refs[slot][hh]
            cmax = s_ref[chunks[0], :]
            for ch in chunks[1:]:
                cmax = jnp.maximum(cmax, s_ref[ch, :])
            m = m_refs[hh][...]
            m_new = jnp.maximum(m, jnp.max(cmax, axis=0, keepdims=True))
            alpha = jnp.exp2(m - m_new)
            for ch in chunks:
                p_ref[ch, :] = jnp.exp2(s_ref[ch, :] - m_new).astype(BF16)
            alpha_refs[slot][hh][...] = alpha
            m_refs[hh][...] = m_new

    def accumulate(t, slot):
        ones = jnp.ones((MLA_DEN_ROWS, kb), BF16)
        for hh, vs in enumerate(vslices):
            v_aug = jnp.concatenate([vt_ref[t, vs, :], ones], axis=0)
            acc_refs[hh][...] = (alpha_refs[slot][hh][...] * acc_refs[hh][...]
                                 + _dot(v_aug, p_refs[slot][hh][...]))

    def stage(t, phase, ahead=True, behind=True):
        if ahead:
            scores(t + MLA_AHEAD, (phase + MLA_AHEAD) % MLA_SLOTS)
        if behind:
            accumulate(t - MLA_AHEAD, (phase - MLA_AHEAD) % MLA_SLOTS)
        softmax(phase)

    lo = MLA_AHEAD
    trips = max(nkv - 2 * MLA_AHEAD, 0) // MLA_SLOTS
    tail = range(lo + trips * MLA_SLOTS, nkv)
    assert all(t + MLA_AHEAD >= nkv for t in tail)

    def init_m():
        for hh in range(2):
            m_refs[hh][...] = jnp.full(m_refs[hh].shape, NEG, F32)

    def init_acc():
        for hh in range(2):
            acc_refs[hh][...] = jnp.zeros(acc_refs[hh].shape, F32)

    def output():
        out_t = jnp.concatenate(
            [acc_refs[hh][:V_HEAD, :] / acc_refs[hh][V_HEAD:V_HEAD + 1, :] for hh in range(2)], axis=0)
        o_ref[...] = out_t.T.astype(BF16)

    def loop():
        def full_stages(i, carry):
            for j in range(MLA_SLOTS):
                stage(lo + MLA_SLOTS * i + j, (lo + j) % MLA_SLOTS)
            return carry

        lax.fori_loop(0, trips, full_stages, 0)

    fill_scores = [functools.partial(scores, t, t % MLA_SLOTS) for t in range(min(MLA_AHEAD, nkv))]
    fill_stages = [functools.partial(stage, t, t % MLA_SLOTS, ahead=t + MLA_AHEAD < nkv, behind=False)
                   for t in range(min(lo, nkv))]
    drain_stages = [functools.partial(stage, t, t % MLA_SLOTS, ahead=False, behind=t >= MLA_AHEAD) for t in tail]
    drain_accs = [functools.partial(accumulate, t, t % MLA_SLOTS) for t in range(max(nkv - MLA_AHEAD, 0), nkv)]

    def run(pieces):
        for piece in pieces:
            piece()

    n = pl.program_id(2)
    last = pl.num_programs(2) - 1

    def fill_and_loop():
        run(fill_scores)
        init_m()
        init_acc()
        run(fill_stages)
        loop()

    def drain():
        run(drain_stages)
        run(drain_accs)
        output()

    @pl.when(n == 0)
    def _():
        fill_and_loop()

    @pl.when(jnp.logical_and(n > 0, n < last))
    def _():
        run(drain_stages)
        run(fill_scores)
        run(drain_accs)
        output()
        init_m()
        init_acc()
        run(fill_stages)
        loop()

    @pl.when(n == last)
    def _():
        drain()


def _mla_attn(q, k, vt):
    b, s, _ = q.shape
    _, nkv, _, kb = vt.shape
    qb = min(MLA_QB, s)
    nq = s // qb
    npair = B_HEADS // 2
    per_slot = ([pltpu.VMEM((kb, qb), F32)] * 2
                + [pltpu.VMEM((kb, qb), BF16)] * 2
                + [pltpu.VMEM((1, qb), F32)] * 2)
    state = ([pltpu.VMEM((1, qb), F32)] * 2
             + [pltpu.VMEM((V_HEAD + MLA_DEN_ROWS, qb), F32)] * 2)
    return pl.pallas_call(
        functools.partial(_mla_body, kb=kb, nkv=nkv),
        grid=(b, npair, nq + 1),
        in_specs=[
            pl.BlockSpec((None, qb, 2 * B_HEAD_PAD), lambda i, j, n: (i, jnp.minimum(n, nq - 1), j)),
            pl.BlockSpec((None, s, 2 * B_HEAD_PAD), lambda i, j, n: (i, 0, j)),
            pl.BlockSpec((None, nkv, 2 * V_HEAD, kb), lambda i, j, n: (i, 0, j, 0)),
        ],
        out_specs=pl.BlockSpec((None, qb, 2 * V_HEAD), lambda i, j, n: (i, jnp.maximum(n - 1, 0), j)),
        out_shape=jax.ShapeDtypeStruct((b, s, B_HEADS * V_HEAD), BF16),
        scratch_shapes=per_slot * MLA_SLOTS + state,
        compiler_params=_params(3),
        name="mla_attn",
    )(q, k, vt)


def _oproj_body(x_ref, loc_ref, qc_ref, mk_ref, mvt_ref, wo_ref, o_ref):
    sub = min(OPROJ_SUB, qc_ref.shape[0])
    lane = lax.broadcasted_iota(jnp.int32, (sub, LANES), 1)
    ones = jnp.ones((MLA_DEN_ROWS, N_MEM), BF16)
    zero = jnp.zeros((sub, LANES), BF16)
    subs = [slice(r0, r0 + sub) for r0 in range(0, qc_ref.shape[0], sub)]

    def scores(rows, hd):
        ts = slice(LANES * (hd // 2), LANES * (hd // 2 + 1))
        in_half = (lane < HEAD_DIM) if hd % 2 == 0 else (lane >= HEAD_DIM)
        return _dot_nt(mk_ref[:, ts], jnp.where(in_half, qc_ref[rows, ts], zero))

    def attend(hd, s):
        p = jnp.exp2(s - jnp.max(s, axis=0, keepdims=True)).astype(BF16)
        v_aug = jnp.concatenate([mvt_ref[HEAD_DIM * hd:HEAD_DIM * (hd + 1), :], ones], axis=0)
        o_aug = _dot(v_aug, p)
        return o_aug[:HEAD_DIM, :] / o_aug[HEAD_DIM:HEAD_DIM + 1, :]

    y_local = [_dot(loc_ref[rows, :], wo_ref[:LOCAL_W, :]) for rows in subs]
    chains = [(i, hd) for i in range(len(subs)) for hd in range(MEM_HEADS)]
    pending, heads = {}, {}
    for c in range(len(chains) + OPROJ_AHEAD):
        if c < len(chains):
            i, hd = chains[c]
            pending[(i, hd)] = scores(subs[i], hd)
        if c >= OPROJ_AHEAD:
            i, hd = chains[c - OPROJ_AHEAD]
            heads[(i, hd)] = attend(hd, pending.pop((i, hd)))
            if hd == MEM_HEADS - 1:
                tiles = [jnp.concatenate([heads.pop((i, 2 * t)), heads.pop((i, 2 * t + 1))], axis=0).T.astype(BF16)
                         for t in range(MEM_HEADS // 2)]
                cross = jnp.concatenate(tiles, axis=-1)
                o_ref[subs[i], :] = x_ref[subs[i], :] + (y_local[i] + _dot(cross, wo_ref[LOCAL_W:, :]))


def _out_proj(x, local, qc, mk, mvt, wo):
    b, s, d = x.shape
    tm = min(2 * OPROJ_SUB, s)

    def tok(width):
        return pl.BlockSpec((None, tm, width), lambda i, j: (i, j, 0))

    return pl.pallas_call(
        _oproj_body,
        grid=(b, s // tm),
        in_specs=[tok(d), tok(LOCAL_W), tok(MEM_WIDTH),
                  pl.BlockSpec((None, N_MEM, MEM_WIDTH), lambda i, j: (i, 0, 0)),
                  pl.BlockSpec((None, MEM_WIDTH, N_MEM), lambda i, j: (i, 0, 0)),
                  _resident(wo.shape)],
        out_specs=tok(d),
        out_shape=jax.ShapeDtypeStruct((b, s, d), F32),
        compiler_params=_params(2),
        name="out_proj",
    )(x, local, qc, mk, mvt, wo)


def _prep_a(a_w_in):
    nl, d, _ = a_w_in.shape
    order = jnp.array(A_HEAD_ORDER)
    q = a_w_in[:, :, :A_Q_W].reshape(nl, d, A_Q_HEADS, HEAD_DIM)[:, :, order].reshape(nl, d, A_Q_W)
    k = a_w_in[:, :, A_Q_W:A_Q_W + A_KV_W]
    v = a_w_in[:, :, A_Q_W + A_KV_W:A_Q_W + 2 * A_KV_W]
    qc = a_w_in[:, :, A_Q_W + 2 * A_KV_W:]
    w = jnp.concatenate([q, k, qc], axis=-1).astype(BF16)
    return w, jnp.transpose(v, (0, 2, 1)).astype(BF16)


def _prep_w_o(w_o):
    depth, _, d = w_o.shape
    order = jnp.array(A_HEAD_ORDER)
    local = w_o[:, :LOCAL_W].reshape(depth, A_Q_HEADS, HEAD_DIM, d)
    local = jnp.where((jnp.arange(depth) % 2 == 0)[:, None, None, None], local[:, order], local)
    return jnp.concatenate([local.reshape(depth, LOCAL_W, d), w_o[:, LOCAL_W:]], axis=1).astype(BF16)


def _prep_b(b_w_in, b_w_q_up, b_w_kv_up):
    nl, d, _ = b_w_in.shape
    c_q = b_w_in[:, :, :Q_LORA]
    c_kv = b_w_in[:, :, Q_LORA:Q_LORA + KV_LORA]
    k_r = b_w_in[:, :, Q_LORA + KV_LORA:Q_LORA + KV_LORA + QK_ROPE]
    qc = b_w_in[:, :, Q_LORA + KV_LORA + QK_ROPE:]
    kr_tile = jnp.pad(_with_rotate_half(k_r), ((0, 0), (0, 0), (QK_NOPE, 0)))
    w_in = jnp.concatenate([c_q, c_kv, qc, kr_tile], axis=-1).astype(BF16)
    wq = b_w_q_up.reshape(nl, Q_LORA, B_HEADS, B_QK)
    wq = jnp.concatenate([wq[..., :QK_NOPE], _with_rotate_half(wq[..., QK_NOPE:])], axis=-1)
    wq = wq.reshape(nl, Q_LORA, B_HEADS * B_HEAD_PAD).astype(BF16)
    wkv = b_w_kv_up.reshape(nl, KV_LORA, B_HEADS, QK_NOPE + V_HEAD)
    wk = jnp.pad(wkv[..., :QK_NOPE], ((0, 0), (0, 0), (0, 0), (0, B_HEAD_PAD - QK_NOPE)))
    wk = wk.reshape(nl, KV_LORA, B_HEADS * B_HEAD_PAD).astype(BF16)
    wvt = jnp.transpose(wkv[..., QK_NOPE:].reshape(nl, KV_LORA, B_HEADS * V_HEAD), (0, 2, 1)).astype(BF16)
    return w_in, wq, wk, wvt


def _trunk(x, mem, w):
    b, s, d = x.shape
    depth = w["mix_norm"].shape[0]
    tab_a = _rope_table(s, HEAD_DIM, 0, HEAD_DIM, 1.0, False)
    tab_bq = _rope_rot_table(s, B_QK ** -0.5 * LOG2E, True)
    tab_bk = _rope_rot_table(s, 1.0, False)
    for i in range(depth):
        x = _ffn(x.reshape(b * s, d), w["ffn1_norm"][i], w["ffn1_w_gu"], w["ffn1_w_down"], i).reshape(b, s, d)
        mk, mvt = _mem_kv(mem, w["mem_norm"][i], w["w_mem_k"][i], w["w_mem_vt"][i])
        j = i // 2
        if i % 2 == 0:
            q, k, vt, qc = _proj_a(x, w["mix_norm"][i], w["a_w_in"][j], w["a_wvt"][j], tab_a)
            local = _win_attn(q, k, vt, w["a_sink"][j])
        else:
            q, k, v, qc = _proj_b(x, w["mix_norm"][i], w["b_w_in"][j], w["b_q_norm"][j], w["b_wq"][j],
                                  w["b_kv_norm"][j], w["b_wk"][j], w["b_wvt"][j], tab_bq, tab_bk)
            local = _mla_attn(q, k, v)
        x = _out_proj(x, local, qc, mk, mvt, w["w_o"][i])
        g_final = w["final_norm"] if i == depth - 1 else None
        x = _ffn(x.reshape(b * s, d), w["ffn2_norm"][i], w["ffn2_w_gu"], w["ffn2_w_down"], i,
                 g_final).reshape(b, s, d)
    return x


def kernel(x_prompt, x_sample, mem_prompt, mem_sample, ffn1_norm, ffn1_w_gu, ffn1_w_down, mix_norm,
           mem_norm, w_mem_kv, a_w_in, a_sink, b_w_in, b_q_norm, b_w_q_up, b_kv_norm, b_w_kv_up,
           w_o, ffn2_norm, ffn2_w_gu, ffn2_w_down, final_norm):
    w = {
        "ffn1_norm": ffn1_norm, "ffn2_norm": ffn2_norm, "mix_norm": mix_norm, "mem_norm": mem_norm,
        "w_mem_k": w_mem_kv[:, :, :MEM_WIDTH].astype(BF16),
        "w_mem_vt": jnp.transpose(w_mem_kv[:, :, MEM_WIDTH:], (0, 2, 1)).astype(BF16), "a_sink": a_sink,
        "b_q_norm": b_q_norm, "b_kv_norm": b_kv_norm, "w_o": _prep_w_o(w_o),
        "final_norm": final_norm,
    }
    w["ffn1_w_gu"], w["ffn1_w_down"] = ffn1_w_gu.astype(BF16), ffn1_w_down.astype(BF16)
    w["ffn2_w_gu"], w["ffn2_w_down"] = ffn2_w_gu.astype(BF16), ffn2_w_down.astype(BF16)
    w["a_w_in"], w["a_wvt"] = _prep_a(a_w_in)
    w["b_w_in"], w["b_wq"], w["b_wk"], w["b_wvt"] = _prep_b(b_w_in, b_w_q_up, b_w_kv_up)
    return (_trunk(x_prompt, mem_prompt, w), _trunk(x_sample, mem_sample, w))
```

```python
import functools
import math

import jax
import jax.numpy as jnp
from jax import lax
from jax.experimental import pallas as pl
from jax.experimental.pallas import tpu as pltpu

D_MODEL = 1024
HEAD_DIM = 64
ROPE_THETA = 10000.0
NORM_EPS = 1e-6
D_FF = 2816
N_MEM = 256
MEM_HEADS = 4
MEM_WIDTH = MEM_HEADS * HEAD_DIM
A_Q_HEADS = 12
A_KV_HEADS = 4
A_GROUP = A_Q_HEADS // A_KV_HEADS
WINDOW = 128
A_Q_W = A_Q_HEADS * HEAD_DIM
A_KV_W = A_KV_HEADS * HEAD_DIM
B_HEADS = 12
Q_LORA = 384
KV_LORA = 256
QK_NOPE = 64
QK_ROPE = 32
V_HEAD = 64
B_QK = QK_NOPE + QK_ROPE
LOCAL_W = A_Q_W
NEG = -1e30
LOG2E = math.log2(math.e)

LANES = 128
B_HEAD_PAD = LANES
FF_CHUNK = 256
PROJB_SUB = 512
OPROJ_SUB = 512
OPROJ_AHEAD = 4
MLA_QB = 512
MLA_KB = 512
MLA_NKV = 8
MLA_CHUNK = 32
MLA_DEN_ROWS = 16
MLA_AHEAD = 2
MLA_SLOTS = 2 * MLA_AHEAD
VMEM_LIMIT = 56 * 1024 * 1024

F32 = jnp.float32
BF16 = jnp.bfloat16


def _params(n_axes):
    return pltpu.CompilerParams(dimension_semantics=("arbitrary",) * n_axes,
                                vmem_limit_bytes=VMEM_LIMIT)


def _rms(x, g):
    return x * lax.rsqrt(jnp.mean(x * x, axis=-1, keepdims=True) + NORM_EPS) * g


def _dot(a, b):
    return jnp.dot(a, b, preferred_element_type=F32)


def _dot_nt(a, b):
    return lax.dot_general(a, b, (((1,), (1,)), ((), ())), preferred_element_type=F32)


def _resident(shape):
    zeros = (0,) * len(shape)
    return pl.BlockSpec(shape, lambda *_: zeros, pipeline_mode=pl.Buffered(1))


def _ffn_body(*refs, final):
    if final:
        x_ref, g_ref, wgu_ref, wd_ref, gf_ref, o_ref, h_ref = refs
    else:
        x_ref, g_ref, wgu_ref, wd_ref, o_ref, h_ref = refs
    h_ref[...] = _rms(x_ref[...], g_ref[...]).astype(BF16)
    nch = D_FF // FF_CHUNK

    def gate_up(c):
        h = h_ref[...]
        gate = _dot(h, wgu_ref[:, FF_CHUNK * c:FF_CHUNK * (c + 1)])
        up = _dot(h, wgu_ref[:, D_FF + FF_CHUNK * c:D_FF + FF_CHUNK * (c + 1)])
        return gate, up

    acc = None
    nxt = gate_up(0)
    for c in range(nch):
        gate, up = nxt
        if c + 1 < nch:
            nxt = gate_up(c + 1)
        a = (gate / (1.0 + jnp.exp(-gate)) * up).astype(BF16)
        down = _dot(a, wd_ref[FF_CHUNK * c:FF_CHUNK * (c + 1), :])
        acc = down if acc is None else acc + down
    y = x_ref[...] + 0.5 * acc
    if final:
        y = _rms(y, gf_ref[...])
    o_ref[...] = y


def _layer_resident(stacked, layer):
    zeros = (0,) * (stacked.ndim - 1)
    return pl.BlockSpec((None,) + stacked.shape[1:], lambda *_: (layer,) + zeros,
                        pipeline_mode=pl.Buffered(1))


def _ffn(x, g, w_gu, w_down, layer, g_final=None):
    t, d = x.shape
    tm = min(1024, t)
    final = g_final is not None
    in_specs = [
        pl.BlockSpec((tm, d), lambda i: (i, 0)),
        _resident((1, d)),
        _layer_resident(w_gu, layer),
        _layer_resident(w_down, layer),
    ]
    args = [x, g.reshape(1, d), w_gu, w_down]
    if final:
        in_specs.append(_resident((1, d)))
        args.append(g_final.reshape(1, d))
    return pl.pallas_call(
        functools.partial(_ffn_body, final=final),
        grid=(t // tm,),
        in_specs=in_specs,
        out_specs=pl.BlockSpec((tm, d), lambda i: (i, 0)),
        out_shape=jax.ShapeDtypeStruct((t, d), F32),
        scratch_shapes=[pltpu.VMEM((tm, d), BF16)],
        compiler_params=_params(1),
        name="ffn_final" if final else "ffn",
    )(*args)


def _memkv_body(mem_ref, g_ref, wk_ref, wvt_ref, mk_ref, mvt_ref):
    h = _rms(mem_ref[...], g_ref[...]).astype(BF16)
    mk_ref[...] = _dot(h, wk_ref[...]).astype(BF16)
    mvt_ref[...] = _dot_nt(wvt_ref[...], h).astype(BF16)


def _mem_kv(mem, g, wk, wvt):
    b, n, d = mem.shape
    return pl.pallas_call(
        _memkv_body,
        grid=(b,),
        in_specs=[pl.BlockSpec((None, n, d), lambda i: (i, 0, 0)), _resident((1, d)),
                  _resident(wk.shape), _resident(wvt.shape)],
        out_specs=[pl.BlockSpec((None, n, MEM_WIDTH), lambda i: (i, 0, 0)),
                   pl.BlockSpec((None, MEM_WIDTH, n), lambda i: (i, 0, 0))],
        out_shape=[jax.ShapeDtypeStruct((b, n, MEM_WIDTH), BF16),
                   jax.ShapeDtypeStruct((b, MEM_WIDTH, n), BF16)],
        compiler_params=_params(1),
        name="mem_kv",
    )(mem, g.reshape(1, d), wk, wvt)


def _rope_block(xb, tab_ref, shift):
    c = tab_ref[:, 0:LANES]
    s_plus = tab_ref[:, LANES:2 * LANES]
    s_minus = tab_ref[:, 2 * LANES:3 * LANES]
    return (xb * c + pltpu.roll(xb, shift, 1) * s_plus
            + pltpu.roll(xb, LANES - shift, 1) * s_minus)


def _rope_table(seq, dim, lane_start, period, scale, pass_through):
    half = dim // 2
    inv = 1.0 / (ROPE_THETA ** (jnp.arange(0, dim, 2, dtype=F32) / dim))
    ang = jnp.arange(seq, dtype=F32)[:, None] * inv[None, :]
    cos, sin = jnp.cos(ang), jnp.sin(ang)
    zeros_h = jnp.zeros((seq, half), F32)
    lead = jnp.full((seq, lane_start), 1.0 if pass_through else 0.0, F32)
    lead0 = jnp.zeros((seq, lane_start), F32)
    tail0 = jnp.zeros((seq, period - lane_start - dim), F32)
    reps = LANES // period
    c = jnp.tile(jnp.concatenate([lead, cos, cos, tail0], -1), (1, reps))
    s_plus = jnp.tile(jnp.concatenate([lead0, zeros_h, sin, tail0], -1), (1, reps))
    s_minus = jnp.tile(jnp.concatenate([lead0, -sin, zeros_h, tail0], -1), (1, reps))
    return jnp.concatenate([c, s_plus, s_minus], -1) * scale


def _rope_rot_block(xb, tab_ref):
    c = tab_ref[:, 0:LANES]
    s = tab_ref[:, LANES:2 * LANES]
    return xb * c + pltpu.roll(xb, LANES - QK_ROPE, 1) * s


def _rope_rot_table(seq, scale, pass_through):
    inv = 1.0 / (ROPE_THETA ** (jnp.arange(0, QK_ROPE, 2, dtype=F32) / QK_ROPE))
    ang = jnp.arange(seq, dtype=F32)[:, None] * inv[None, :]
    cos, sin = jnp.cos(ang), jnp.sin(ang)
    lead = jnp.full((seq, QK_NOPE), 1.0 if pass_through else 0.0, F32)
    lead0 = jnp.zeros((seq, QK_NOPE), F32)
    tail0 = jnp.zeros((seq, LANES - QK_NOPE - QK_ROPE), F32)
    c = jnp.concatenate([lead, cos, cos, tail0], -1)
    s = jnp.concatenate([lead0, sin, sin, tail0], -1)
    return jnp.concatenate([c, s], -1) * scale


def _with_rotate_half(w_rope):
    x1, x2 = w_rope[..., :QK_ROPE // 2], w_rope[..., QK_ROPE // 2:]
    return jnp.concatenate([w_rope, -x2, x1], axis=-1)


A_HEAD_ORDER = (0, 3, 1, 4, 2, 5, 6, 9, 7, 10, 8, 11)
A_KV_TILES = A_KV_W // LANES
A_TILE_HEADS = A_GROUP
WIN_QBLOCKS = 4
WIN_AHEAD = 3


def _proja_body(x_ref, g_ref, w_ref, wvt_ref, tab_ref, q_ref, k_ref, vt_ref, qc_ref, *, qscale):
    h = _rms(x_ref[...], g_ref[...]).astype(BF16)
    proj = _dot(h, w_ref[...])
    nq = A_Q_W // LANES
    nk = A_KV_W // LANES
    for j in range(nq + nk):
        rb = _rope_block(proj[:, LANES * j:LANES * (j + 1)], tab_ref, HEAD_DIM // 2)
        if j < nq:
            q_ref[:, LANES * j:LANES * (j + 1)] = (rb * qscale).astype(BF16)
        else:
            k_ref[:, LANES * (j - nq):LANES * (j - nq + 1)] = rb.astype(BF16)
    qc_ref[...] = (proj[:, A_Q_W + A_KV_W:] * qscale).astype(BF16)
    vt = _dot_nt(wvt_ref[...], h).astype(BF16)
    for i in range(vt_ref.shape[0]):
        vt_ref[i] = vt[:, WINDOW * i:WINDOW * (i + 1)]


def _proj_a(x, g, w, wvt, tab):
    b, s, d = x.shape
    tm = min(512, s)
    nblk = tm // WINDOW

    def out(width):
        return (jax.ShapeDtypeStruct((b, s, width), BF16),
                pl.BlockSpec((None, tm, width), lambda i, j: (i, j, 0)))

    vt = (jax.ShapeDtypeStruct((b, s // WINDOW, A_KV_W, WINDOW), BF16),
          pl.BlockSpec((None, nblk, A_KV_W, WINDOW), lambda i, j: (i, j, 0, 0)))
    shapes, specs = zip(out(A_Q_W), out(A_KV_W), vt, out(MEM_WIDTH))
    return pl.pallas_call(
        functools.partial(_proja_body, qscale=HEAD_DIM ** -0.5 * LOG2E),
        grid=(b, s // tm),
        in_specs=[
            pl.BlockSpec((None, tm, d), lambda i, j: (i, j, 0)),
            _resident((1, d)),
            _resident(w.shape),
            _resident(wvt.shape),
            pl.BlockSpec((tm, 3 * LANES), lambda i, j: (j, 0)),
        ],
        out_specs=list(specs),
        out_shape=list(shapes),
        compiler_params=_params(2),
        name="proj_a",
    )(x, g.reshape(1, d), w, wvt, tab)


def _wattn_body(bias_ref, sink_ref, q_ref, kp_ref, kc_ref, kn_ref, vp_ref, vc_ref, vn_ref, o_ref, *, nsteps):
    step = pl.program_id(1)
    lane = lax.broadcasted_iota(jnp.int32, (WINDOW, LANES), 1)
    ones = jnp.ones((MLA_DEN_ROWS, 3 * WINDOW), BF16)
    zero = jnp.zeros((WINDOW, LANES), BF16)
    def operands(blk, tile):
        ts = slice(LANES * tile, LANES * (tile + 1))
        k_blocks = ([kp_ref[:, ts]] + [kc_ref[WINDOW * i:WINDOW * (i + 1), ts] for i in range(WIN_QBLOCKS)]
                    + [kn_ref[:, ts]])
        vt_blocks = [vp_ref[ts, :]] + [vc_ref[i, ts, :] for i in range(WIN_QBLOCKS)] + [vn_ref[ts, :]]
        k_band = jnp.concatenate(k_blocks[blk:blk + 3], axis=0)
        vt_band = jnp.concatenate(vt_blocks[blk:blk + 3], axis=1)
        return k_band, vt_band

    def scores(blk, tile, half):
        variant = 1
        if blk == 0:
            variant = jnp.where(step == 0, 0, variant)
        if blk == WIN_QBLOCKS - 1:
            variant = jnp.where(step == nsteps - 1, 2, variant)
        rows = slice(WINDOW * blk, WINDOW * (blk + 1))
        q_tiles = [q_ref[rows, LANES * (A_TILE_HEADS * tile + r):LANES * (A_TILE_HEADS * tile + r + 1)]
                   for r in range(A_TILE_HEADS)]
        in_half = (lane < HEAD_DIM) if half == 0 else (lane >= HEAD_DIM)
        q_stack = jnp.concatenate([jnp.where(in_half, qt, zero) for qt in q_tiles], axis=0)
        return _dot_nt(operands(blk, tile)[0], q_stack) + bias_ref[variant]

    def attend(blk, tile, half, s):
        sink = sink_ref[2 * tile + half]
        m = jnp.maximum(jnp.max(s, axis=0, keepdims=True), sink)
        p = jnp.exp2(s - m).astype(BF16)
        vt_band = operands(blk, tile)[1]
        v_aug = jnp.concatenate([vt_band[HEAD_DIM * half:HEAD_DIM * (half + 1), :], ones], axis=0)
        o_aug = _dot(v_aug, p)
        den = o_aug[HEAD_DIM:HEAD_DIM + 1, :] + jnp.exp2(sink - m)
        return o_aug[:HEAD_DIM, :] / den

    def emit(blk, tile, halves):
        rows = slice(WINDOW * blk, WINDOW * (blk + 1))
        for r in range(A_TILE_HEADS):
            cs = slice(WINDOW * r, WINDOW * (r + 1))
            out_t = jnp.concatenate([halves[0][:, cs], halves[1][:, cs]], axis=0)
            j = A_TILE_HEADS * tile + r
            o_ref[rows, LANES * j:LANES * (j + 1)] = out_t.T.astype(BF16)

    chains = [(blk, tile, half) for blk in range(WIN_QBLOCKS) for tile in range(A_KV_TILES) for half in range(2)]
    pending, done = {}, {}
    for i in range(len(chains) + WIN_AHEAD):
        if i < len(chains):
            pending[chains[i]] = scores(*chains[i])
        if i >= WIN_AHEAD:
            blk, tile, half = chains[i - WIN_AHEAD]
            done[(blk, tile, half)] = attend(blk, tile, half, pending.pop((blk, tile, half)))
            if half == 1:
                emit(blk, tile, [done.pop((blk, tile, 0)), done.pop((blk, tile, 1))])


def _band_bias(dtype=F32):
    kj = jnp.arange(3 * WINDOW)[:, None]
    qi = (jnp.arange(3 * WINDOW) % WINDOW)[None, :]
    band = (kj - qi >= 0) & (kj - qi <= 2 * WINDOW)
    first = band & (kj >= WINDOW)
    last = band & (kj < 2 * WINDOW)
    return jnp.where(jnp.stack([first, band, last]), 0.0, NEG).astype(dtype)


def _win_attn(q, k, vt, sink):
    b, s, _ = q.shape
    nb = s // WINDOW
    nsteps = nb // WIN_QBLOCKS
    assert nb >= 2 and nb % WIN_QBLOCKS == 0
    qrows = WIN_QBLOCKS * WINDOW
    sink_rows = jnp.repeat(sink.reshape(A_KV_HEADS, 1, A_GROUP) * LOG2E, WINDOW, axis=-1)

    def halo(j, shift):
        return jnp.clip(WIN_QBLOCKS * j + shift, 0, nb - 1)

    def k_halo(shift):
        return pl.BlockSpec((None, WINDOW, A_KV_W), lambda i, j: (i, halo(j, shift), 0))

    def vt_halo(shift):
        return pl.BlockSpec((None, None, A_KV_W, WINDOW), lambda i, j: (i, halo(j, shift), 0, 0))

    return pl.pallas_call(
        functools.partial(_wattn_body, nsteps=nsteps),
        grid=(b, nsteps),
        in_specs=[
            _resident((3, 3 * WINDOW, 3 * WINDOW)),
            _resident((A_KV_HEADS, 1, 3 * WINDOW)),
            pl.BlockSpec((None, qrows, A_Q_W), lambda i, j: (i, j, 0)),
            k_halo(-1),
            pl.BlockSpec((None, qrows, A_KV_W), lambda i, j: (i, j, 0)),
            k_halo(WIN_QBLOCKS),
            vt_halo(-1),
            pl.BlockSpec((None, WIN_QBLOCKS, A_KV_W, WINDOW), lambda i, j: (i, j, 0, 0)),
            vt_halo(WIN_QBLOCKS),
        ],
        out_specs=pl.BlockSpec((None, qrows, A_Q_W), lambda i, j: (i, j, 0)),
        out_shape=jax.ShapeDtypeStruct((b, s, A_Q_W), BF16),
        compiler_params=_params(2),
        name="win_attn",
    )(_band_bias(), sink_rows, q, k, k, k, vt, vt, vt)


def _projb_body(x_ref, g_ref, win_ref, gq_ref, wq_ref, gkv_ref, wk_ref, wvt_ref, tabq_ref, tabk_ref,
                q_ref, k_ref, vt_ref, qc_ref, *, qcscale, sub):
    kb = vt_ref.shape[-1]
    subs = [slice(r0, r0 + sub) for r0 in range(0, x_ref.shape[0], sub)]

    def latents(rows):
        h = _rms(x_ref[rows, :], g_ref[...]).astype(BF16)
        proj = _dot(h, win_ref[...])
        c_q = _rms(proj[:, :Q_LORA], gq_ref[...]).astype(BF16)
        kv0 = Q_LORA
        c_kv = _rms(proj[:, kv0:kv0 + KV_LORA], gkv_ref[...]).astype(BF16)
        qc0 = kv0 + KV_LORA
        qc_ref[rows, :] = (proj[:, qc0:qc0 + MEM_WIDTH] * qcscale).astype(BF16)
        kr0 = qc0 + MEM_WIDTH
        return c_q, c_kv, proj[:, kr0:kr0 + LANES]

    def up_project(c_q, c_kv):
        q_all = _dot(c_q, wq_ref[...])
        k_all = _dot(c_kv, wk_ref[...])
        vt = _dot_nt(wvt_ref[...], c_kv)
        return q_all, k_all, vt

    def finish(rows, q_all, k_all, vt, kr_tile):
        r0 = rows.start
        vt_ref[r0 // kb, :, r0 % kb:r0 % kb + sub] = vt.astype(BF16)
        k_rope = _rope_rot_block(kr_tile, tabk_ref.at[rows, :])
        for hd in range(B_HEADS):
            hs = slice(B_HEAD_PAD * hd, B_HEAD_PAD * (hd + 1))
            q_ref[rows, hs] = _rope_rot_block(q_all[:, hs], tabq_ref.at[rows, :]).astype(BF16)
            k_ref[rows, hs] = (k_all[:, hs] + k_rope).astype(BF16)

    lat = [latents(rows) for rows in subs]
    ups = [up_project(c_q, c_kv) for c_q, c_kv, _ in lat]
    for rows, (q_all, k_all, vt), (_, _, kr_tile) in zip(subs, ups, lat):
        finish(rows, q_all, k_all, vt, kr_tile)


def _proj_b(x, g, w_in, gq, wq, gkv, wk, wvt, tabq, tabk):
    b, s, d = x.shape
    kb = min(max(MLA_KB, s // MLA_NKV), s)
    tm = min(2 * PROJB_SUB, s)
    sub = min(PROJB_SUB, tm)
    assert tm % kb == 0 and kb % sub == 0

    def out(width):
        return (jax.ShapeDtypeStruct((b, s, width), BF16),
                pl.BlockSpec((None, tm, width), lambda i, j: (i, j, 0)))

    vt = (jax.ShapeDtypeStruct((b, s // kb, B_HEADS * V_HEAD, kb), BF16),
          pl.BlockSpec((None, tm // kb, B_HEADS * V_HEAD, kb), lambda i, j: (i, j, 0, 0)))
    shapes, specs = zip(out(B_HEADS * B_HEAD_PAD), out(B_HEADS * B_HEAD_PAD), vt, out(MEM_WIDTH))
    tab_spec = pl.BlockSpec((tm, 2 * LANES), lambda i, j: (j, 0))
    return pl.pallas_call(
        functools.partial(_projb_body, qcscale=HEAD_DIM ** -0.5 * LOG2E, sub=sub),
        grid=(b, s // tm),
        in_specs=[
            pl.BlockSpec((None, tm, d), lambda i, j: (i, j, 0)),
            _resident((1, d)),
            _resident(w_in.shape),
            _resident((1, Q_LORA)),
            _resident(wq.shape),
            _resident((1, KV_LORA)),
            _resident(wk.shape),
            _resident(wvt.shape),
            tab_spec, tab_spec,
        ],
        out_specs=list(specs),
        out_shape=list(shapes),
        compiler_params=_params(2),
        name="proj_b",
    )(x, g.reshape(1, d), w_in, gq.reshape(1, -1), wq, gkv.reshape(1, -1), wk, wvt, tabq, tabk)


def _mla_body(q_ref, k_ref, vt_ref, o_ref, *scratch, kb, nkv):
    per_slot = 3 * 2
    slots = [scratch[per_slot * i:per_slot * (i + 1)] for i in range(MLA_SLOTS)]
    s_refs = [sl[0:2] for sl in slots]
    p_refs = [sl[2:4] for sl in slots]
    alpha_refs = [sl[4:6] for sl in slots]
    m_refs, acc_refs = (scratch[per_slot * MLA_SLOTS + 2 * i:per_slot * MLA_SLOTS + 2 * (i + 1)]
                        for i in range(2))
    hslices = [slice(B_HEAD_PAD * hh, B_HEAD_PAD * (hh + 1)) for hh in range(2)]
    vslices = [slice(V_HEAD * hh, V_HEAD * (hh + 1)) for hh in range(2)]
    chunks = [slice(c, c + MLA_CHUNK) for c in range(0, kb, MLA_CHUNK)]

    def scores(t, slot):
        rows = pl.ds(pl.multiple_of(t * kb, kb), kb)
        for hh, hs in enumerate(hslices):
            s_refs[slot][hh][...] = _dot_nt(k_ref[rows, hs], q_ref[:, hs])

    def softmax(slot):
        for hh in range(2):
            s_ref, p_ref = s_refs[slot][hh], p_refs[slot][hh]
            cmax = s_ref[chunks[0], :]
            for ch in chunks[1:]:
                cmax = jnp.maximum(cmax, s_ref[ch, :])
            m = m_refs[hh][...]
            m_new = jnp.maximum(m, jnp.max(cmax, axis=0, keepdims=True))
            alpha = jnp.exp2(m - m_new)
            for ch in chunks:
                p_ref[ch, :] = jnp.exp2(s_ref[ch, :] - m_new).astype(BF16)
            alpha_refs[slot][hh][...] = alpha
            m_refs[hh][...] = m_new

    def accumulate(t, slot):
        ones = jnp.ones((MLA_DEN_ROWS, kb), BF16)
        for hh, vs in enumerate(vslices):
            v_aug = jnp.concatenate([vt_ref[t, vs, :], ones], axis=0)
            acc_refs[hh][...] = (alpha_refs[slot][hh][...] * acc_refs[hh][...]
                                 + _dot(v_aug, p_refs[slot][hh][...]))

    def stage(t, phase, ahead=True, behind=True):
        if ahead:
            scores(t + MLA_AHEAD, (phase + MLA_AHEAD) % MLA_SLOTS)
        if behind:
            accumulate(t - MLA_AHEAD, (phase - MLA_AHEAD) % MLA_SLOTS)
        softmax(phase)

    lo = MLA_AHEAD
    trips = max(nkv - 2 * MLA_AHEAD, 0) // MLA_SLOTS
    tail = range(lo + trips * MLA_SLOTS, nkv)
    assert all(t + MLA_AHEAD >= nkv for t in tail)

    def init_m():
        for hh in range(2):
            m_refs[hh][...] = jnp.full(m_refs[hh].shape, NEG, F32)

    def init_acc():
        for hh in range(2):
            acc_refs[hh][...] = jnp.zeros(acc_refs[hh].shape, F32)

    def output():
        out_t = jnp.concatenate(
            [acc_refs[hh][:V_HEAD, :] / acc_refs[hh][V_HEAD:V_HEAD + 1, :] for hh in range(2)], axis=0)
        o_ref[...] = out_t.T.astype(BF16)

    def loop():
        def full_stages(i, carry):
            for j in range(MLA_SLOTS):
                stage(lo + MLA_SLOTS * i + j, (lo + j) % MLA_SLOTS)
            return carry

        lax.fori_loop(0, trips, full_stages, 0)

    fill_scores = [functools.partial(scores, t, t % MLA_SLOTS) for t in range(min(MLA_AHEAD, nkv))]
    fill_stages = [functools.partial(stage, t, t % MLA_SLOTS, ahead=t + MLA_AHEAD < nkv, behind=False)
                   for t in range(min(lo, nkv))]
    drain_stages = [functools.partial(stage, t, t % MLA_SLOTS, ahead=False, behind=t >= MLA_AHEAD) for t in tail]
    drain_accs = [functools.partial(accumulate, t, t % MLA_SLOTS) for t in range(max(nkv - MLA_AHEAD, 0), nkv)]

    def run(pieces):
        for piece in pieces:
            piece()

    n = pl.program_id(2)
    last = pl.num_programs(2) - 1

    def fill_and_loop():
        run(fill_scores)
        init_m()
        init_acc()
        run(fill_stages)
        loop()

    def drain():
        run(drain_stages)
        run(drain_accs)
        output()

    @pl.when(n == 0)
    def _():
        fill_and_loop()

    @pl.when(jnp.logical_and(n > 0, n < last))
    def _():
        drain()
        fill_and_loop()

    @pl.when(n == last)
    def _():
        drain()


def _mla_attn(q, k, vt):
    b, s, _ = q.shape
    _, nkv, _, kb = vt.shape
    qb = min(MLA_QB, s)
    nq = s // qb
    npair = B_HEADS // 2
    per_slot = ([pltpu.VMEM((kb, qb), F32)] * 2
                + [pltpu.VMEM((kb, qb), BF16)] * 2
                + [pltpu.VMEM((1, qb), F32)] * 2)
    state = ([pltpu.VMEM((1, qb), F32)] * 2
             + [pltpu.VMEM((V_HEAD + MLA_DEN_ROWS, qb), F32)] * 2)
    return pl.pallas_call(
        functools.partial(_mla_body, kb=kb, nkv=nkv),
        grid=(b, npair, nq + 1),
        in_specs=[
            pl.BlockSpec((None, qb, 2 * B_HEAD_PAD), lambda i, j, n: (i, jnp.minimum(n, nq - 1), j)),
            pl.BlockSpec((None, s, 2 * B_HEAD_PAD), lambda i, j, n: (i, 0, j)),
            pl.BlockSpec((None, nkv, 2 * V_HEAD, kb), lambda i, j, n: (i, 0, j, 0)),
        ],
        out_specs=pl.BlockSpec((None, qb, 2 * V_HEAD), lambda i, j, n: (i, jnp.maximum(n - 1, 0), j)),
        out_shape=jax.ShapeDtypeStruct((b, s, B_HEADS * V_HEAD), BF16),
        scratch_shapes=per_slot * MLA_SLOTS + state,
        compiler_params=_params(3),
        name="mla_attn",
    )(q, k, vt)


def _oproj_body(x_ref, loc_ref, qc_ref, mk_ref, mvt_ref, wo_ref, o_ref):
    sub = min(OPROJ_SUB, qc_ref.shape[0])
    lane = lax.broadcasted_iota(jnp.int32, (sub, LANES), 1)
    ones = jnp.ones((MLA_DEN_ROWS, N_MEM), BF16)
    zero = jnp.zeros((sub, LANES), BF16)
    subs = [slice(r0, r0 + sub) for r0 in range(0, qc_ref.shape[0], sub)]

    def scores(rows, hd):
        ts = slice(LANES * (hd // 2), LANES * (hd // 2 + 1))
        in_half = (lane < HEAD_DIM) if hd % 2 == 0 else (lane >= HEAD_DIM)
        return _dot_nt(mk_ref[:, ts], jnp.where(in_half, qc_ref[rows, ts], zero))

    def attend(hd, s):
        p = jnp.exp2(s - jnp.max(s, axis=0, keepdims=True)).astype(BF16)
        v_aug = jnp.concatenate([mvt_ref[HEAD_DIM * hd:HEAD_DIM * (hd + 1), :], ones], axis=0)
        o_aug = _dot(v_aug, p)
        return o_aug[:HEAD_DIM, :] / o_aug[HEAD_DIM:HEAD_DIM + 1, :]

    y_local = [_dot(loc_ref[rows, :], wo_ref[:LOCAL_W, :]) for rows in subs]
    chains = [(i, hd) for i in range(len(subs)) for hd in range(MEM_HEADS)]
    pending, heads = {}, {}
    for c in range(len(chains) + OPROJ_AHEAD):
        if c < len(chains):
            i, hd = chains[c]
            pending[(i, hd)] = scores(subs[i], hd)
        if c >= OPROJ_AHEAD:
            i, hd = chains[c - OPROJ_AHEAD]
            heads[(i, hd)] = attend(hd, pending.pop((i, hd)))
            if hd == MEM_HEADS - 1:
                tiles = [jnp.concatenate([heads.pop((i, 2 * t)), heads.pop((i, 2 * t + 1))], axis=0).T.astype(BF16)
                         for t in range(MEM_HEADS // 2)]
                cross = jnp.concatenate(tiles, axis=-1)
                o_ref[subs[i], :] = x_ref[subs[i], :] + (y_local[i] + _dot(cross, wo_ref[LOCAL_W:, :]))


def _out_proj(x, local, qc, mk, mvt, wo):
    b, s, d = x.shape
    tm = min(2 * OPROJ_SUB, s)

    def tok(width):
        return pl.BlockSpec((None, tm, width), lambda i, j: (i, j, 0))

    return pl.pallas_call(
        _oproj_body,
        grid=(b, s // tm),
        in_specs=[tok(d), tok(LOCAL_W), tok(MEM_WIDTH),
                  pl.BlockSpec((None, N_MEM, MEM_WIDTH), lambda i, j: (i, 0, 0)),
                  pl.BlockSpec((None, MEM_WIDTH, N_MEM), lambda i, j: (i, 0, 0)),
                  _resident(wo.shape)],
        out_specs=tok(d),
        out_shape=jax.ShapeDtypeStruct((b, s, d), F32),
        compiler_params=_params(2),
        name="out_proj",
    )(x, local, qc, mk, mvt, wo)


def _prep_a(a_w_in):
    nl, d, _ = a_w_in.shape
    order = jnp.array(A_HEAD_ORDER)
    q = a_w_in[:, :, :A_Q_W].reshape(nl, d, A_Q_HEADS, HEAD_DIM)[:, :, order].reshape(nl, d, A_Q_W)
    k = a_w_in[:, :, A_Q_W:A_Q_W + A_KV_W]
    v = a_w_in[:, :, A_Q_W + A_KV_W:A_Q_W + 2 * A_KV_W]
    qc = a_w_in[:, :, A_Q_W + 2 * A_KV_W:]
    w = jnp.concatenate([q, k, qc], axis=-1).astype(BF16)
    return w, jnp.transpose(v, (0, 2, 1)).astype(BF16)


def _prep_w_o(w_o):
    depth, _, d = w_o.shape
    order = jnp.array(A_HEAD_ORDER)
    local = w_o[:, :LOCAL_W].reshape(depth, A_Q_HEADS, HEAD_DIM, d)
    local = jnp.where((jnp.arange(depth) % 2 == 0)[:, None, None, None], local[:, order], local)
    return jnp.concatenate([local.reshape(depth, LOCAL_W, d), w_o[:, LOCAL_W:]], axis=1).astype(BF16)


def _prep_b(b_w_in, b_w_q_up, b_w_kv_up):
    nl, d, _ = b_w_in.shape
    c_q = b_w_in[:, :, :Q_LORA]
    c_kv = b_w_in[:, :, Q_LORA:Q_LORA + KV_LORA]
    k_r = b_w_in[:, :, Q_LORA + KV_LORA:Q_LORA + KV_LORA + QK_ROPE]
    qc = b_w_in[:, :, Q_LORA + KV_LORA + QK_ROPE:]
    kr_tile = jnp.pad(_with_rotate_half(k_r), ((0, 0), (0, 0), (QK_NOPE, 0)))
    w_in = jnp.concatenate([c_q, c_kv, qc, kr_tile], axis=-1).astype(BF16)
    wq = b_w_q_up.reshape(nl, Q_LORA, B_HEADS, B_QK)
    wq = jnp.concatenate([wq[..., :QK_NOPE], _with_rotate_half(wq[..., QK_NOPE:])], axis=-1)
    wq = wq.reshape(nl, Q_LORA, B_HEADS * B_HEAD_PAD).astype(BF16)
    wkv = b_w_kv_up.reshape(nl, KV_LORA, B_HEADS, QK_NOPE + V_HEAD)
    wk = jnp.pad(wkv[..., :QK_NOPE], ((0, 0), (0, 0), (0, 0), (0, B_HEAD_PAD - QK_NOPE)))
    wk = wk.reshape(nl, KV_LORA, B_HEADS * B_HEAD_PAD).astype(BF16)
    wvt = jnp.transpose(wkv[..., QK_NOPE:].reshape(nl, KV_LORA, B_HEADS * V_HEAD), (0, 2, 1)).astype(BF16)
    return w_in, wq, wk, wvt


def _trunk(x, mem, w):
    b, s, d = x.shape
    depth = w["mix_norm"].shape[0]
    tab_a = _rope_table(s, HEAD_DIM, 0, HEAD_DIM, 1.0, False)
    tab_bq = _rope_rot_table(s, B_QK ** -0.5 * LOG2E, True)
    tab_bk = _rope_rot_table(s, 1.0, False)
    for i in range(depth):
        x = _ffn(x.reshape(b * s, d), w["ffn1_norm"][i], w["ffn1_w_gu"], w["ffn1_w_down"], i).reshape(b, s, d)
        mk, mvt = _mem_kv(mem, w["mem_norm"][i], w["w_mem_k"][i], w["w_mem_vt"][i])
        j = i // 2
        if i % 2 == 0:
            q, k, vt, qc = _proj_a(x, w["mix_norm"][i], w["a_w_in"][j], w["a_wvt"][j], tab_a)
            local = _win_attn(q, k, vt, w["a_sink"][j])
        else:
            q, k, v, qc = _proj_b(x, w["mix_norm"][i], w["b_w_in"][j], w["b_q_norm"][j], w["b_wq"][j],
                                  w["b_kv_norm"][j], w["b_wk"][j], w["b_wvt"][j], tab_bq, tab_bk)
            local = _mla_attn(q, k, v)
        x = _out_proj(x, local, qc, mk, mvt, w["w_o"][i])
        g_final = w["final_norm"] if i == depth - 1 else None
        x = _ffn(x.reshape(b * s, d), w["ffn2_norm"][i], w["ffn2_w_gu"], w["ffn2_w_down"], i,
                 g_final).reshape(b, s, d)
    return x


def kernel(x_prompt, x_sample, mem_prompt, mem_sample, ffn1_norm, ffn1_w_gu, ffn1_w_down, mix_norm,
           mem_norm, w_mem_kv, a_w_in, a_sink, b_w_in, b_q_norm, b_w_q_up, b_kv_norm, b_w_kv_up,
           w_o, ffn2_norm, ffn2_w_gu, ffn2_w_down, final_norm):
    w = {
        "ffn1_norm": ffn1_norm, "ffn2_norm": ffn2_norm, "mix_norm": mix_norm, "mem_norm": mem_norm,
        "w_mem_k": w_mem_kv[:, :, :MEM_WIDTH].astype(BF16),
        "w_mem_vt": jnp.transpose(w_mem_kv[:, :, MEM_WIDTH:], (0, 2, 1)).astype(BF16), "a_sink": a_sink,
        "b_q_norm": b_q_norm, "b_kv_norm": b_kv_norm, "w_o": _prep_w_o(w_o),
        "final_norm": final_norm,
    }
    w["ffn1_w_gu"], w["ffn1_w_down"] = ffn1_w_gu.astype(BF16), ffn1_w_down.astype(BF16)
    w["ffn2_w_gu"], w["ffn2_w_down"] = ffn2_w_gu.astype(BF16), ffn2_w_down.astype(BF16)
    w["a_w_in"], w["a_wvt"] = _prep_a(a_w_in)
    w["b_w_in"], w["b_wq"], w["b_wk"], w["b_wvt"] = _prep_b(b_w_in, b_w_q_up, b_w_kv_up)
    return (_trunk(x_prompt, mem_prompt, w), _trunk(x_sample, mem_sample, w))
```

```python
import functools
import math

import jax
import jax.numpy as jnp
from jax import lax
from jax.experimental import pallas as pl
from jax.experimental.pallas import tpu as pltpu

D_MODEL = 1024
HEAD_DIM = 64
ROPE_THETA = 10000.0
NORM_EPS = 1e-6
D_FF = 2816
N_MEM = 256
MEM_HEADS = 4
MEM_WIDTH = MEM_HEADS * HEAD_DIM
A_Q_HEADS = 12
A_KV_HEADS = 4
A_GROUP = A_Q_HEADS // A_KV_HEADS
WINDOW = 128
A_Q_W = A_Q_HEADS * HEAD_DIM
A_KV_W = A_KV_HEADS * HEAD_DIM
B_HEADS = 12
Q_LORA = 384
KV_LORA = 256
QK_NOPE = 64
QK_ROPE = 32
V_HEAD = 64
B_QK = QK_NOPE + QK_ROPE
LOCAL_W = A_Q_W
NEG = -1e30
LOG2E = math.log2(math.e)

LANES = 128
B_HEAD_PAD = LANES
FF_CHUNK = 256
PROJA_SUB = 512
PROJB_SUB = 512
OPROJ_SUB = 512
OPROJ_AHEAD = 4
MLA_QB = 512
MLA_KB = 512
MLA_NKV = 8
MLA_CHUNK = 32
MLA_DEN_ROWS = 16
MLA_AHEAD = 2
MLA_SLOTS = 2 * MLA_AHEAD
VMEM_LIMIT = 56 * 1024 * 1024

F32 = jnp.float32
BF16 = jnp.bfloat16


def _params(n_axes):
    return pltpu.CompilerParams(dimension_semantics=("arbitrary",) * n_axes,
                                vmem_limit_bytes=VMEM_LIMIT)


def _rms(x, g):
    return x * lax.rsqrt(jnp.mean(x * x, axis=-1, keepdims=True) + NORM_EPS) * g


def _dot(a, b):
    return jnp.dot(a, b, preferred_element_type=F32)


def _dot_nt(a, b):
    return lax.dot_general(a, b, (((1,), (1,)), ((), ())), preferred_element_type=F32)


def _resident(shape):
    zeros = (0,) * len(shape)
    return pl.BlockSpec(shape, lambda *_: zeros, pipeline_mode=pl.Buffered(1))


def _ffn_body(*refs, final):
    if final:
        x_ref, g_ref, wgu_ref, wd_ref, gf_ref, o_ref, h_ref = refs
    else:
        x_ref, g_ref, wgu_ref, wd_ref, o_ref, h_ref = refs
    h_ref[...] = _rms(x_ref[...], g_ref[...]).astype(BF16)
    nch = D_FF // FF_CHUNK

    def gate_up(c):
        h = h_ref[...]
        gate = _dot(h, wgu_ref[:, FF_CHUNK * c:FF_CHUNK * (c + 1)])
        up = _dot(h, wgu_ref[:, D_FF + FF_CHUNK * c:D_FF + FF_CHUNK * (c + 1)])
        return gate, up

    acc = None
    nxt = gate_up(0)
    for c in range(nch):
        gate, up = nxt
        if c + 1 < nch:
            nxt = gate_up(c + 1)
        a = (gate / (1.0 + jnp.exp(-gate)) * up).astype(BF16)
        down = _dot(a, wd_ref[FF_CHUNK * c:FF_CHUNK * (c + 1), :])
        acc = down if acc is None else acc + down
    y = x_ref[...] + 0.5 * acc
    if final:
        y = _rms(y, gf_ref[...])
    o_ref[...] = y


def _layer_resident(stacked, layer):
    zeros = (0,) * (stacked.ndim - 1)
    return pl.BlockSpec((None,) + stacked.shape[1:], lambda *_: (layer,) + zeros,
                        pipeline_mode=pl.Buffered(1))


def _ffn(x, g, w_gu, w_down, layer, g_final=None):
    t, d = x.shape
    tm = min(1024, t)
    final = g_final is not None
    in_specs = [
        pl.BlockSpec((tm, d), lambda i: (i, 0)),
        _resident((1, d)),
        _layer_resident(w_gu, layer),
        _layer_resident(w_down, layer),
    ]
    args = [x, g.reshape(1, d), w_gu, w_down]
    if final:
        in_specs.append(_resident((1, d)))
        args.append(g_final.reshape(1, d))
    return pl.pallas_call(
        functools.partial(_ffn_body, final=final),
        grid=(t // tm,),
        in_specs=in_specs,
        out_specs=pl.BlockSpec((tm, d), lambda i: (i, 0)),
        out_shape=jax.ShapeDtypeStruct((t, d), F32),
        scratch_shapes=[pltpu.VMEM((tm, d), BF16)],
        compiler_params=_params(1),
        name="ffn_final" if final else "ffn",
    )(*args)


def _memkv_body(mem_ref, g_ref, wk_ref, wvt_ref, mk_ref, mvt_ref):
    h = _rms(mem_ref[...], g_ref[...]).astype(BF16)
    mk_ref[...] = _dot(h, wk_ref[...]).astype(BF16)
    mvt_ref[...] = _dot_nt(wvt_ref[...], h).astype(BF16)


def _mem_kv(mem, g, wk, wvt):
    b, n, d = mem.shape
    return pl.pallas_call(
        _memkv_body,
        grid=(b,),
        in_specs=[pl.BlockSpec((None, n, d), lambda i: (i, 0, 0)), _resident((1, d)),
                  _resident(wk.shape), _resident(wvt.shape)],
        out_specs=[pl.BlockSpec((None, n, MEM_WIDTH), lambda i: (i, 0, 0)),
                   pl.BlockSpec((None, MEM_WIDTH, n), lambda i: (i, 0, 0))],
        out_shape=[jax.ShapeDtypeStruct((b, n, MEM_WIDTH), BF16),
                   jax.ShapeDtypeStruct((b, MEM_WIDTH, n), BF16)],
        compiler_params=_params(1),
        name="mem_kv",
    )(mem, g.reshape(1, d), wk, wvt)


def _rope_block(xb, tab_ref, shift):
    c = tab_ref[:, 0:LANES]
    s_plus = tab_ref[:, LANES:2 * LANES]
    s_minus = tab_ref[:, 2 * LANES:3 * LANES]
    return (xb * c + pltpu.roll(xb, shift, 1) * s_plus
            + pltpu.roll(xb, LANES - shift, 1) * s_minus)


def _rope_table(seq, dim, lane_start, period, scale, pass_through):
    half = dim // 2
    inv = 1.0 / (ROPE_THETA ** (jnp.arange(0, dim, 2, dtype=F32) / dim))
    ang = jnp.arange(seq, dtype=F32)[:, None] * inv[None, :]
    cos, sin = jnp.cos(ang), jnp.sin(ang)
    zeros_h = jnp.zeros((seq, half), F32)
    lead = jnp.full((seq, lane_start), 1.0 if pass_through else 0.0, F32)
    lead0 = jnp.zeros((seq, lane_start), F32)
    tail0 = jnp.zeros((seq, period - lane_start - dim), F32)
    reps = LANES // period
    c = jnp.tile(jnp.concatenate([lead, cos, cos, tail0], -1), (1, reps))
    s_plus = jnp.tile(jnp.concatenate([lead0, zeros_h, sin, tail0], -1), (1, reps))
    s_minus = jnp.tile(jnp.concatenate([lead0, -sin, zeros_h, tail0], -1), (1, reps))
    return jnp.concatenate([c, s_plus, s_minus], -1) * scale


def _rope_rot_block(xb, tab_ref):
    c = tab_ref[:, 0:LANES]
    s = tab_ref[:, LANES:2 * LANES]
    return xb * c + pltpu.roll(xb, LANES - QK_ROPE, 1) * s


def _rope_rot_table(seq, scale, pass_through):
    inv = 1.0 / (ROPE_THETA ** (jnp.arange(0, QK_ROPE, 2, dtype=F32) / QK_ROPE))
    ang = jnp.arange(seq, dtype=F32)[:, None] * inv[None, :]
    cos, sin = jnp.cos(ang), jnp.sin(ang)
    lead = jnp.full((seq, QK_NOPE), 1.0 if pass_through else 0.0, F32)
    lead0 = jnp.zeros((seq, QK_NOPE), F32)
    tail0 = jnp.zeros((seq, LANES - QK_NOPE - QK_ROPE), F32)
    c = jnp.concatenate([lead, cos, cos, tail0], -1)
    s = jnp.concatenate([lead0, sin, sin, tail0], -1)
    return jnp.concatenate([c, s], -1) * scale


def _with_rotate_half(w_rope):
    x1, x2 = w_rope[..., :QK_ROPE // 2], w_rope[..., QK_ROPE // 2:]
    return jnp.concatenate([w_rope, -x2, x1], axis=-1)


A_HEAD_ORDER = (0, 3, 1, 4, 2, 5, 6, 9, 7, 10, 8, 11)
A_KV_TILES = A_KV_W // LANES
A_TILE_HEADS = A_GROUP
WIN_QBLOCKS = 4
WIN_AHEAD = 3


def _proja_body(x_ref, g_ref, w_ref, wvt_ref, tab_ref, q_ref, k_ref, vt_ref, qc_ref, *, qscale, sub):
    subs = [slice(r0, r0 + sub) for r0 in range(0, x_ref.shape[0], sub)]
    nq = A_Q_W // LANES
    nk = A_KV_W // LANES

    def project(rows):
        h = _rms(x_ref[rows, :], g_ref[...]).astype(BF16)
        proj = _dot(h, w_ref[...])
        vt = _dot_nt(wvt_ref[...], h)
        return proj, vt

    def finish(rows, proj, vt):
        for i in range(sub // WINDOW):
            vt_ref[rows.start // WINDOW + i] = vt[:, WINDOW * i:WINDOW * (i + 1)].astype(BF16)
        qc_ref[rows, :] = (proj[:, A_Q_W + A_KV_W:] * qscale).astype(BF16)
        tab = tab_ref.at[rows, :]
        for j in range(nq + nk):
            rb = _rope_block(proj[:, LANES * j:LANES * (j + 1)], tab, HEAD_DIM // 2)
            if j < nq:
                q_ref[rows, LANES * j:LANES * (j + 1)] = (rb * qscale).astype(BF16)
            else:
                k_ref[rows, LANES * (j - nq):LANES * (j - nq + 1)] = rb.astype(BF16)

    projected = [project(rows) for rows in subs]
    for rows, (proj, vt) in zip(subs, projected):
        finish(rows, proj, vt)


def _proj_a(x, g, w, wvt, tab):
    b, s, d = x.shape
    tm = min(2 * PROJA_SUB, s)
    sub = min(PROJA_SUB, tm)
    nblk = tm // WINDOW

    def out(width):
        return (jax.ShapeDtypeStruct((b, s, width), BF16),
                pl.BlockSpec((None, tm, width), lambda i, j: (i, j, 0)))

    vt = (jax.ShapeDtypeStruct((b, s // WINDOW, A_KV_W, WINDOW), BF16),
          pl.BlockSpec((None, nblk, A_KV_W, WINDOW), lambda i, j: (i, j, 0, 0)))
    shapes, specs = zip(out(A_Q_W), out(A_KV_W), vt, out(MEM_WIDTH))
    return pl.pallas_call(
        functools.partial(_proja_body, qscale=HEAD_DIM ** -0.5 * LOG2E, sub=sub),
        grid=(b, s // tm),
        in_specs=[
            pl.BlockSpec((None, tm, d), lambda i, j: (i, j, 0)),
            _resident((1, d)),
            _resident(w.shape),
            _resident(wvt.shape),
            pl.BlockSpec((tm, 3 * LANES), lambda i, j: (j, 0)),
        ],
        out_specs=list(specs),
        out_shape=list(shapes),
        compiler_params=_params(2),
        name="proj_a",
    )(x, g.reshape(1, d), w, wvt, tab)


def _wattn_body(bias_ref, sink_ref, q_ref, kp_ref, kc_ref, kn_ref, vp_ref, vc_ref, vn_ref, o_ref, *, nsteps):
    step = pl.program_id(1)
    lane = lax.broadcasted_iota(jnp.int32, (WINDOW, LANES), 1)
    ones = jnp.ones((MLA_DEN_ROWS, 3 * WINDOW), BF16)
    zero = jnp.zeros((WINDOW, LANES), BF16)
    def operands(blk, tile):
        ts = slice(LANES * tile, LANES * (tile + 1))
        k_blocks = ([kp_ref[:, ts]] + [kc_ref[WINDOW * i:WINDOW * (i + 1), ts] for i in range(WIN_QBLOCKS)]
                    + [kn_ref[:, ts]])
        vt_blocks = [vp_ref[ts, :]] + [vc_ref[i, ts, :] for i in range(WIN_QBLOCKS)] + [vn_ref[ts, :]]
        k_band = jnp.concatenate(k_blocks[blk:blk + 3], axis=0)
        vt_band = jnp.concatenate(vt_blocks[blk:blk + 3], axis=1)
        return k_band, vt_band

    def scores(blk, tile, half):
        variant = 1
        if blk == 0:
            variant = jnp.where(step == 0, 0, variant)
        if blk == WIN_QBLOCKS - 1:
            variant = jnp.where(step == nsteps - 1, 2, variant)
        rows = slice(WINDOW * blk, WINDOW * (blk + 1))
        q_tiles = [q_ref[rows, LANES * (A_TILE_HEADS * tile + r):LANES * (A_TILE_HEADS * tile + r + 1)]
                   for r in range(A_TILE_HEADS)]
        in_half = (lane < HEAD_DIM) if half == 0 else (lane >= HEAD_DIM)
        q_stack = jnp.concatenate([jnp.where(in_half, qt, zero) for qt in q_tiles], axis=0)
        return _dot_nt(operands(blk, tile)[0], q_stack) + bias_ref[variant]

    def attend(blk, tile, half, s):
        sink = sink_ref[2 * tile + half]
        m = jnp.maximum(jnp.max(s, axis=0, keepdims=True), sink)
        p = jnp.exp2(s - m).astype(BF16)
        vt_band = operands(blk, tile)[1]
        v_aug = jnp.concatenate([vt_band[HEAD_DIM * half:HEAD_DIM * (half + 1), :], ones], axis=0)
        o_aug = _dot(v_aug, p)
        den = o_aug[HEAD_DIM:HEAD_DIM + 1, :] + jnp.exp2(sink - m)
        return o_aug[:HEAD_DIM, :] / den

    def emit(blk, tile, halves):
        rows = slice(WINDOW * blk, WINDOW * (blk + 1))
        for r in range(A_TILE_HEADS):
            cs = slice(WINDOW * r, WINDOW * (r + 1))
            out_t = jnp.concatenate([halves[0][:, cs], halves[1][:, cs]], axis=0)
            j = A_TILE_HEADS * tile + r
            o_ref[rows, LANES * j:LANES * (j + 1)] = out_t.T.astype(BF16)

    chains = [(blk, tile, half) for blk in range(WIN_QBLOCKS) for tile in range(A_KV_TILES) for half in range(2)]
    pending, done = {}, {}
    for i in range(len(chains) + WIN_AHEAD):
        if i < len(chains):
            pending[chains[i]] = scores(*chains[i])
        if i >= WIN_AHEAD:
            blk, tile, half = chains[i - WIN_AHEAD]
            done[(blk, tile, half)] = attend(blk, tile, half, pending.pop((blk, tile, half)))
            if half == 1:
                emit(blk, tile, [done.pop((blk, tile, 0)), done.pop((blk, tile, 1))])


def _band_bias(dtype=F32):
    kj = jnp.arange(3 * WINDOW)[:, None]
    qi = (jnp.arange(3 * WINDOW) % WINDOW)[None, :]
    band = (kj - qi >= 0) & (kj - qi <= 2 * WINDOW)
    first = band & (kj >= WINDOW)
    last = band & (kj < 2 * WINDOW)
    return jnp.where(jnp.stack([first, band, last]), 0.0, NEG).astype(dtype)


def _win_attn(q, k, vt, sink):
    b, s, _ = q.shape
    nb = s // WINDOW
    nsteps = nb // WIN_QBLOCKS
    assert nb >= 2 and nb % WIN_QBLOCKS == 0
    qrows = WIN_QBLOCKS * WINDOW
    sink_rows = jnp.repeat(sink.reshape(A_KV_HEADS, 1, A_GROUP) * LOG2E, WINDOW, axis=-1)

    def halo(j, shift):
        return jnp.clip(WIN_QBLOCKS * j + shift, 0, nb - 1)

    def k_halo(shift):
        return pl.BlockSpec((None, WINDOW, A_KV_W), lambda i, j: (i, halo(j, shift), 0))

    def vt_halo(shift):
        return pl.BlockSpec((None, None, A_KV_W, WINDOW), lambda i, j: (i, halo(j, shift), 0, 0))

    return pl.pallas_call(
        functools.partial(_wattn_body, nsteps=nsteps),
        grid=(b, nsteps),
        in_specs=[
            _resident((3, 3 * WINDOW, 3 * WINDOW)),
            _resident((A_KV_HEADS, 1, 3 * WINDOW)),
            pl.BlockSpec((None, qrows, A_Q_W), lambda i, j: (i, j, 0)),
            k_halo(-1),
            pl.BlockSpec((None, qrows, A_KV_W), lambda i, j: (i, j, 0)),
            k_halo(WIN_QBLOCKS),
            vt_halo(-1),
            pl.BlockSpec((None, WIN_QBLOCKS, A_KV_W, WINDOW), lambda i, j: (i, j, 0, 0)),
            vt_halo(WIN_QBLOCKS),
        ],
        out_specs=pl.BlockSpec((None, qrows, A_Q_W), lambda i, j: (i, j, 0)),
        out_shape=jax.ShapeDtypeStruct((b, s, A_Q_W), BF16),
        compiler_params=_params(2),
        name="win_attn",
    )(_band_bias(), sink_rows, q, k, k, k, vt, vt, vt)


def _projb_body(x_ref, g_ref, win_ref, gq_ref, wq_ref, gkv_ref, wk_ref, wvt_ref, tabq_ref, tabk_ref,
                q_ref, k_ref, vt_ref, qc_ref, *, qcscale, sub):
    kb = vt_ref.shape[-1]
    subs = [slice(r0, r0 + sub) for r0 in range(0, x_ref.shape[0], sub)]

    def latents(rows):
        h = _rms(x_ref[rows, :], g_ref[...]).astype(BF16)
        proj = _dot(h, win_ref[...])
        c_q = _rms(proj[:, :Q_LORA], gq_ref[...]).astype(BF16)
        kv0 = Q_LORA
        c_kv = _rms(proj[:, kv0:kv0 + KV_LORA], gkv_ref[...]).astype(BF16)
        qc0 = kv0 + KV_LORA
        qc_ref[rows, :] = (proj[:, qc0:qc0 + MEM_WIDTH] * qcscale).astype(BF16)
        kr0 = qc0 + MEM_WIDTH
        return c_q, c_kv, proj[:, kr0:kr0 + LANES]

    def up_project(c_q, c_kv):
        q_all = _dot(c_q, wq_ref[...])
        k_all = _dot(c_kv, wk_ref[...])
        vt = _dot_nt(wvt_ref[...], c_kv)
        return q_all, k_all, vt

    def finish(rows, q_all, k_all, vt, kr_tile):
        r0 = rows.start
        vt_ref[r0 // kb, :, r0 % kb:r0 % kb + sub] = vt.astype(BF16)
        k_rope = _rope_rot_block(kr_tile, tabk_ref.at[rows, :])
        for hd in range(B_HEADS):
            hs = slice(B_HEAD_PAD * hd, B_HEAD_PAD * (hd + 1))
            q_ref[rows, hs] = _rope_rot_block(q_all[:, hs], tabq_ref.at[rows, :]).astype(BF16)
            k_ref[rows, hs] = (k_all[:, hs] + k_rope).astype(BF16)

    lat = [latents(rows) for rows in subs]
    ups = [up_project(c_q, c_kv) for c_q, c_kv, _ in lat]
    for rows, (q_all, k_all, vt), (_, _, kr_tile) in zip(subs, ups, lat):
        finish(rows, q_all, k_all, vt, kr_tile)


def _proj_b(x, g, w_in, gq, wq, gkv, wk, wvt, tabq, tabk):
    b, s, d = x.shape
    kb = min(max(MLA_KB, s // MLA_NKV), s)
    tm = min(2 * PROJB_SUB, s)
    sub = min(PROJB_SUB, tm)
    assert tm % kb == 0 and kb % sub == 0

    def out(width):
        return (jax.ShapeDtypeStruct((b, s, width), BF16),
                pl.BlockSpec((None, tm, width), lambda i, j: (i, j, 0)))

    vt = (jax.ShapeDtypeStruct((b, s // kb, B_HEADS * V_HEAD, kb), BF16),
          pl.BlockSpec((None, tm // kb, B_HEADS * V_HEAD, kb), lambda i, j: (i, j, 0, 0)))
    shapes, specs = zip(out(B_HEADS * B_HEAD_PAD), out(B_HEADS * B_HEAD_PAD), vt, out(MEM_WIDTH))
    tab_spec = pl.BlockSpec((tm, 2 * LANES), lambda i, j: (j, 0))
    return pl.pallas_call(
        functools.partial(_projb_body, qcscale=HEAD_DIM ** -0.5 * LOG2E, sub=sub),
        grid=(b, s // tm),
        in_specs=[
            pl.BlockSpec((None, tm, d), lambda i, j: (i, j, 0)),
            _resident((1, d)),
            _resident(w_in.shape),
            _resident((1, Q_LORA)),
            _resident(wq.shape),
            _resident((1, KV_LORA)),
            _resident(wk.shape),
            _resident(wvt.shape),
            tab_spec, tab_spec,
        ],
        out_specs=list(specs),
        out_shape=list(shapes),
        compiler_params=_params(2),
        name="proj_b",
    )(x, g.reshape(1, d), w_in, gq.reshape(1, -1), wq, gkv.reshape(1, -1), wk, wvt, tabq, tabk)


def _mla_body(q_ref, k_ref, vt_ref, o_ref, *scratch, kb, nkv):
    per_slot = 3 * 2
    slots = [scratch[per_slot * i:per_slot * (i + 1)] for i in range(MLA_SLOTS)]
    s_refs = [sl[0:2] for sl in slots]
    p_refs = [sl[2:4] for sl in slots]
    alpha_refs = [sl[4:6] for sl in slots]
    m_refs, acc_refs = (scratch[per_slot * MLA_SLOTS + 2 * i:per_slot * MLA_SLOTS + 2 * (i + 1)]
                        for i in range(2))
    hslices = [slice(B_HEAD_PAD * hh, B_HEAD_PAD * (hh + 1)) for hh in range(2)]
    vslices = [slice(V_HEAD * hh, V_HEAD * (hh + 1)) for hh in range(2)]
    chunks = [slice(c, c + MLA_CHUNK) for c in range(0, kb, MLA_CHUNK)]

    def scores(t, slot):
        rows = pl.ds(pl.multiple_of(t * kb, kb), kb)
        for hh, hs in enumerate(hslices):
            s_refs[slot][hh][...] = _dot_nt(k_ref[rows, hs], q_ref[:, hs])

    def softmax(slot):
        for hh in range(2):
            s_ref, p_ref = s_refs[slot][hh], p_refs[slot][hh]
            cmax = s_ref[chunks[0], :]
            for ch in chunks[1:]:
                cmax = jnp.maximum(cmax, s_ref[ch, :])
            m = m_refs[hh][...]
            m_new = jnp.maximum(m, jnp.max(cmax, axis=0, keepdims=True))
            alpha = jnp.exp2(m - m_new)
            for ch in chunks:
                p_ref[ch, :] = jnp.exp2(s_ref[ch, :] - m_new).astype(BF16)
            alpha_refs[slot][hh][...] = alpha
            m_refs[hh][...] = m_new

    def accumulate(t, slot):
        ones = jnp.ones((MLA_DEN_ROWS, kb), BF16)
        for hh, vs in enumerate(vslices):
            v_aug = jnp.concatenate([vt_ref[t, vs, :], ones], axis=0)
            acc_refs[hh][...] = (alpha_refs[slot][hh][...] * acc_refs[hh][...]
                                 + _dot(v_aug, p_refs[slot][hh][...]))

    def stage(t, phase, ahead=True, behind=True):
        if ahead:
            scores(t + MLA_AHEAD, (phase + MLA_AHEAD) % MLA_SLOTS)
        if behind:
            accumulate(t - MLA_AHEAD, (phase - MLA_AHEAD) % MLA_SLOTS)
        softmax(phase)

    lo = MLA_AHEAD
    trips = max(nkv - 2 * MLA_AHEAD, 0) // MLA_SLOTS
    tail = range(lo + trips * MLA_SLOTS, nkv)
    assert all(t + MLA_AHEAD >= nkv for t in tail)

    def init_m():
        for hh in range(2):
            m_refs[hh][...] = jnp.full(m_refs[hh].shape, NEG, F32)

    def init_acc():
        for hh in range(2):
            acc_refs[hh][...] = jnp.zeros(acc_refs[hh].shape, F32)

    def output():
        out_t = jnp.concatenate(
            [acc_refs[hh][:V_HEAD, :] / acc_refs[hh][V_HEAD:V_HEAD + 1, :] for hh in range(2)], axis=0)
        o_ref[...] = out_t.T.astype(BF16)

    def loop():
        def full_stages(i, carry):
            for j in range(MLA_SLOTS):
                stage(lo + MLA_SLOTS * i + j, (lo + j) % MLA_SLOTS)
            return carry

        lax.fori_loop(0, trips, full_stages, 0)

    fill_scores = [functools.partial(scores, t, t % MLA_SLOTS) for t in range(min(MLA_AHEAD, nkv))]
    fill_stages = [functools.partial(stage, t, t % MLA_SLOTS, ahead=t + MLA_AHEAD < nkv, behind=False)
                   for t in range(min(lo, nkv))]
    drain_stages = [functools.partial(stage, t, t % MLA_SLOTS, ahead=False, behind=t >= MLA_AHEAD) for t in tail]
    drain_accs = [functools.partial(accumulate, t, t % MLA_SLOTS) for t in range(max(nkv - MLA_AHEAD, 0), nkv)]

    def run(pieces):
        for piece in pieces:
            piece()

    n = pl.program_id(2)
    last = pl.num_programs(2) - 1

    def fill_and_loop():
        run(fill_scores)
        init_m()
        init_acc()
        run(fill_stages)
        loop()

    def drain():
        run(drain_stages)
        run(drain_accs)
        output()

    @pl.when(n == 0)
    def _():
        fill_and_loop()

    @pl.when(jnp.logical_and(n > 0, n < last))
    def _():
        drain()
        fill_and_loop()

    @pl.when(n == last)
    def _():
        drain()


def _mla_attn(q, k, vt):
    b, s, _ = q.shape
    _, nkv, _, kb = vt.shape
    qb = min(MLA_QB, s)
    nq = s // qb
    npair = B_HEADS // 2
    per_slot = ([pltpu.VMEM((kb, qb), F32)] * 2
                + [pltpu.VMEM((kb, qb), BF16)] * 2
                + [pltpu.VMEM((1, qb), F32)] * 2)
    state = ([pltpu.VMEM((1, qb), F32)] * 2
             + [pltpu.VMEM((V_HEAD + MLA_DEN_ROWS, qb), F32)] * 2)
    return pl.pallas_call(
        functools.partial(_mla_body, kb=kb, nkv=nkv),
        grid=(b, npair, nq + 1),
        in_specs=[
            pl.BlockSpec((None, qb, 2 * B_HEAD_PAD), lambda i, j, n: (i, jnp.minimum(n, nq - 1), j)),
            pl.BlockSpec((None, s, 2 * B_HEAD_PAD), lambda i, j, n: (i, 0, j)),
            pl.BlockSpec((None, nkv, 2 * V_HEAD, kb), lambda i, j, n: (i, 0, j, 0)),
        ],
        out_specs=pl.BlockSpec((None, qb, 2 * V_HEAD), lambda i, j, n: (i, jnp.maximum(n - 1, 0), j)),
        out_shape=jax.ShapeDtypeStruct((b, s, B_HEADS * V_HEAD), BF16),
        scratch_shapes=per_slot * MLA_SLOTS + state,
        compiler_params=_params(3),
        name="mla_attn",
    )(q, k, vt)


def _oproj_body(x_ref, loc_ref, qc_ref, mk_ref, mvt_ref, wo_ref, o_ref):
    sub = min(OPROJ_SUB, qc_ref.shape[0])
    lane = lax.broadcasted_iota(jnp.int32, (sub, LANES), 1)
    ones = jnp.ones((MLA_DEN_ROWS, N_MEM), BF16)
    zero = jnp.zeros((sub, LANES), BF16)
    subs = [slice(r0, r0 + sub) for r0 in range(0, qc_ref.shape[0], sub)]

    def scores(rows, hd):
        ts = slice(LANES * (hd // 2), LANES * (hd // 2 + 1))
        in_half = (lane < HEAD_DIM) if hd % 2 == 0 else (lane >= HEAD_DIM)
        return _dot_nt(mk_ref[:, ts], jnp.where(in_half, qc_ref[rows, ts], zero))

    def attend(hd, s):
        p = jnp.exp2(s - jnp.max(s, axis=0, keepdims=True)).astype(BF16)
        v_aug = jnp.concatenate([mvt_ref[HEAD_DIM * hd:HEAD_DIM * (hd + 1), :], ones], axis=0)
        o_aug = _dot(v_aug, p)
        return o_aug[:HEAD_DIM, :] / o_aug[HEAD_DIM:HEAD_DIM + 1, :]

    y_local = [_dot(loc_ref[rows, :], wo_ref[:LOCAL_W, :]) for rows in subs]
    chains = [(i, hd) for i in range(len(subs)) for hd in range(MEM_HEADS)]
    pending, heads = {}, {}
    for c in range(len(chains) + OPROJ_AHEAD):
        if c < len(chains):
            i, hd = chains[c]
            pending[(i, hd)] = scores(subs[i], hd)
        if c >= OPROJ_AHEAD:
            i, hd = chains[c - OPROJ_AHEAD]
            heads[(i, hd)] = attend(hd, pending.pop((i, hd)))
            if hd == MEM_HEADS - 1:
                tiles = [jnp.concatenate([heads.pop((i, 2 * t)), heads.pop((i, 2 * t + 1))], axis=0).T.astype(BF16)
                         for t in range(MEM_HEADS // 2)]
                cross = jnp.concatenate(tiles, axis=-1)
                o_ref[subs[i], :] = x_ref[subs[i], :] + (y_local[i] + _dot(cross, wo_ref[LOCAL_W:, :]))


def _out_proj(x, local, qc, mk, mvt, wo):
    b, s, d = x.shape
    tm = min(2 * OPROJ_SUB, s)

    def tok(width):
        return pl.BlockSpec((None, tm, width), lambda i, j: (i, j, 0))

    return pl.pallas_call(
        _oproj_body,
        grid=(b, s // tm),
        in_specs=[tok(d), tok(LOCAL_W), tok(MEM_WIDTH),
                  pl.BlockSpec((None, N_MEM, MEM_WIDTH), lambda i, j: (i, 0, 0)),
                  pl.BlockSpec((None, MEM_WIDTH, N_MEM), lambda i, j: (i, 0, 0)),
                  _resident(wo.shape)],
        out_specs=tok(d),
        out_shape=jax.ShapeDtypeStruct((b, s, d), F32),
        compiler_params=_params(2),
        name="out_proj",
    )(x, local, qc, mk, mvt, wo)


def _prep_a(a_w_in):
    nl, d, _ = a_w_in.shape
    order = jnp.array(A_HEAD_ORDER)
    q = a_w_in[:, :, :A_Q_W].reshape(nl, d, A_Q_HEADS, HEAD_DIM)[:, :, order].reshape(nl, d, A_Q_W)
    k = a_w_in[:, :, A_Q_W:A_Q_W + A_KV_W]
    v = a_w_in[:, :, A_Q_W + A_KV_W:A_Q_W + 2 * A_KV_W]
    qc = a_w_in[:, :, A_Q_W + 2 * A_KV_W:]
    w = jnp.concatenate([q, k, qc], axis=-1).astype(BF16)
    return w, jnp.transpose(v, (0, 2, 1)).astype(BF16)


def _prep_w_o(w_o):
    depth, _, d = w_o.shape
    order = jnp.array(A_HEAD_ORDER)
    local = w_o[:, :LOCAL_W].reshape(depth, A_Q_HEADS, HEAD_DIM, d)
    local = jnp.where((jnp.arange(depth) % 2 == 0)[:, None, None, None], local[:, order], local)
    return jnp.concatenate([local.reshape(depth, LOCAL_W, d), w_o[:, LOCAL_W:]], axis=1).astype(BF16)


def _prep_b(b_w_in, b_w_q_up, b_w_kv_up):
    nl, d, _ = b_w_in.shape
    c_q = b_w_in[:, :, :Q_LORA]
    c_kv = b_w_in[:, :, Q_LORA:Q_LORA + KV_LORA]
    k_r = b_w_in[:, :, Q_LORA + KV_LORA:Q_LORA + KV_LORA + QK_ROPE]
    qc = b_w_in[:, :, Q_LORA + KV_LORA + QK_ROPE:]
    kr_tile = jnp.pad(_with_rotate_half(k_r), ((0, 0), (0, 0), (QK_NOPE, 0)))
    w_in = jnp.concatenate([c_q, c_kv, qc, kr_tile], axis=-1).astype(BF16)
    wq = b_w_q_up.reshape(nl, Q_LORA, B_HEADS, B_QK)
    wq = jnp.concatenate([wq[..., :QK_NOPE], _with_rotate_half(wq[..., QK_NOPE:])], axis=-1)
    wq = wq.reshape(nl, Q_LORA, B_HEADS * B_HEAD_PAD).astype(BF16)
    wkv = b_w_kv_up.reshape(nl, KV_LORA, B_HEADS, QK_NOPE + V_HEAD)
    wk = jnp.pad(wkv[..., :QK_NOPE], ((0, 0), (0, 0), (0, 0), (0, B_HEAD_PAD - QK_NOPE)))
    wk = wk.reshape(nl, KV_LORA, B_HEADS * B_HEAD_PAD).astype(BF16)
    wvt = jnp.transpose(wkv[..., QK_NOPE:].reshape(nl, KV_LORA, B_HEADS * V_HEAD), (0, 2, 1)).astype(BF16)
    return w_in, wq, wk, wvt


def _trunk(x, mem, w):
    b, s, d = x.shape
    depth = w["mix_norm"].shape[0]
    tab_a = _rope_table(s, HEAD_DIM, 0, HEAD_DIM, 1.0, False)
    tab_bq = _rope_rot_table(s, B_QK ** -0.5 * LOG2E, True)
    tab_bk = _rope_rot_table(s, 1.0, False)
    for i in range(depth):
        x = _ffn(x.reshape(b * s, d), w["ffn1_norm"][i], w["ffn1_w_gu"], w["ffn1_w_down"], i).reshape(b, s, d)
        mk, mvt = _mem_kv(mem, w["mem_norm"][i], w["w_mem_k"][i], w["w_mem_vt"][i])
        j = i // 2
        if i % 2 == 0:
            q, k, vt, qc = _proj_a(x, w["mix_norm"][i], w["a_w_in"][j], w["a_wvt"][j], tab_a)
            local = _win_attn(q, k, vt, w["a_sink"][j])
        else:
            q, k, v, qc = _proj_b(x, w["mix_norm"][i], w["b_w_in"][j], w["b_q_norm"][j], w["b_wq"][j],
                                  w["b_kv_norm"][j], w["b_wk"][j], w["b_wvt"][j], tab_bq, tab_bk)
            local = _mla_attn(q, k, v)
        x = _out_proj(x, local, qc, mk, mvt, w["w_o"][i])
        g_final = w["final_norm"] if i == depth - 1 else None
        x = _ffn(x.reshape(b * s, d), w["ffn2_norm"][i], w["ffn2_w_gu"], w["ffn2_w_down"], i,
                 g_final).reshape(b, s, d)
    return x


def kernel(x_prompt, x_sample, mem_prompt, mem_sample, ffn1_norm, ffn1_w_gu, ffn1_w_down, mix_norm,
           mem_norm, w_mem_kv, a_w_in, a_sink, b_w_in, b_q_norm, b_w_q_up, b_kv_norm, b_w_kv_up,
           w_o, ffn2_norm, ffn2_w_gu, ffn2_w_down, final_norm):
    w = {
        "ffn1_norm": ffn1_norm, "ffn2_norm": ffn2_norm, "mix_norm": mix_norm, "mem_norm": mem_norm,
        "w_mem_k": w_mem_kv[:, :, :MEM_WIDTH].astype(BF16),
        "w_mem_vt": jnp.transpose(w_mem_kv[:, :, MEM_WIDTH:], (0, 2, 1)).astype(BF16), "a_sink": a_sink,
        "b_q_norm": b_q_norm, "b_kv_norm": b_kv_norm, "w_o": _prep_w_o(w_o),
        "final_norm": final_norm,
    }
    w["ffn1_w_gu"], w["ffn1_w_down"] = ffn1_w_gu.astype(BF16), ffn1_w_down.astype(BF16)
    w["ffn2_w_gu"], w["ffn2_w_down"] = ffn2_w_gu.astype(BF16), ffn2_w_down.astype(BF16)
    w["a_w_in"], w["a_wvt"] = _prep_a(a_w_in)
    w["b_w_in"], w["b_wq"], w["b_wk"], w["b_wvt"] = _prep_b(b_w_in, b_w_q_up, b_w_kv_up)
    return (_trunk(x_prompt, mem_prompt, w), _trunk(x_sample, mem_sample, w))
```

```python
import functools
import math

import jax
import jax.numpy as jnp
from jax import lax
from jax.experimental import pallas as pl
from jax.experimental.pallas import tpu as pltpu

D_MODEL = 1024
HEAD_DIM = 64
ROPE_THETA = 10000.0
NORM_EPS = 1e-6
D_FF = 2816
N_MEM = 256
MEM_HEADS = 4
MEM_WIDTH = MEM_HEADS * HEAD_DIM
A_Q_HEADS = 12
A_KV_HEADS = 4
A_GROUP = A_Q_HEADS // A_KV_HEADS
WINDOW = 128
A_Q_W = A_Q_HEADS * HEAD_DIM
A_KV_W = A_KV_HEADS * HEAD_DIM
B_HEADS = 12
Q_LORA = 384
KV_LORA = 256
QK_NOPE = 64
QK_ROPE = 32
V_HEAD = 64
B_QK = QK_NOPE + QK_ROPE
LOCAL_W = A_Q_W
NEG = -1e30
LOG2E = math.log2(math.e)

LANES = 128
B_HEAD_PAD = LANES
FF_CHUNK = 256
PROJA_SUB = 512
PROJB_SUB = 512
OPROJ_SUB = 512
OPROJ_AHEAD = 4
MLA_QB = 512
MLA_KB = 512
MLA_NKV = 8
MLA_CHUNK = 32
MLA_DEN_ROWS = 16
MLA_AHEAD = 2
MLA_SLOTS = 2 * MLA_AHEAD
VMEM_LIMIT = 56 * 1024 * 1024

F32 = jnp.float32
BF16 = jnp.bfloat16


def _params(n_axes):
    return pltpu.CompilerParams(dimension_semantics=("arbitrary",) * n_axes,
                                vmem_limit_bytes=VMEM_LIMIT)


def _rms(x, g):
    return x * lax.rsqrt(jnp.mean(x * x, axis=-1, keepdims=True) + NORM_EPS) * g


def _dot(a, b):
    return jnp.dot(a, b, preferred_element_type=F32)


def _dot_nt(a, b):
    return lax.dot_general(a, b, (((1,), (1,)), ((), ())), preferred_element_type=F32)


def _resident(shape):
    zeros = (0,) * len(shape)
    return pl.BlockSpec(shape, lambda *_: zeros, pipeline_mode=pl.Buffered(1))


def _ffn_body(*refs, final):
    if final:
        x_ref, g_ref, wgu_ref, wd_ref, gf_ref, o_ref, h_ref = refs
    else:
        x_ref, g_ref, wgu_ref, wd_ref, o_ref, h_ref = refs
    h_ref[...] = _rms(x_ref[...], g_ref[...]).astype(BF16)
    nch = D_FF // FF_CHUNK

    def gate_up(c):
        h = h_ref[...]
        gate = _dot(h, wgu_ref[:, FF_CHUNK * c:FF_CHUNK * (c + 1)])
        up = _dot(h, wgu_ref[:, D_FF + FF_CHUNK * c:D_FF + FF_CHUNK * (c + 1)])
        return gate, up

    acc = None
    nxt = gate_up(0)
    for c in range(nch):
        gate, up = nxt
        if c + 1 < nch:
            nxt = gate_up(c + 1)
        a = (gate / (1.0 + jnp.exp(-gate)) * up).astype(BF16)
        down = _dot(a, wd_ref[FF_CHUNK * c:FF_CHUNK * (c + 1), :])
        acc = down if acc is None else acc + down
    y = x_ref[...] + 0.5 * acc
    if final:
        y = _rms(y, gf_ref[...])
    o_ref[...] = y


def _layer_resident(stacked, layer):
    zeros = (0,) * (stacked.ndim - 1)
    return pl.BlockSpec((None,) + stacked.shape[1:], lambda *_: (layer,) + zeros,
                        pipeline_mode=pl.Buffered(1))


def _ffn(x, g, w_gu, w_down, layer, g_final=None):
    t, d = x.shape
    tm = min(1024, t)
    final = g_final is not None
    in_specs = [
        pl.BlockSpec((tm, d), lambda i: (i, 0)),
        _resident((1, d)),
        _layer_resident(w_gu, layer),
        _layer_resident(w_down, layer),
    ]
    args = [x, g.reshape(1, d), w_gu, w_down]
    if final:
        in_specs.append(_resident((1, d)))
        args.append(g_final.reshape(1, d))
    return pl.pallas_call(
        functools.partial(_ffn_body, final=final),
        grid=(t // tm,),
        in_specs=in_specs,
        out_specs=pl.BlockSpec((tm, d), lambda i: (i, 0)),
        out_shape=jax.ShapeDtypeStruct((t, d), F32),
        scratch_shapes=[pltpu.VMEM((tm, d), BF16)],
        compiler_params=_params(1),
        name="ffn_final" if final else "ffn",
    )(*args)


def _memkv_body(mem_ref, g_ref, wk_ref, wvt_ref, mk_ref, mvt_ref):
    h = _rms(mem_ref[...], g_ref[...]).astype(BF16)
    mk_ref[...] = _dot(h, wk_ref[...]).astype(BF16)
    mvt_ref[...] = _dot_nt(wvt_ref[...], h).astype(BF16)


def _mem_kv(mem, g, wk, wvt):
    b, n, d = mem.shape
    return pl.pallas_call(
        _memkv_body,
        grid=(b,),
        in_specs=[pl.BlockSpec((None, n, d), lambda i: (i, 0, 0)), _resident((1, d)),
                  _resident(wk.shape), _resident(wvt.shape)],
        out_specs=[pl.BlockSpec((None, n, MEM_WIDTH), lambda i: (i, 0, 0)),
                   pl.BlockSpec((None, MEM_WIDTH, n), lambda i: (i, 0, 0))],
        out_shape=[jax.ShapeDtypeStruct((b, n, MEM_WIDTH), BF16),
                   jax.ShapeDtypeStruct((b, MEM_WIDTH, n), BF16)],
        compiler_params=_params(1),
        name="mem_kv",
    )(mem, g.reshape(1, d), wk, wvt)


def _rope_block(xb, tab_ref, shift):
    c = tab_ref[:, 0:LANES]
    s_plus = tab_ref[:, LANES:2 * LANES]
    s_minus = tab_ref[:, 2 * LANES:3 * LANES]
    return (xb * c + pltpu.roll(xb, shift, 1) * s_plus
            + pltpu.roll(xb, LANES - shift, 1) * s_minus)


def _rope_table(seq, dim, lane_start, period, scale, pass_through):
    half = dim // 2
    inv = 1.0 / (ROPE_THETA ** (jnp.arange(0, dim, 2, dtype=F32) / dim))
    ang = jnp.arange(seq, dtype=F32)[:, None] * inv[None, :]
    cos, sin = jnp.cos(ang), jnp.sin(ang)
    zeros_h = jnp.zeros((seq, half), F32)
    lead = jnp.full((seq, lane_start), 1.0 if pass_through else 0.0, F32)
    lead0 = jnp.zeros((seq, lane_start), F32)
    tail0 = jnp.zeros((seq, period - lane_start - dim), F32)
    reps = LANES // period
    c = jnp.tile(jnp.concatenate([lead, cos, cos, tail0], -1), (1, reps))
    s_plus = jnp.tile(jnp.concatenate([lead0, zeros_h, sin, tail0], -1), (1, reps))
    s_minus = jnp.tile(jnp.concatenate([lead0, -sin, zeros_h, tail0], -1), (1, reps))
    return jnp.concatenate([c, s_plus, s_minus], -1) * scale


def _rope_rot_block(xb, tab_ref):
    c = tab_ref[:, 0:LANES]
    s = tab_ref[:, LANES:2 * LANES]
    return xb * c + pltpu.roll(xb, LANES - QK_ROPE, 1) * s


def _rope_rot_table(seq, scale, pass_through):
    inv = 1.0 / (ROPE_THETA ** (jnp.arange(0, QK_ROPE, 2, dtype=F32) / QK_ROPE))
    ang = jnp.arange(seq, dtype=F32)[:, None] * inv[None, :]
    cos, sin = jnp.cos(ang), jnp.sin(ang)
    lead = jnp.full((seq, QK_NOPE), 1.0 if pass_through else 0.0, F32)
    lead0 = jnp.zeros((seq, QK_NOPE), F32)
    tail0 = jnp.zeros((seq, LANES - QK_NOPE - QK_ROPE), F32)
    c = jnp.concatenate([lead, cos, cos, tail0], -1)
    s = jnp.concatenate([lead0, sin, sin, tail0], -1)
    return jnp.concatenate([c, s], -1) * scale


def _with_rotate_half(w_rope):
    x1, x2 = w_rope[..., :QK_ROPE // 2], w_rope[..., QK_ROPE // 2:]
    return jnp.concatenate([w_rope, -x2, x1], axis=-1)


A_HEAD_ORDER = (0, 3, 1, 4, 2, 5, 6, 9, 7, 10, 8, 11)
A_KV_TILES = A_KV_W // LANES
A_TILE_HEADS = A_GROUP
WIN_QBLOCKS = 4
WIN_AHEAD = 3


def _proja_body(x_ref, g_ref, w_ref, wvt_ref, tab_ref, q_ref, k_ref, vt_ref, qc_ref, *, qscale, sub):
    subs = [slice(r0, r0 + sub) for r0 in range(0, x_ref.shape[0], sub)]
    nq = A_Q_W // LANES
    nk = A_KV_W // LANES

    def project(rows):
        h = _rms(x_ref[rows, :], g_ref[...]).astype(BF16)
        proj = _dot(h, w_ref[...])
        vt = _dot_nt(wvt_ref[...], h)
        return proj, vt

    def finish(rows, proj, vt):
        for i in range(sub // WINDOW):
            vt_ref[rows.start // WINDOW + i] = vt[:, WINDOW * i:WINDOW * (i + 1)].astype(BF16)
        qc_ref[rows, :] = (proj[:, A_Q_W + A_KV_W:] * qscale).astype(BF16)
        tab = tab_ref.at[rows, :]
        for j in range(nq + nk):
            rb = _rope_block(proj[:, LANES * j:LANES * (j + 1)], tab, HEAD_DIM // 2)
            if j < nq:
                q_ref[rows, LANES * j:LANES * (j + 1)] = (rb * qscale).astype(BF16)
            else:
                k_ref[rows, LANES * (j - nq):LANES * (j - nq + 1)] = rb.astype(BF16)

    projected = [project(rows) for rows in subs]
    for rows, (proj, vt) in zip(subs, projected):
        finish(rows, proj, vt)


def _proj_a(x, g, w, wvt, tab):
    b, s, d = x.shape
    tm = min(2 * PROJA_SUB, s)
    sub = min(PROJA_SUB, tm)
    nblk = tm // WINDOW

    def out(width):
        return (jax.ShapeDtypeStruct((b, s, width), BF16),
                pl.BlockSpec((None, tm, width), lambda i, j: (i, j, 0)))

    vt = (jax.ShapeDtypeStruct((b, s // WINDOW, A_KV_W, WINDOW), BF16),
          pl.BlockSpec((None, nblk, A_KV_W, WINDOW), lambda i, j: (i, j, 0, 0)))
    shapes, specs = zip(out(A_Q_W), out(A_KV_W), vt, out(MEM_WIDTH))
    return pl.pallas_call(
        functools.partial(_proja_body, qscale=HEAD_DIM ** -0.5 * LOG2E, sub=sub),
        grid=(b, s // tm),
        in_specs=[
            pl.BlockSpec((None, tm, d), lambda i, j: (i, j, 0)),
            _resident((1, d)),
            _resident(w.shape),
            _resident(wvt.shape),
            pl.BlockSpec((tm, 3 * LANES), lambda i, j: (j, 0)),
        ],
        out_specs=list(specs),
        out_shape=list(shapes),
        compiler_params=_params(2),
        name="proj_a",
    )(x, g.reshape(1, d), w, wvt, tab)


def _wattn_body(bias_ref, sink_ref, q_ref, kp_ref, kc_ref, kn_ref, vp_ref, vc_ref, vn_ref, o_ref, *, nsteps):
    step = pl.program_id(1)
    lane = lax.broadcasted_iota(jnp.int32, (WINDOW, LANES), 1)
    ones = jnp.ones((MLA_DEN_ROWS, 3 * WINDOW), BF16)
    zero = jnp.zeros((WINDOW, LANES), BF16)
    def operands(blk, tile):
        ts = slice(LANES * tile, LANES * (tile + 1))
        k_blocks = ([kp_ref[:, ts]] + [kc_ref[WINDOW * i:WINDOW * (i + 1), ts] for i in range(WIN_QBLOCKS)]
                    + [kn_ref[:, ts]])
        vt_blocks = [vp_ref[ts, :]] + [vc_ref[i, ts, :] for i in range(WIN_QBLOCKS)] + [vn_ref[ts, :]]
        k_band = jnp.concatenate(k_blocks[blk:blk + 3], axis=0)
        vt_band = jnp.concatenate(vt_blocks[blk:blk + 3], axis=1)
        return k_band, vt_band

    def scores(blk, tile, half):
        variant = 1
        if blk == 0:
            variant = jnp.where(step == 0, 0, variant)
        if blk == WIN_QBLOCKS - 1:
            variant = jnp.where(step == nsteps - 1, 2, variant)
        rows = slice(WINDOW * blk, WINDOW * (blk + 1))
        q_tiles = [q_ref[rows, LANES * (A_TILE_HEADS * tile + r):LANES * (A_TILE_HEADS * tile + r + 1)]
                   for r in range(A_TILE_HEADS)]
        in_half = (lane < HEAD_DIM) if half == 0 else (lane >= HEAD_DIM)
        q_stack = jnp.concatenate([jnp.where(in_half, qt, zero) for qt in q_tiles], axis=0)
        return _dot_nt(operands(blk, tile)[0], q_stack) + bias_ref[variant]

    def attend(blk, tile, half, s):
        sink = sink_ref[2 * tile + half]
        m = jnp.maximum(jnp.max(s, axis=0, keepdims=True), sink)
        p = jnp.exp2(s - m).astype(BF16)
        vt_band = operands(blk, tile)[1]
        v_aug = jnp.concatenate([vt_band[HEAD_DIM * half:HEAD_DIM * (half + 1), :], ones], axis=0)
        o_aug = _dot(v_aug, p)
        den = o_aug[HEAD_DIM:HEAD_DIM + 1, :] + jnp.exp2(sink - m)
        return o_aug[:HEAD_DIM, :] / den

    def emit(blk, tile, halves):
        rows = slice(WINDOW * blk, WINDOW * (blk + 1))
        for r in range(A_TILE_HEADS):
            cs = slice(WINDOW * r, WINDOW * (r + 1))
            out_t = jnp.concatenate([halves[0][:, cs], halves[1][:, cs]], axis=0)
            j = A_TILE_HEADS * tile + r
            o_ref[rows, LANES * j:LANES * (j + 1)] = out_t.T.astype(BF16)

    chains = [(blk, tile, half) for blk in range(WIN_QBLOCKS) for tile in range(A_KV_TILES) for half in range(2)]
    pending, done = {}, {}
    for i in range(len(chains) + WIN_AHEAD):
        if i < len(chains):
            pending[chains[i]] = scores(*chains[i])
        if i >= WIN_AHEAD:
            blk, tile, half = chains[i - WIN_AHEAD]
            done[(blk, tile, half)] = attend(blk, tile, half, pending.pop((blk, tile, half)))
            if half == 1:
                emit(blk, tile, [done.pop((blk, tile, 0)), done.pop((blk, tile, 1))])


def _band_bias(dtype=F32):
    kj = jnp.arange(3 * WINDOW)[:, None]
    qi = (jnp.arange(3 * WINDOW) % WINDOW)[None, :]
    band = (kj - qi >= 0) & (kj - qi <= 2 * WINDOW)
    first = band & (kj >= WINDOW)
    last = band & (kj < 2 * WINDOW)
    return jnp.where(jnp.stack([first, band, last]), 0.0, NEG).astype(dtype)


def _win_attn(q, k, vt, sink):
    b, s, _ = q.shape
    nb = s // WINDOW
    nsteps = nb // WIN_QBLOCKS
    assert nb >= 2 and nb % WIN_QBLOCKS == 0
    qrows = WIN_QBLOCKS * WINDOW
    sink_rows = jnp.repeat(sink.reshape(A_KV_HEADS, 1, A_GROUP) * LOG2E, WINDOW, axis=-1)

    def halo(j, shift):
        return jnp.clip(WIN_QBLOCKS * j + shift, 0, nb - 1)

    def k_halo(shift):
        return pl.BlockSpec((None, WINDOW, A_KV_W), lambda i, j: (i, halo(j, shift), 0))

    def vt_halo(shift):
        return pl.BlockSpec((None, None, A_KV_W, WINDOW), lambda i, j: (i, halo(j, shift), 0, 0))

    return pl.pallas_call(
        functools.partial(_wattn_body, nsteps=nsteps),
        grid=(b, nsteps),
        in_specs=[
            _resident((3, 3 * WINDOW, 3 * WINDOW)),
            _resident((A_KV_HEADS, 1, 3 * WINDOW)),
            pl.BlockSpec((None, qrows, A_Q_W), lambda i, j: (i, j, 0)),
            k_halo(-1),
            pl.BlockSpec((None, qrows, A_KV_W), lambda i, j: (i, j, 0)),
            k_halo(WIN_QBLOCKS),
            vt_halo(-1),
            pl.BlockSpec((None, WIN_QBLOCKS, A_KV_W, WINDOW), lambda i, j: (i, j, 0, 0)),
            vt_halo(WIN_QBLOCKS),
        ],
        out_specs=pl.BlockSpec((None, qrows, A_Q_W), lambda i, j: (i, j, 0)),
        out_shape=jax.ShapeDtypeStruct((b, s, A_Q_W), BF16),
        compiler_params=_params(2),
        name="win_attn",
    )(_band_bias(), sink_rows, q, k, k, k, vt, vt, vt)


def _projb_body(x_ref, g_ref, win_ref, gq_ref, wq_ref, gkv_ref, wk_ref, wvt_ref, tabq_ref, tabk_ref,
                q_ref, k_ref, vt_ref, qc_ref, *, qcscale, sub):
    kb = vt_ref.shape[-1]
    subs = [slice(r0, r0 + sub) for r0 in range(0, x_ref.shape[0], sub)]

    def latents(rows):
        h = _rms(x_ref[rows, :], g_ref[...]).astype(BF16)
        proj = _dot(h, win_ref[...])
        c_q = _rms(proj[:, :Q_LORA], gq_ref[...]).astype(BF16)
        kv0 = Q_LORA
        c_kv = _rms(proj[:, kv0:kv0 + KV_LORA], gkv_ref[...]).astype(BF16)
        qc0 = kv0 + KV_LORA
        qc_ref[rows, :] = (proj[:, qc0:qc0 + MEM_WIDTH] * qcscale).astype(BF16)
        kr0 = qc0 + MEM_WIDTH
        return c_q, c_kv, proj[:, kr0:kr0 + LANES]

    def up_project(c_q, c_kv):
        q_all = _dot(c_q, wq_ref[...])
        k_all = _dot(c_kv, wk_ref[...])
        vt = _dot_nt(wvt_ref[...], c_kv)
        return q_all, k_all, vt

    def finish(rows, q_all, k_all, vt, kr_tile):
        r0 = rows.start
        vt_ref[r0 // kb, :, r0 % kb:r0 % kb + sub] = vt.astype(BF16)
        k_rope = _rope_rot_block(kr_tile, tabk_ref.at[rows, :])
        for hd in range(B_HEADS):
            hs = slice(B_HEAD_PAD * hd, B_HEAD_PAD * (hd + 1))
            q_ref[rows, hs] = _rope_rot_block(q_all[:, hs], tabq_ref.at[rows, :]).astype(BF16)
            k_ref[rows, hs] = (k_all[:, hs] + k_rope).astype(BF16)

    lat = [latents(rows) for rows in subs]
    ups = [up_project(c_q, c_kv) for c_q, c_kv, _ in lat]
    for rows, (q_all, k_all, vt), (_, _, kr_tile) in zip(subs, ups, lat):
        finish(rows, q_all, k_all, vt, kr_tile)


def _proj_b(x, g, w_in, gq, wq, gkv, wk, wvt, tabq, tabk):
    b, s, d = x.shape
    kb = min(max(MLA_KB, s // MLA_NKV), s)
    tm = min(2 * PROJB_SUB, s)
    sub = min(PROJB_SUB, tm)
    assert tm % kb == 0 and kb % sub == 0

    def out(width):
        return (jax.ShapeDtypeStruct((b, s, width), BF16),
                pl.BlockSpec((None, tm, width), lambda i, j: (i, j, 0)))

    vt = (jax.ShapeDtypeStruct((b, s // kb, B_HEADS * V_HEAD, kb), BF16),
          pl.BlockSpec((None, tm // kb, B_HEADS * V_HEAD, kb), lambda i, j: (i, j, 0, 0)))
    shapes, specs = zip(out(B_HEADS * B_HEAD_PAD), out(B_HEADS * B_HEAD_PAD), vt, out(MEM_WIDTH))
    tab_spec = pl.BlockSpec((tm, 2 * LANES), lambda i, j: (j, 0))
    return pl.pallas_call(
        functools.partial(_projb_body, qcscale=HEAD_DIM ** -0.5 * LOG2E, sub=sub),
        grid=(b, s // tm),
        in_specs=[
            pl.BlockSpec((None, tm, d), lambda i, j: (i, j, 0)),
            _resident((1, d)),
            _resident(w_in.shape),
            _resident((1, Q_LORA)),
            _resident(wq.shape),
            _resident((1, KV_LORA)),
            _resident(wk.shape),
            _resident(wvt.shape),
            tab_spec, tab_spec,
        ],
        out_specs=list(specs),
        out_shape=list(shapes),
        compiler_params=_params(2),
        name="proj_b",
    )(x, g.reshape(1, d), w_in, gq.reshape(1, -1), wq, gkv.reshape(1, -1), wk, wvt, tabq, tabk)


def _mla_body(q_ref, k_ref, vt_ref, o_ref, *scratch, kb, nkv):
    per_slot = 3 * 2
    slots = [scratch[per_slot * i:per_slot * (i + 1)] for i in range(MLA_SLOTS)]
    s_refs = [sl[0:2] for sl in slots]
    p_refs = [sl[2:4] for sl in slots]
    alpha_refs = [sl[4:6] for sl in slots]
    m_refs, acc_refs = (scratch[per_slot * MLA_SLOTS + 2 * i:per_slot * MLA_SLOTS + 2 * (i + 1)]
                        for i in range(2))
    hslices = [slice(B_HEAD_PAD * hh, B_HEAD_PAD * (hh + 1)) for hh in range(2)]
    vslices = [slice(V_HEAD * hh, V_HEAD * (hh + 1)) for hh in range(2)]
    chunks = [slice(c, c + MLA_CHUNK) for c in range(0, kb, MLA_CHUNK)]

    def scores(t, slot):
        rows = pl.ds(pl.multiple_of(t * kb, kb), kb)
        for hh, hs in enumerate(hslices):
            s_refs[slot][hh][...] = _dot_nt(k_ref[rows, hs], q_ref[:, hs])

    def softmax(slot):
        for hh in range(2):
            s_ref, p_ref = s_refs[slot][hh], p_refs[slot][hh]
            cmax = s_ref[chunks[0], :]
            for ch in chunks[1:]:
                cmax = jnp.maximum(cmax, s_ref[ch, :])
            m = m_refs[hh][...]
            m_new = jnp.maximum(m, jnp.max(cmax, axis=0, keepdims=True))
            alpha = jnp.exp2(m - m_new)
            for ch in chunks:
                p_ref[ch, :] = jnp.exp2(s_ref[ch, :] - m_new).astype(BF16)
            alpha_refs[slot][hh][...] = alpha
            m_refs[hh][...] = m_new

    def accumulate(t, slot):
        ones = jnp.ones((MLA_DEN_ROWS, kb), BF16)
        for hh, vs in enumerate(vslices):
            v_aug = jnp.concatenate([vt_ref[t, vs, :], ones], axis=0)
            acc_refs[hh][...] = (alpha_refs[slot][hh][...] * acc_refs[hh][...]
                                 + _dot(v_aug, p_refs[slot][hh][...]))

    def stage(t, phase, ahead=True, behind=True):
        if ahead:
            scores(t + MLA_AHEAD, (phase + MLA_AHEAD) % MLA_SLOTS)
        if behind:
            accumulate(t - MLA_AHEAD, (phase - MLA_AHEAD) % MLA_SLOTS)
        softmax(phase)

    lo = MLA_AHEAD
    trips = max(nkv - 2 * MLA_AHEAD, 0) // MLA_SLOTS
    tail = range(lo + trips * MLA_SLOTS, nkv)
    assert all(t + MLA_AHEAD >= nkv for t in tail)

    def init_m():
        for hh in range(2):
            m_refs[hh][...] = jnp.full(m_refs[hh].shape, NEG, F32)

    def init_acc():
        for hh in range(2):
            acc_refs[hh][...] = jnp.zeros(acc_refs[hh].shape, F32)

    def output():
        out_t = jnp.concatenate(
            [acc_refs[hh][:V_HEAD, :] / acc_refs[hh][V_HEAD:V_HEAD + 1, :] for hh in range(2)], axis=0)
        o_ref[...] = out_t.T.astype(BF16)

    def loop():
        def full_stages(i, carry):
            for j in range(MLA_SLOTS):
                stage(lo + MLA_SLOTS * i + j, (lo + j) % MLA_SLOTS)
            return carry

        lax.fori_loop(0, trips, full_stages, 0)

    fill_scores = [functools.partial(scores, t, t % MLA_SLOTS) for t in range(min(MLA_AHEAD, nkv))]
    fill_stages = [functools.partial(stage, t, t % MLA_SLOTS, ahead=t + MLA_AHEAD < nkv, behind=False)
                   for t in range(min(lo, nkv))]
    drain_stages = [functools.partial(stage, t, t % MLA_SLOTS, ahead=False, behind=t >= MLA_AHEAD) for t in tail]
    drain_accs = [functools.partial(accumulate, t, t % MLA_SLOTS) for t in range(max(nkv - MLA_AHEAD, 0), nkv)]

    def run(pieces):
        for piece in pieces:
            piece()

    n = pl.program_id(2)
    last = pl.num_programs(2) - 1

    def fill_and_loop():
        run(fill_scores)
        init_m()
        init_acc()
        run(fill_stages)
        loop()

    def drain():
        run(drain_stages)
        run(drain_accs)
        output()

    @pl.when(n == 0)
    def _():
        fill_and_loop()

    def interleaved_drain_fill_and_loop():
        for i in range(max(len(drain_stages), len(fill_scores))):
            run(drain_stages[i:i + 1])
            run(fill_scores[i:i + 1])
        init_m()
        for i in range(max(len(drain_accs), len(fill_stages))):
            run(drain_accs[i:i + 1])
            run(fill_stages[i:i + 1])
        output()
        init_acc()
        loop()

    @pl.when(jnp.logical_and(n > 0, n < last))
    def _():
        if kb > MLA_KB:
            interleaved_drain_fill_and_loop()
        else:
            drain()
            fill_and_loop()

    @pl.when(n == last)
    def _():
        drain()


def _mla_attn(q, k, vt):
    b, s, _ = q.shape
    _, nkv, _, kb = vt.shape
    qb = min(MLA_QB, s)
    nq = s // qb
    npair = B_HEADS // 2
    per_slot = ([pltpu.VMEM((kb, qb), F32)] * 2
                + [pltpu.VMEM((kb, qb), BF16)] * 2
                + [pltpu.VMEM((1, qb), F32)] * 2)
    state = ([pltpu.VMEM((1, qb), F32)] * 2
             + [pltpu.VMEM((V_HEAD + MLA_DEN_ROWS, qb), F32)] * 2)
    return pl.pallas_call(
        functools.partial(_mla_body, kb=kb, nkv=nkv),
        grid=(b, npair, nq + 1),
        in_specs=[
            pl.BlockSpec((None, qb, 2 * B_HEAD_PAD), lambda i, j, n: (i, jnp.minimum(n, nq - 1), j)),
            pl.BlockSpec((None, s, 2 * B_HEAD_PAD), lambda i, j, n: (i, 0, j)),
            pl.BlockSpec((None, nkv, 2 * V_HEAD, kb), lambda i, j, n: (i, 0, j, 0)),
        ],
        out_specs=pl.BlockSpec((None, qb, 2 * V_HEAD), lambda i, j, n: (i, jnp.maximum(n - 1, 0), j)),
        out_shape=jax.ShapeDtypeStruct((b, s, B_HEADS * V_HEAD), BF16),
        scratch_shapes=per_slot * MLA_SLOTS + state,
        compiler_params=_params(3),
        name="mla_attn",
    )(q, k, vt)


def _oproj_body(x_ref, loc_ref, qc_ref, mk_ref, mvt_ref, wo_ref, o_ref):
    sub = min(OPROJ_SUB, qc_ref.shape[0])
    lane = lax.broadcasted_iota(jnp.int32, (sub, LANES), 1)
    ones = jnp.ones((MLA_DEN_ROWS, N_MEM), BF16)
    zero = jnp.zeros((sub, LANES), BF16)
    subs = [slice(r0, r0 + sub) for r0 in range(0, qc_ref.shape[0], sub)]

    def scores(rows, hd):
        ts = slice(LANES * (hd // 2), LANES * (hd // 2 + 1))
        in_half = (lane < HEAD_DIM) if hd % 2 == 0 else (lane >= HEAD_DIM)
        return _dot_nt(mk_ref[:, ts], jnp.where(in_half, qc_ref[rows, ts], zero))

    def attend(hd, s):
        p = jnp.exp2(s - jnp.max(s, axis=0, keepdims=True)).astype(BF16)
        v_aug = jnp.concatenate([mvt_ref[HEAD_DIM * hd:HEAD_DIM * (hd + 1), :], ones], axis=0)
        o_aug = _dot(v_aug, p)
        return o_aug[:HEAD_DIM, :] / o_aug[HEAD_DIM:HEAD_DIM + 1, :]

    y_local = [_dot(loc_ref[rows, :], wo_ref[:LOCAL_W, :]) for rows in subs]
    chains = [(i, hd) for i in range(len(subs)) for hd in range(MEM_HEADS)]
    pending, heads = {}, {}
    for c in range(len(chains) + OPROJ_AHEAD):
        if c < len(chains):
            i, hd = chains[c]
            pending[(i, hd)] = scores(subs[i], hd)
        if c >= OPROJ_AHEAD:
            i, hd = chains[c - OPROJ_AHEAD]
            heads[(i, hd)] = attend(hd, pending.pop((i, hd)))
            if hd == MEM_HEADS - 1:
                tiles = [jnp.concatenate([heads.pop((i, 2 * t)), heads.pop((i, 2 * t + 1))], axis=0).T.astype(BF16)
                         for t in range(MEM_HEADS // 2)]
                cross = jnp.concatenate(tiles, axis=-1)
                o_ref[subs[i], :] = x_ref[subs[i], :] + (y_local[i] + _dot(cross, wo_ref[LOCAL_W:, :]))


def _out_proj(x, local, qc, mk, mvt, wo):
    b, s, d = x.shape
    tm = min(2 * OPROJ_SUB, s)

    def tok(width):
        return pl.BlockSpec((None, tm, width), lambda i, j: (i, j, 0))

    return pl.pallas_call(
        _oproj_body,
        grid=(b, s // tm),
        in_specs=[tok(d), tok(LOCAL_W), tok(MEM_WIDTH),
                  pl.BlockSpec((None, N_MEM, MEM_WIDTH), lambda i, j: (i, 0, 0)),
                  pl.BlockSpec((None, MEM_WIDTH, N_MEM), lambda i, j: (i, 0, 0)),
                  _resident(wo.shape)],
        out_specs=tok(d),
        out_shape=jax.ShapeDtypeStruct((b, s, d), F32),
        compiler_params=_params(2),
        name="out_proj",
    )(x, local, qc, mk, mvt, wo)


def _prep_a(a_w_in):
    nl, d, _ = a_w_in.shape
    order = jnp.array(A_HEAD_ORDER)
    q = a_w_in[:, :, :A_Q_W].reshape(nl, d, A_Q_HEADS, HEAD_DIM)[:, :, order].reshape(nl, d, A_Q_W)
    k = a_w_in[:, :, A_Q_W:A_Q_W + A_KV_W]
    v = a_w_in[:, :, A_Q_W + A_KV_W:A_Q_W + 2 * A_KV_W]
    qc = a_w_in[:, :, A_Q_W + 2 * A_KV_W:]
    w = jnp.concatenate([q, k, qc], axis=-1).astype(BF16)
    return w, jnp.transpose(v, (0, 2, 1)).astype(BF16)


def _prep_w_o(w_o):
    depth, _, d = w_o.shape
    order = jnp.array(A_HEAD_ORDER)
    local = w_o[:, :LOCAL_W].reshape(depth, A_Q_HEADS, HEAD_DIM, d)
    local = jnp.where((jnp.arange(depth) % 2 == 0)[:, None, None, None], local[:, order], local)
    return jnp.concatenate([local.reshape(depth, LOCAL_W, d), w_o[:, LOCAL_W:]], axis=1).astype(BF16)


def _prep_b(b_w_in, b_w_q_up, b_w_kv_up):
    nl, d, _ = b_w_in.shape
    c_q = b_w_in[:, :, :Q_LORA]
    c_kv = b_w_in[:, :, Q_LORA:Q_LORA + KV_LORA]
    k_r = b_w_in[:, :, Q_LORA + KV_LORA:Q_LORA + KV_LORA + QK_ROPE]
    qc = b_w_in[:, :, Q_LORA + KV_LORA + QK_ROPE:]
    kr_tile = jnp.pad(_with_rotate_half(k_r), ((0, 0), (0, 0), (QK_NOPE, 0)))
    w_in = jnp.concatenate([c_q, c_kv, qc, kr_tile], axis=-1).astype(BF16)
    wq = b_w_q_up.reshape(nl, Q_LORA, B_HEADS, B_QK)
    wq = jnp.concatenate([wq[..., :QK_NOPE], _with_rotate_half(wq[..., QK_NOPE:])], axis=-1)
    wq = wq.reshape(nl, Q_LORA, B_HEADS * B_HEAD_PAD).astype(BF16)
    wkv = b_w_kv_up.reshape(nl, KV_LORA, B_HEADS, QK_NOPE + V_HEAD)
    wk = jnp.pad(wkv[..., :QK_NOPE], ((0, 0), (0, 0), (0, 0), (0, B_HEAD_PAD - QK_NOPE)))
    wk = wk.reshape(nl, KV_LORA, B_HEADS * B_HEAD_PAD).astype(BF16)
    wvt = jnp.transpose(wkv[..., QK_NOPE:].reshape(nl, KV_LORA, B_HEADS * V_HEAD), (0, 2, 1)).astype(BF16)
    return w_in, wq, wk, wvt


def _trunk(x, mem, w):
    b, s, d = x.shape
    depth = w["mix_norm"].shape[0]
    tab_a = _rope_table(s, HEAD_DIM, 0, HEAD_DIM, 1.0, False)
    tab_bq = _rope_rot_table(s, B_QK ** -0.5 * LOG2E, True)
    tab_bk = _rope_rot_table(s, 1.0, False)
    for i in range(depth):
        x = _ffn(x.reshape(b * s, d), w["ffn1_norm"][i], w["ffn1_w_gu"], w["ffn1_w_down"], i).reshape(b, s, d)
        mk, mvt = _mem_kv(mem, w["mem_norm"][i], w["w_mem_k"][i], w["w_mem_vt"][i])
        j = i // 2
        if i % 2 == 0:
            q, k, vt, qc = _proj_a(x, w["mix_norm"][i], w["a_w_in"][j], w["a_wvt"][j], tab_a)
            local = _win_attn(q, k, vt, w["a_sink"][j])
        else:
            q, k, v, qc = _proj_b(x, w["mix_norm"][i], w["b_w_in"][j], w["b_q_norm"][j], w["b_wq"][j],
                                  w["b_kv_norm"][j], w["b_wk"][j], w["b_wvt"][j], tab_bq, tab_bk)
            local = _mla_attn(q, k, v)
        x = _out_proj(x, local, qc, mk, mvt, w["w_o"][i])
        g_final = w["final_norm"] if i == depth - 1 else None
        x = _ffn(x.reshape(b * s, d), w["ffn2_norm"][i], w["ffn2_w_gu"], w["ffn2_w_down"], i,
                 g_final).reshape(b, s, d)
    return x


def kernel(x_prompt, x_sample, mem_prompt, mem_sample, ffn1_norm, ffn1_w_gu, ffn1_w_down, mix_norm,
           mem_norm, w_mem_kv, a_w_in, a_sink, b_w_in, b_q_norm, b_w_q_up, b_kv_norm, b_w_kv_up,
           w_o, ffn2_norm, ffn2_w_gu, ffn2_w_down, final_norm):
    w = {
        "ffn1_norm": ffn1_norm, "ffn2_norm": ffn2_norm, "mix_norm": mix_norm, "mem_norm": mem_norm,
        "w_mem_k": w_mem_kv[:, :, :MEM_WIDTH].astype(BF16),
        "w_mem_vt": jnp.transpose(w_mem_kv[:, :, MEM_WIDTH:], (0, 2, 1)).astype(BF16), "a_sink": a_sink,
        "b_q_norm": b_q_norm, "b_kv_norm": b_kv_norm, "w_o": _prep_w_o(w_o),
        "final_norm": final_norm,
    }
    w["ffn1_w_gu"], w["ffn1_w_down"] = ffn1_w_gu.astype(BF16), ffn1_w_down.astype(BF16)
    w["ffn2_w_gu"], w["ffn2_w_down"] = ffn2_w_gu.astype(BF16), ffn2_w_down.astype(BF16)
    w["a_w_in"], w["a_wvt"] = _prep_a(a_w_in)
    w["b_w_in"], w["b_wq"], w["b_wk"], w["b_wvt"] = _prep_b(b_w_in, b_w_q_up, b_w_kv_up)
    return (_trunk(x_prompt, mem_prompt, w), _trunk(x_sample, mem_sample, w))
```
